```python
import math
import jax, jax.numpy as jnp
from jax import lax
import numpy as np

D_MODEL = 4096
BATCH = 1
SEQ = 8192
DEPTH = 2

EPS = 1e-6
ROPE_THETA = 10000.0

A_HEADS = 16
A_DK = 128
A_DV = 128
A_CONV = 4
A_CHUNK = 64
B_HEADS = 16
B_Q_LORA = 1024
B_KV_LORA = 512
B_NOPE = 128
B_ROPE = 64
B_VDIM = 128
B_QBLOCK = 128
C_HEADS = 16
C_DH = 128
C_PATTERNS = ((128, 1), (512, 4), (2048, 16))
D_FF = 11008
N_EXPERTS = 8
TOP_K = 2
D_EXPERT = 4096

A_QK_W = A_HEADS * A_DK
A_V_W = A_HEADS * A_DV
A_QKV_W = 2 * A_QK_W + A_V_W
B_V_W = B_HEADS * B_VDIM
C_W = C_HEADS * C_DH
IN_SPLITS = (A_QKV_W, A_V_W, A_HEADS, A_HEADS, B_Q_LORA, B_KV_LORA, B_ROPE, C_W, C_W, C_W, D_MODEL, D_MODEL, D_MODEL)
IN_COLS = sum(IN_SPLITS)

kernel_name = "hybrid_gdn_mla_dilated_moe_block"


def rms_norm(x, g):
    xf = x.astype(jnp.float32)
    y = xf * lax.rsqrt(jnp.mean(xf * xf, axis=-1, keepdims=True) + EPS)
    return (y * g.astype(jnp.float32)).astype(x.dtype)


def l2_normalize(x):
    xf = x.astype(jnp.float32)
    return xf * lax.rsqrt(jnp.sum(xf * xf, axis=-1, keepdims=True) + EPS)


def split_columns(x, sizes):
    out, start = [], 0
    for n in sizes:
        out.append(x[..., start:start + n])
        start += n
    return out


def rope_tables(seq, dim):
    inv = ROPE_THETA ** (-jnp.arange(0, dim, 2, dtype=jnp.float32) / dim)
    ang = jnp.arange(seq, dtype=jnp.float32)[:, None] * inv[None, :]
    return jnp.cos(ang), jnp.sin(ang)


def apply_rope(x, cos, sin):
    half = x.shape[-1] // 2
    x1, x2 = x[..., :half], x[..., half:]
    c = cos[None, :, None, :].astype(x.dtype)
    s = sin[None, :, None, :].astype(x.dtype)
    return jnp.concatenate([x1 * c - x2 * s, x2 * c + x1 * s], axis=-1)


def causal_depthwise_conv(x, w):
    K = w.shape[0]
    S = x.shape[1]
    xp = jnp.pad(x, ((0, 0), (K - 1, 0), (0, 0)))
    y = xp[:, 0:S] * w[0]
    for j in range(1, K):
        y = y + xp[:, j:j + S] * w[j]
    return y


def swiglu(h, w1, w3, w2):
    return (jax.nn.silu(h @ w1) * (h @ w3)) @ w2


def gated_delta_rule_chunked(q, k, v, g, beta):
    f32 = jnp.float32
    Bn, S, H, DK = q.shape
    DV = v.shape[-1]
    C = A_CHUNK
    N = S // C

    def chunks(t):
        t = jnp.moveaxis(t, 2, 1)
        return t.reshape((Bn, H, N, C) + t.shape[3:])

    q = chunks(l2_normalize(q)) * (DK ** -0.5)
    k = chunks(l2_normalize(k))
    v = chunks(v.astype(f32))
    g = jnp.cumsum(chunks(g.astype(f32)), axis=-1)
    beta = chunks(beta.astype(f32))

    idx = jnp.arange(C)
    causal = idx[:, None] >= idx[None, :]
    strict = idx[:, None] > idx[None, :]
    decay = jnp.exp(jnp.where(causal, g[..., :, None] - g[..., None, :], -jnp.inf))
    k_beta = k * beta[..., None]
    lower = jnp.where(strict, jnp.einsum('bhnid,bhnjd->bhnij', k_beta, k) * decay, 0.0)
    rhs = jnp.concatenate([v * beta[..., None], k_beta * jnp.exp(g)[..., None]], axis=-1)
    sol = lax.linalg.triangular_solve(lower, rhs, left_side=True, lower=True, unit_diagonal=True)
    u, w = sol[..., :DV], sol[..., DV:]
    qk = jnp.einsum('bhnid,bhnjd->bhnij', q, k) * decay
    g_last = g[..., -1]
    q_dec = q * jnp.exp(g)[..., None]
    k_dec = k * jnp.exp(g_last[..., None] - g)[..., None]

    def step(state, xs):
        q_c, k_c, u_c, w_c, qk_c, gl = xs
        v_new = u_c - jnp.einsum('bhcd,bhde->bhce', w_c, state)
        out = jnp.einsum('bhcd,bhde->bhce', q_c, state) + jnp.einsum('bhij,bhje->bhie', qk_c, v_new)
        state = state * jnp.exp(gl)[..., None, None] + jnp.einsum('bhcd,bhce->bhde', k_c, v_new)
        return state, out

    xs = tuple(jnp.moveaxis(t, 2, 0) for t in (q_dec, k_dec, u, w, qk, g_last))
    state0 = jnp.zeros((Bn, H, DK, DV), f32)
    _, o = lax.scan(step, state0, xs)
    o = jnp.moveaxis(o, 0, 2).reshape(Bn, H, S, DV)
    return jnp.moveaxis(o, 1, 2)


def gdn_branch(qkv, z, a_in, b_in, conv_w, a_log, dt_bias, norm_w):
    Bn, S, _ = qkv.shape
    qkv = jax.nn.silu(causal_depthwise_conv(qkv, conv_w))
    qa, ka, va = split_columns(qkv, (A_QK_W, A_QK_W, A_V_W))
    g = -jnp.exp(a_log.astype(jnp.float32)) * jax.nn.softplus(a_in.astype(jnp.float32) + dt_bias.astype(jnp.float32))
    beta = jax.nn.sigmoid(b_in.astype(jnp.float32))
    o = gated_delta_rule_chunked(qa.reshape(Bn, S, A_HEADS, A_DK), ka.reshape(Bn, S, A_HEADS, A_DK),
                                 va.reshape(Bn, S, A_HEADS, A_DV), g, beta)
    o = rms_norm(o, norm_w) * jax.nn.silu(z.reshape(Bn, S, A_HEADS, A_DV).astype(jnp.float32))
    return o.reshape(Bn, S, A_V_W).astype(qkv.dtype)


def mla_branch(c_q, c_kv, k_rope, q_norm, w_uq, kv_norm, w_ukv, cos, sin):
    Bn, S, _ = c_q.shape
    q = (rms_norm(c_q, q_norm) @ w_uq).reshape(Bn, S, B_HEADS, B_NOPE + B_ROPE)
    kv = (rms_norm(c_kv, kv_norm) @ w_ukv).reshape(Bn, S, B_HEADS, B_NOPE + B_VDIM)
    q_nope, q_pe = q[..., :B_NOPE], apply_rope(q[..., B_NOPE:], cos, sin)
    k_nope, v = kv[..., :B_NOPE], kv[..., B_NOPE:]
    k_pe = apply_rope(k_rope[:, :, None, :], cos, sin)[:, :, 0]
    scale = (B_NOPE + B_ROPE) ** -0.5
    nb = S // B_QBLOCK
    qn_b = q_nope.reshape(Bn, nb, B_QBLOCK, B_HEADS, B_NOPE).transpose(1, 0, 2, 3, 4)
    qp_b = q_pe.reshape(Bn, nb, B_QBLOCK, B_HEADS, B_ROPE).transpose(1, 0, 2, 3, 4)
    kpos = jnp.arange(S)

    def block(args):
        qn, qp, start = args
        s = (jnp.einsum('bqhd,bkhd->bhqk', qn, k_nope) + jnp.einsum('bqhd,bkd->bhqk', qp, k_pe)).astype(jnp.float32) * scale
        qpos = start + jnp.arange(B_QBLOCK)
        s = jnp.where(kpos[None, :] <= qpos[:, None], s, -jnp.inf)
        p = jax.nn.softmax(s, axis=-1).astype(v.dtype)
        return jnp.einsum('bhqk,bkhd->bqhd', p, v)

    o = lax.map(block, (qn_b, qp_b, jnp.arange(nb) * B_QBLOCK))
    return o.transpose(1, 0, 2, 3, 4).reshape(Bn, S, B_V_W)


def dilated_pattern(q, k, v, window, dilation):
    Bn, S, H, D = q.shape
    L = S // dilation
    blk = window // dilation
    nb = -(-L // blk)
    Lp = nb * blk

    def to_blocks(t):
        t = t.reshape(Bn, L, dilation, H, D).transpose(0, 2, 1, 3, 4)
        t = jnp.pad(t, ((0, 0), (0, 0), (0, Lp - L), (0, 0), (0, 0)))
        return t.reshape(Bn, dilation, nb, blk, H, D)

    def with_prev(t):
        prev = jnp.concatenate([jnp.zeros_like(t[:, :, :1]), t[:, :, :-1]], axis=2)
        return jnp.concatenate([prev, t], axis=3)

    qb = to_blocks(q)
    kc = with_prev(to_blocks(k))
    vc = with_prev(to_blocks(v))
    s = jnp.einsum('brnqhd,brnkhd->brnqhk', qb, kc).astype(jnp.float32)
    qi = jnp.arange(blk)[:, None]
    kj = jnp.arange(2 * blk)[None, :]
    dist = qi + blk - kj
    in_band = (dist >= 0) & (dist <= blk)
    first = (jnp.arange(nb)[:, None, None] > 0) | (kj[None] >= blk)
    valid = (in_band[None] & first)[:, :, None, :]
    s = jnp.where(valid, s, -jnp.inf)
    m = jnp.max(s, axis=-1)
    p = jnp.exp(s - m[..., None])
    l = jnp.sum(p, axis=-1)
    o = jnp.einsum('brnqhk,brnkhd->brnqhd', p.astype(v.dtype), vc).astype(jnp.float32)

    def back(t):
        t = t.reshape((Bn, dilation, Lp) + t.shape[4:])[:, :, :L]
        return jnp.moveaxis(t, 1, 2).reshape((Bn, S) + t.shape[3:])

    return back(m), back(l), back(o)


def dilated_branch(qc, kc, vc, cos, sin):
    Bn, S, _ = qc.shape
    q = apply_rope(qc.reshape(Bn, S, C_HEADS, C_DH), cos, sin) * (C_DH ** -0.5)
    k = apply_rope(kc.reshape(Bn, S, C_HEADS, C_DH), cos, sin)
    v = vc.reshape(Bn, S, C_HEADS, C_DH)
    results = [dilated_pattern(q, k, v, w, d) for (w, d) in C_PATTERNS]
    ms = jnp.stack([r[0] for r in results])
    ls = jnp.stack([r[1] for r in results])
    os_ = jnp.stack([r[2] for r in results])
    e = jnp.exp(ms - jnp.max(ms, axis=0, keepdims=True))
    out = jnp.sum(e[..., None] * os_, axis=0) / jnp.sum(e * ls, axis=0)[..., None]
    return out.reshape(Bn, S, C_W).astype(qc.dtype)


def hybrid_mixer(h, w_in, conv_w, a_log, dt_bias, gdn_norm, q_norm, w_uq, kv_norm, w_ukv,
                 w_branch_a, w_branch_b, w_branch_c, w_out, cos_b, sin_b, cos_c, sin_c):
    proj = h @ w_in
    (qkv_a, z_a, a_in, b_in, c_q, c_kv, k_rope, q_c, k_c, v_c,
     gate_a, gate_b, gate_c) = split_columns(proj, IN_SPLITS)
    y_a = gdn_branch(qkv_a, z_a, a_in, b_in, conv_w, a_log, dt_bias, gdn_norm) @ w_branch_a
    y_b = mla_branch(c_q, c_kv, k_rope, q_norm, w_uq, kv_norm, w_ukv, cos_b, sin_b) @ w_branch_b
    y_c = dilated_branch(q_c, k_c, v_c, cos_c, sin_c) @ w_branch_c
    merged = jax.nn.sigmoid(gate_a) * y_a + jax.nn.sigmoid(gate_b) * y_b + jax.nn.sigmoid(gate_c) * y_c
    return merged @ w_out


def moe_ffn(h, router_w, w1, w3, w2):
    logits = (h @ router_w).astype(jnp.float32)
    top_vals, top_idx = lax.top_k(logits, TOP_K)
    gates = jax.nn.softmax(top_vals, axis=-1)
    combine = jnp.sum(jax.nn.one_hot(top_idx, N_EXPERTS, dtype=jnp.float32) * gates[..., None], axis=-2)
    combine = combine.astype(h.dtype)
    y = jnp.zeros_like(h)
    for e in range(N_EXPERTS):
        y = y + combine[..., e:e + 1] * swiglu(h, w1[e], w3[e], w2[e])
    return y


def setup_inputs(seed: int = 0) -> dict:
    key = jax.random.key(seed)
    ks = jax.random.split(key, 24)
    f32 = jnp.float32
    n_dense = (DEPTH + 1) // 2
    n_moe = DEPTH // 2

    def normal(k, shape, scale):
        return jax.random.normal(k, shape, f32) * scale

    def gain(k, shape):
        return 1.0 + 0.02 * jax.random.normal(k, shape, f32)

    dt = jnp.exp(jax.random.uniform(ks[4], (DEPTH, A_HEADS), f32, math.log(1e-3), math.log(1e-1)))
    return {
        "x": jax.random.normal(ks[0], (BATCH, SEQ, D_MODEL), f32),
        "attn_norm": gain(ks[1], (DEPTH, D_MODEL)),
        "w_in": normal(ks[2], (DEPTH, D_MODEL, IN_COLS), D_MODEL ** -0.5),
        "conv_w": normal(ks[3], (DEPTH, A_CONV, A_QKV_W), A_CONV ** -0.5),
        "gdn_a_log": jnp.log(jax.random.uniform(ks[5], (DEPTH, A_HEADS), f32, 1.0, 16.0)),
        "gdn_dt_bias": dt + jnp.log(-jnp.expm1(-dt)),
        "gdn_norm": gain(ks[6], (DEPTH, A_DV)),
        "mla_q_norm": gain(ks[7], (DEPTH, B_Q_LORA)),
        "mla_w_uq": normal(ks[8], (DEPTH, B_Q_LORA, B_HEADS * (B_NOPE + B_ROPE)), B_Q_LORA ** -0.5),
        "mla_kv_norm": gain(ks[9], (DEPTH, B_KV_LORA)),
        "mla_w_ukv": normal(ks[10], (DEPTH, B_KV_LORA, B_HEADS * (B_NOPE + B_VDIM)), B_KV_LORA ** -0.5),
        "w_branch_a": normal(ks[11], (DEPTH, A_V_W, D_MODEL), A_V_W ** -0.5),
        "w_branch_b": normal(ks[12], (DEPTH, B_V_W, D_MODEL), B_V_W ** -0.5),
        "w_branch_c": normal(ks[13], (DEPTH, C_W, D_MODEL), C_W ** -0.5),
        "w_out": normal(ks[14], (DEPTH, D_MODEL, D_MODEL), D_MODEL ** -0.5),
        "ffn_norm": gain(ks[15], (DEPTH, D_MODEL)),
        "dense_w1": normal(ks[16], (n_dense, D_MODEL, D_FF), D_MODEL ** -0.5),
        "dense_w3": normal(ks[17], (n_dense, D_MODEL, D_FF), D_MODEL ** -0.5),
        "dense_w2": normal(ks[18], (n_dense, D_FF, D_MODEL), D_FF ** -0.5),
        "router_w": normal(ks[19], (n_moe, D_MODEL, N_EXPERTS), D_MODEL ** -0.5),
        "moe_w1": normal(ks[20], (n_moe, N_EXPERTS, D_MODEL, D_EXPERT), D_MODEL ** -0.5),
        "moe_w3": normal(ks[21], (n_moe, N_EXPERTS, D_MODEL, D_EXPERT), D_MODEL ** -0.5),
        "moe_w2": normal(ks[22], (n_moe, N_EXPERTS, D_EXPERT, D_MODEL), D_EXPERT ** -0.5),
        "final_norm": gain(ks[23], (D_MODEL,)),
    }


def reference(x, attn_norm, w_in, conv_w, gdn_a_log, gdn_dt_bias, gdn_norm, mla_q_norm, mla_w_uq,
              mla_kv_norm, mla_w_ukv, w_branch_a, w_branch_b, w_branch_c, w_out, ffn_norm,
              dense_w1, dense_w3, dense_w2, router_w, moe_w1, moe_w3, moe_w2, final_norm):
    S = x.shape[1]
    cos_b, sin_b = rope_tables(S, B_ROPE)
    cos_c, sin_c = rope_tables(S, C_DH)
    for layer in range(DEPTH):
        h = rms_norm(x, attn_norm[layer])
        x = x + hybrid_mixer(h, w_in[layer], conv_w[layer], gdn_a_log[layer], gdn_dt_bias[layer], gdn_norm[layer],
                             mla_q_norm[layer], mla_w_uq[layer], mla_kv_norm[layer], mla_w_ukv[layer],
                             w_branch_a[layer], w_branch_b[layer], w_branch_c[layer], w_out[layer],
                             cos_b, sin_b, cos_c, sin_c)
        h = rms_norm(x, ffn_norm[layer])
        i = layer // 2
        if layer % 2 == 0:
            x = x + swiglu(h, dense_w1[i], dense_w3[i], dense_w2[i])
        else:
            x = x + moe_ffn(h, router_w[i], moe_w1[i], moe_w3[i], moe_w2[i])
    return rms_norm(x, final_norm)
```

```python
import functools
import math

import jax
import jax.numpy as jnp
from jax import lax
from jax.experimental import pallas as pl
from jax.experimental.pallas import tpu as pltpu

F32 = jnp.float32
BF16 = jnp.bfloat16

EPS = 1e-6
ROPE_THETA = 10000.0
NEG = -1e30

LANES = 128
VMEM_LIMIT_BYTES = 56 * 1024 * 1024

D_MODEL = 4096
DEPTH = 2
A_HEADS, A_DK, A_DV, A_CONV, A_CHUNK = 16, 128, 128, 4, 64
B_HEADS, B_Q_LORA, B_KV_LORA, B_NOPE, B_ROPE, B_VDIM = 16, 1024, 512, 128, 64, 128
C_HEADS, C_DH = 16, 128
C_PATTERNS = ((128, 1), (512, 4), (2048, 16))
D_FF = 11008
N_EXPERTS, TOP_K, D_EXPERT = 8, 2, 4096

A_QK_W = A_HEADS * A_DK
A_V_W = A_HEADS * A_DV
A_QKV_W = 2 * A_QK_W + A_V_W
C_W = C_HEADS * C_DH
OFF_A_IN = A_QKV_W + A_V_W
OFF_B_IN = OFF_A_IN + A_HEADS
OFF_CQ = OFF_B_IN + A_HEADS
OFF_CKV = OFF_CQ + B_Q_LORA
OFF_KROPE = OFF_CKV + B_KV_LORA
OFF_REST = OFF_KROPE + B_ROPE
REST_W = 3 * C_W + 3 * D_MODEL
PS_CQ, PS_CKV, PS_KR, PS_AB, PS_W = 0, 1024, 1536, 1664, 1792

B_QH = 384
B_KH = 256


def _cparams(*sem):
    return pltpu.CompilerParams(dimension_semantics=sem, vmem_limit_bytes=VMEM_LIMIT_BYTES)


def _dot(a, b):
    return jnp.dot(a, b, preferred_element_type=F32)


def _dot_nt(a, b):
    return lax.dot_general(a, b, (((1,), (1,)), ((), ())), preferred_element_type=F32)


def _sigmoid(x):
    return 1.0 / (1.0 + jnp.exp(-x))


def _silu(x):
    return x * _sigmoid(x)


def _softplus(x):
    return jnp.maximum(x, 0.0) + jnp.log(1.0 + jnp.exp(-jnp.abs(x)))


def _rmsnorm_body(x_ref, g_ref, o_ref):
    x = x_ref[...]
    y = x * lax.rsqrt(jnp.mean(x * x, axis=-1, keepdims=True) + EPS)
    o_ref[...] = (y * g_ref[...]).astype(o_ref.dtype)


def rmsnorm(x, g, out_dtype, bm=256):
    M, D = x.shape
    return pl.pallas_call(
        _rmsnorm_body,
        grid=(M // bm,),
        in_specs=[pl.BlockSpec((bm, D), lambda i: (i, 0)), pl.BlockSpec((1, D), lambda i: (0, 0))],
        out_specs=pl.BlockSpec((bm, D), lambda i: (i, 0)),
        out_shape=jax.ShapeDtypeStruct((M, D), out_dtype),
        compiler_params=_cparams("parallel"),
        name="rmsnorm",
    )(x, g.reshape(1, D))


def _add_rmsnorm_body(x_ref, y_ref, g_ref, xo_ref, ho_ref):
    x = x_ref[...] + y_ref[...].astype(F32)
    xo_ref[...] = x
    y = x * lax.rsqrt(jnp.mean(x * x, axis=-1, keepdims=True) + EPS)
    ho_ref[...] = (y * g_ref[...]).astype(ho_ref.dtype)


def add_rmsnorm(x, y, g, out_dtype, bm=256):
    M, D = x.shape
    row = pl.BlockSpec((bm, D), lambda i: (i, 0))
    return pl.pallas_call(
        _add_rmsnorm_body,
        grid=(M // bm,),
        in_specs=[row, row, pl.BlockSpec((1, D), lambda i: (0, 0))],
        out_specs=[row, row],
        out_shape=[jax.ShapeDtypeStruct((M, D), F32), jax.ShapeDtypeStruct((M, D), out_dtype)],
        compiler_params=_cparams("parallel"),
        name="add_rmsnorm",
    )(x, y, g.reshape(1, D))


def _add_rmsnorm_final_body(x_ref, y_ref, g_ref, o_ref):
    x = x_ref[...] + y_ref[...].astype(F32)
    y = x * lax.rsqrt(jnp.mean(x * x, axis=-1, keepdims=True) + EPS)
    o_ref[...] = y * g_ref[...]


def add_rmsnorm_final(x, y, g, bm=256):
    M, D = x.shape
    row = pl.BlockSpec((bm, D), lambda i: (i, 0))
    return pl.pallas_call(
        _add_rmsnorm_final_body,
        grid=(M // bm,),
        in_specs=[row, row, pl.BlockSpec((1, D), lambda i: (0, 0))],
        out_specs=row,
        out_shape=jax.ShapeDtypeStruct((M, D), F32),
        compiler_params=_cparams("parallel"),
        name="add_rmsnorm_final",
    )(x, y, g.reshape(1, D))


def _mm_body(x_ref, w_ref, o_ref, wbf_ref):
    @pl.when(pl.program_id(1) == 0)
    def _():
        wbf_ref[...] = w_ref[...].astype(BF16)

    o_ref[...] = _dot(x_ref[...], wbf_ref[...]).astype(o_ref.dtype)


def matmul(x, w, n_cols, out_dtype, bm, bn):
    M, K = x.shape
    return pl.pallas_call(
        _mm_body,
        grid=(n_cols // bn, M // bm),
        in_specs=[pl.BlockSpec((bm, K), lambda j, i: (i, 0)), pl.BlockSpec((K, bn), lambda j, i: (0, j))],
        out_specs=pl.BlockSpec((bm, bn), lambda j, i: (i, j)),
        out_shape=jax.ShapeDtypeStruct((M, n_cols), out_dtype),
        scratch_shapes=[pltpu.VMEM((K, bn), BF16)],
        compiler_params=_cparams("parallel", "arbitrary"),
        name="matmul",
    )(x, w)


GDN_ROWS = 256
GDN_HB = 4
GDN_PREV = 16


def _gdn_body(q_ref, k_ref, v_ref, qp_ref, kp_ref, vp_ref, z_ref, ab_ref, cwq_ref, cwk_ref, cwv_ref,
              alog_ref, dtb_ref, nw_ref, o_ref, state_ref):
    hb = pl.program_id(0)
    i = pl.program_id(1)
    R = GDN_ROWS
    C = A_CHUNK
    NC = R // C

    @pl.when(i == 0)
    def _():
        state_ref[...] = jnp.zeros_like(state_ref)

    has_prev = (i > 0).astype(F32)
    lane = lax.broadcasted_iota(jnp.int32, (R, LANES), 1)
    row = lax.broadcasted_iota(jnp.int32, (R, LANES), 0)
    rin = row & (C - 1)
    ri = lax.broadcasted_iota(jnp.int32, (R, R), 0)
    ci = lax.broadcasted_iota(jnp.int32, (R, R), 1)
    same = (ri // C) == (ci // C)
    incl = same & (ri >= ci)
    strict = same & (ri > ci)

    ab = ab_ref[...]
    g_all = -jnp.exp(alog_ref[...]) * _softplus(ab + dtb_ref[...])
    beta_all = _sigmoid(ab)
    nw = nw_ref[...]

    def conv_silu(cur_ref, prev_ref, cw_ref, sl):
        cur = cur_ref[:, sl].astype(F32)
        prev = prev_ref[:, sl].astype(F32) * has_prev
        full = jnp.concatenate([prev, cur], axis=0)
        cw = cw_ref[:, sl]
        y = cw[A_CONV - 1:A_CONV, :] * cur
        for s in range(1, A_CONV):
            y = y + cw[A_CONV - 1 - s:A_CONV - s, :] * pltpu.roll(full, s, axis=0)[GDN_PREV:, :]
        return _silu(y)

    for hh in range(GDN_HB):
        h = hb * GDN_HB + hh
        sl = slice(hh * LANES, (hh + 1) * LANES)
        g = jnp.broadcast_to(jnp.sum(jnp.where(lane == h, g_all, 0.0), axis=1, keepdims=True), (R, LANES))
        beta = jnp.broadcast_to(
            jnp.sum(jnp.where(lane == h + A_HEADS, beta_all, 0.0), axis=1, keepdims=True), (R, LANES))
        gc = g
        s = 1
        while s < C:
            gc = gc + jnp.where(rin >= s, pltpu.roll(gc, s, axis=0), 0.0)
            s *= 2
        g_last = jnp.broadcast_to(jnp.sum(g.reshape(NC, C, LANES), axis=1, keepdims=True),
                                  (NC, C, LANES)).reshape(R, LANES)

        q = conv_silu(q_ref, qp_ref, cwq_ref, sl)
        k = conv_silu(k_ref, kp_ref, cwk_ref, sl)
        v = conv_silu(v_ref, vp_ref, cwv_ref, sl)
        qn = q * lax.rsqrt(jnp.sum(q * q, axis=-1, keepdims=True) + EPS) * (A_DK ** -0.5)
        kn = k * lax.rsqrt(jnp.sum(k * k, axis=-1, keepdims=True) + EPS)
        kb = kn * beta
        eg = jnp.exp(gc)

        gc_t = gc.T
        diff = jnp.concatenate([gc, gc], axis=1) - jnp.concatenate([gc_t, gc_t], axis=0)
        decay = jnp.exp(jnp.where(incl, diff, NEG))
        kn_b = kn.astype(BF16)
        lower = jnp.where(strict, _dot_nt(kb.astype(BF16), kn_b) * decay, 0.0)
        qk = _dot_nt(qn.astype(BF16), kn_b) * decay

        xp = -lower
        qmat = xp
        for _ in range(int(math.log2(C)) - 1):
            xb = xp.astype(BF16)
            xp = _dot(xb, xb)
            qmat = qmat + xp + _dot(qmat.astype(BF16), xp.astype(BF16))
        rhs = jnp.concatenate([v * beta, kb * eg], axis=1)
        sol = rhs + _dot(qmat.astype(BF16), rhs.astype(BF16))
        u = sol[:, :A_DV]
        w = sol[:, A_DV:]
        q_dec = (qn * eg).astype(BF16)
        k_dec = kn * jnp.exp(g_last - gc)
        w_b = w.astype(BF16)

        state = state_ref[hh]
        outs = []
        for c in range(NC):
            rs = slice(c * C, (c + 1) * C)
            sb = state.astype(BF16)
            v_new = u[rs] - _dot(w_b[rs], sb)
            v_new_b = v_new.astype(BF16)
            qk_c = qk[rs, :][:, rs].astype(BF16)
            outs.append(_dot(q_dec[rs], sb) + _dot(qk_c, v_new_b))
            state = state * jnp.exp(g_last[c * C:c * C + 1, :]) + _dot(k_dec[rs].T.astype(BF16), v_new_b)
        state_ref[hh] = state
        out = jnp.concatenate(outs, axis=0)
        out = out * lax.rsqrt(jnp.mean(out * out, axis=-1, keepdims=True) + EPS) * nw
        o_ref[:, sl] = (out * _silu(z_ref[:, sl].astype(F32))).astype(o_ref.dtype)


def gdn_branch(p_a, p_s, conv_w, a_log, dt_bias, norm_w):
    S = p_a.shape[0]
    R, W = GDN_ROWS, GDN_HB * LANES
    nq = A_QK_W // W
    pad = lambda a: jnp.pad(a.astype(F32), (0, LANES - A_HEADS)).reshape(1, LANES)

    def cur(off):
        return pl.BlockSpec((R, W), lambda hb, i: (i, off + hb))

    def prev(off):
        return pl.BlockSpec((GDN_PREV, W), lambda hb, i: (jnp.maximum(i * (R // GDN_PREV) - 1, 0), off + hb))

    def cw(off):
        return pl.BlockSpec((A_CONV, W), lambda hb, i: (0, off + hb))

    vec = pl.BlockSpec((1, LANES), lambda hb, i: (0, 0))
    return pl.pallas_call(
        _gdn_body,
        grid=(A_HEADS // GDN_HB, S // R),
        in_specs=[cur(0), cur(nq), cur(2 * nq), prev(0), prev(nq), prev(2 * nq), cur(3 * nq),
                  pl.BlockSpec((R, LANES), lambda hb, i: (i, PS_AB // LANES)),
                  cw(0), cw(nq), cw(2 * nq), vec, vec, vec],
        out_specs=pl.BlockSpec((R, W), lambda hb, i: (i, hb)),
        out_shape=jax.ShapeDtypeStruct((S, A_V_W), BF16),
        scratch_shapes=[pltpu.VMEM((GDN_HB, A_DK, A_DV), F32)],
        compiler_params=_cparams("parallel", "arbitrary"),
        name="gdn",
    )(p_a, p_a, p_a, p_a, p_a, p_a, p_a, p_s, conv_w, conv_w, conv_w, pad(a_log), pad(dt_bias),
      norm_w.astype(F32).reshape(1, LANES))


MLA_HG = 4


def _mla_prep_body(cq_ref, ckv_ref, kr_ref, qn_ref, kvn_ref, wq_ref, wkv_ref, cs_ref, qo_ref, ko_ref, vo_ref):
    scale = (B_NOPE + B_ROPE) ** -0.5
    cq = cq_ref[...]
    cq = (cq * lax.rsqrt(jnp.mean(cq * cq, axis=-1, keepdims=True) + EPS) * qn_ref[...]).astype(BF16)
    ckv = ckv_ref[...]
    ckv = (ckv * lax.rsqrt(jnp.mean(ckv * ckv, axis=-1, keepdims=True) + EPS) * kvn_ref[...]).astype(BF16)
    rq = _dot(cq, wq_ref[...])
    rkv = _dot(ckv, wkv_ref[...])
    cs = cs_ref[...]
    sc = pltpu.roll(cs, B_ROPE, axis=1)
    prod = kr_ref[...] * cs
    lane = lax.broadcasted_iota(jnp.int32, prod.shape, 1)
    k_pe = jnp.where(lane < B_ROPE, prod + pltpu.roll(prod, B_ROPE, axis=1), 0.0).astype(BF16)
    for hh in range(MLA_HG):
        qb = hh * B_QH
        roped = rq[:, qb + LANES:qb + 2 * LANES] * cs + rq[:, qb + 2 * LANES:qb + 3 * LANES] * sc
        qo_ref[:, hh * B_KH:hh * B_KH + LANES] = (rq[:, qb:qb + LANES] * scale).astype(BF16)
        qo_ref[:, hh * B_KH + LANES:(hh + 1) * B_KH] = (roped * scale).astype(BF16)
        ko_ref[:, hh * B_KH:hh * B_KH + LANES] = rkv[:, hh * 2 * LANES:hh * 2 * LANES + LANES].astype(BF16)
        ko_ref[:, hh * B_KH + LANES:(hh + 1) * B_KH] = k_pe
        vo_ref[:, hh * LANES:(hh + 1) * LANES] = rkv[:, hh * 2 * LANES + LANES:(hh + 1) * 2 * LANES].astype(BF16)


def mla_prep(p_s, q_norm, kv_norm, wq_ext, wkv, cs_b, bm=512):
    S = p_s.shape[0]
    HG = MLA_HG
    return pl.pallas_call(
        _mla_prep_body,
        grid=(B_HEADS // HG, S // bm),
        in_specs=[
            pl.BlockSpec((bm, B_Q_LORA), lambda g, i: (i, PS_CQ // B_Q_LORA)),
            pl.BlockSpec((bm, B_KV_LORA), lambda g, i: (i, PS_CKV // B_KV_LORA)),
            pl.BlockSpec((bm, LANES), lambda g, i: (i, PS_KR // LANES)),
            pl.BlockSpec((1, B_Q_LORA), lambda g, i: (0, 0)),
            pl.BlockSpec((1, B_KV_LORA), lambda g, i: (0, 0)),
            pl.BlockSpec((B_Q_LORA, HG * B_QH), lambda g, i: (0, g)),
            pl.BlockSpec((B_KV_LORA, HG * 2 * LANES), lambda g, i: (0, g)),
            pl.BlockSpec((bm, LANES), lambda g, i: (i, 0)),
        ],
        out_specs=[
            pl.BlockSpec((bm, HG * B_KH), lambda g, i: (i, g)),
            pl.BlockSpec((bm, HG * B_KH), lambda g, i: (i, g)),
            pl.BlockSpec((bm, HG * LANES), lambda g, i: (i, g)),
        ],
        out_shape=[
            jax.ShapeDtypeStruct((S, B_HEADS * B_KH), BF16),
            jax.ShapeDtypeStruct((S, B_HEADS * B_KH), BF16),
            jax.ShapeDtypeStruct((S, B_HEADS * B_VDIM), BF16),
        ],
        compiler_params=_cparams("parallel", "parallel"),
        name="mla_prep",
    )(p_s, p_s, p_s, q_norm.reshape(1, -1), kv_norm.reshape(1, -1), wq_ext, wkv, cs_b)


def _flash_body(q_ref, k_ref, v_ref, o_ref, m_ref, l_ref, acc_ref, *, blk):
    qi = pl.program_id(1)
    kj = pl.program_id(2)

    @pl.when(kj == 0)
    def _():
        m_ref[...] = jnp.full_like(m_ref, NEG)
        l_ref[...] = jnp.zeros_like(l_ref)
        acc_ref[...] = jnp.zeros_like(acc_ref)

    def step(masked):
        s = _dot_nt(q_ref[...], k_ref[...])
        if masked:
            r = lax.broadcasted_iota(jnp.int32, (blk, blk), 0)
            c = lax.broadcasted_iota(jnp.int32, (blk, blk), 1)
            s = jnp.where(c <= r, s, NEG)
        m_prev = m_ref[...]
        m_new = jnp.maximum(m_prev, jnp.max(s, axis=1, keepdims=True))
        p = jnp.exp(s - m_new)
        alpha = jnp.exp(m_prev - m_new)
        l_ref[...] = alpha * l_ref[...] + jnp.sum(p, axis=1, keepdims=True)
        acc_ref[...] = alpha * acc_ref[...] + _dot(p.astype(BF16), v_ref[...])
        m_ref[...] = m_new

    @pl.when(kj < qi)
    def _():
        step(False)

    @pl.when(kj == qi)
    def _():
        step(True)
        o_ref[...] = (acc_ref[...] / l_ref[...]).astype(o_ref.dtype)


def mla_attention(q_cat, k_cat, v, blk=1024):
    S = v.shape[0]
    n = S // blk
    return pl.pallas_call(
        functools.partial(_flash_body, blk=blk),
        grid=(B_HEADS, n, n),
        in_specs=[
            pl.BlockSpec((blk, B_KH), lambda h, qi, kj: (qi, h)),
            pl.BlockSpec((blk, B_KH), lambda h, qi, kj: (jnp.minimum(kj, qi), h)),
            pl.BlockSpec((blk, B_VDIM), lambda h, qi, kj: (jnp.minimum(kj, qi), h)),
        ],
        out_specs=pl.BlockSpec((blk, B_VDIM), lambda h, qi, kj: (qi, h)),
        out_shape=jax.ShapeDtypeStruct((S, B_HEADS * B_VDIM), BF16),
        scratch_shapes=[pltpu.VMEM((blk, 1), F32), pltpu.VMEM((blk, 1), F32), pltpu.VMEM((blk, B_VDIM), F32)],
        compiler_params=_cparams("parallel", "parallel", "arbitrary"),
        name="mla_flash",
    )(q_cat, k_cat, v)


def _rope_qk_body(q_ref, k_ref, c_ref, s_ref, qo_ref, ko_ref):
    c = c_ref[...]
    s = s_ref[...]
    scale = C_DH ** -0.5
    for h in range(C_HEADS):
        sl = slice(h * C_DH, (h + 1) * C_DH)
        x = q_ref[:, sl].astype(F32)
        qo_ref[:, sl] = ((x * c + pltpu.roll(x, C_DH // 2, axis=1) * s) * scale).astype(BF16)
        x = k_ref[:, sl].astype(F32)
        ko_ref[:, sl] = (x * c + pltpu.roll(x, C_DH // 2, axis=1) * s).astype(BF16)


def rope_qk(p_r, cos_c, sin_c, bm=256):
    S = p_r.shape[0]
    tab = pl.BlockSpec((bm, C_DH), lambda i: (i, 0))
    return pl.pallas_call(
        _rope_qk_body,
        grid=(S // bm,),
        in_specs=[pl.BlockSpec((bm, C_W), lambda i: (i, 0)), pl.BlockSpec((bm, C_W), lambda i: (i, 1)), tab, tab],
        out_specs=[pl.BlockSpec((bm, C_W), lambda i: (i, 0))] * 2,
        out_shape=[jax.ShapeDtypeStruct((S, C_W), BF16)] * 2,
        compiler_params=_cparams("parallel"),
        name="rope_qk",
    )(p_r, p_r, cos_c, sin_c)


DIL_BLK = 128


def _dilated_body(*refs, has_state, last):
    q_ref, kc_ref, kp_ref, vc_ref, vp_ref = refs[:5]
    refs = refs[5:]
    if has_state:
        m_in, l_in, a_in = refs[:3]
        refs = refs[3:]
    if last:
        (o_ref,) = refs
    else:
        m_out, l_out, a_out = refs
    B = DIL_BLK
    n = pl.program_id(1)
    qi = lax.broadcasted_iota(jnp.int32, (B, 2 * B), 0)
    kj = lax.broadcasted_iota(jnp.int32, (B, 2 * B), 1)
    dist = qi + B - kj
    valid = (dist >= 0) & (dist <= B) & ((kj >= B) | (n > 0))
    for h in range(C_HEADS):
        sl = slice(h * C_DH, (h + 1) * C_DH)
        kk = jnp.concatenate([kp_ref[:, sl], kc_ref[:, sl]], axis=0)
        vv = jnp.concatenate([vp_ref[:, sl], vc_ref[:, sl]], axis=0)
        s = jnp.where(valid, _dot_nt(q_ref[:, sl], kk), NEG)
        m_cur = jnp.max(s, axis=1, keepdims=True)
        if has_state:
            m_old = m_in[:, sl]
            m_new = jnp.maximum(m_old, m_cur)
            p = jnp.exp(s - m_new[:, :1])
            alpha = jnp.exp(m_old - m_new)
            l_new = alpha * l_in[:, sl] + jnp.sum(p, axis=1, keepdims=True)
            acc = alpha * a_in[:, sl] + _dot(p.astype(BF16), vv)
        else:
            m_new = jnp.broadcast_to(m_cur, (B, C_DH))
            p = jnp.exp(s - m_cur)
            l_new = jnp.broadcast_to(jnp.sum(p, axis=1, keepdims=True), (B, C_DH))
            acc = _dot(p.astype(BF16), vv)
        if last:
            o_ref[:, sl] = (acc / l_new).astype(o_ref.dtype)
        else:
            m_out[:, sl] = m_new
            l_out[:, sl] = l_new
            a_out[:, sl] = acc


def dilated_pattern(qr, kr, p_r, dilation, state, last):
    S = qr.shape[0]
    L = S // dilation
    B = DIL_BLK
    nb = L // B
    vcols = REST_W // C_W
    qv = qr.reshape(L, dilation * C_W)
    kv = kr.reshape(L, dilation * C_W)
    pv = p_r.reshape(L, dilation * REST_W)
    cur = lambda r, n: (n, r)
    prv = lambda r, n: (jnp.maximum(n - 1, 0), r)
    blk = (B, C_W)
    in_specs = [pl.BlockSpec(blk, cur), pl.BlockSpec(blk, cur), pl.BlockSpec(blk, prv),
                pl.BlockSpec(blk, lambda r, n: (n, r * vcols + 2)),
                pl.BlockSpec(blk, lambda r, n: (jnp.maximum(n - 1, 0), r * vcols + 2))]
    args = [qv, kv, kv, pv, pv]
    if state is not None:
        in_specs += [pl.BlockSpec(blk, cur)] * 3
        args += [t.reshape(L, dilation * C_W) for t in state]
    if last:
        out_specs = pl.BlockSpec(blk, cur)
        out_shape = jax.ShapeDtypeStruct((L, dilation * C_W), BF16)
    else:
        out_specs = [pl.BlockSpec(blk, cur)] * 3
        out_shape = [jax.ShapeDtypeStruct((L, dilation * C_W), F32)] * 3
    out = pl.pallas_call(
        functools.partial(_dilated_body, has_state=state is not None, last=last),
        grid=(dilation, nb),
        in_specs=in_specs,
        out_specs=out_specs,
        out_shape=out_shape,
        compiler_params=_cparams("parallel", "parallel"),
        name=f"dilated_d{dilation}",
    )(*args)
    if last:
        return out.reshape(S, C_W)
    return [t.reshape(S, C_W) for t in out]


def dilated_branch(p_r, cos_c, sin_c):
    qr, kr = rope_qk(p_r, cos_c, sin_c)
    state = None
    for idx, (window, dilation) in enumerate(C_PATTERNS):
        assert window // dilation == DIL_BLK
        last = idx == len(C_PATTERNS) - 1
        state = dilated_pattern(qr, kr, p_r, dilation, state, last)
    return state


def _merge_body(oa_ref, ob_ref, oc_ref, ga_ref, gb_ref, gc_ref, wa_ref, wb_ref, wc_ref, o_ref, wa_s, wb_s, wc_s):
    @pl.when(pl.program_id(1) == 0)
    def _():
        wa_s[...] = wa_ref[...].astype(BF16)
        wb_s[...] = wb_ref[...].astype(BF16)
        wc_s[...] = wc_ref[...].astype(BF16)

    acc = _sigmoid(ga_ref[...].astype(F32)) * _dot(oa_ref[...], wa_s[...])
    acc = acc + _sigmoid(gb_ref[...].astype(F32)) * _dot(ob_ref[...], wb_s[...])
    acc = acc + _sigmoid(gc_ref[...].astype(F32)) * _dot(oc_ref[...], wc_s[...])
    o_ref[...] = acc.astype(o_ref.dtype)


def merge_branches(o_a, o_b, o_c, p_r, w_a, w_b, w_c, bm=512, bn=512):
    S, K = o_a.shape
    g0 = 3 * C_W // bn
    gstep = D_MODEL // bn
    act = pl.BlockSpec((bm, K), lambda j, i: (i, 0))
    wsp = pl.BlockSpec((K, bn), lambda j, i: (0, j))
    gate = lambda b: pl.BlockSpec((bm, bn), lambda j, i: (i, g0 + b * gstep + j))
    return pl.pallas_call(
        _merge_body,
        grid=(D_MODEL // bn, S // bm),
        in_specs=[act, act, act, gate(0), gate(1), gate(2), wsp, wsp, wsp],
        out_specs=pl.BlockSpec((bm, bn), lambda j, i: (i, j)),
        out_shape=jax.ShapeDtypeStruct((S, D_MODEL), BF16),
        scratch_shapes=[pltpu.VMEM((K, bn), BF16)] * 3,
        compiler_params=_cparams("parallel", "arbitrary"),
        name="merge_branches",
    )(o_a, o_b, o_c, p_r, p_r, p_r, w_a, w_b, w_c)


def _ffn_body(h_ref, w1_ref, w3_ref, w2_ref, o_ref):
    @pl.when(pl.program_id(1) == 0)
    def _():
        o_ref[...] = jnp.zeros_like(o_ref)

    h = h_ref[...]
    a = _dot(h, w1_ref[...].astype(BF16))
    b = _dot(h, w3_ref[...].astype(BF16))
    o_ref[...] += _dot((_silu(a) * b).astype(BF16), w2_ref[...].astype(BF16))


def dense_ffn(h, w1, w3, w2, bm=512, bf=256):
    S, D = h.shape
    F = w1.shape[1]
    return pl.pallas_call(
        _ffn_body,
        grid=(S // bm, F // bf),
        in_specs=[pl.BlockSpec((bm, D), lambda i, f: (i, 0)), pl.BlockSpec((D, bf), lambda i, f: (0, f)),
                  pl.BlockSpec((D, bf), lambda i, f: (0, f)), pl.BlockSpec((bf, D), lambda i, f: (f, 0))],
        out_specs=pl.BlockSpec((bm, D), lambda i, f: (i, 0)),
        out_shape=jax.ShapeDtypeStruct((S, D), F32),
        compiler_params=_cparams("parallel", "arbitrary"),
        name="dense_ffn",
    )(h, w1, w3, w2)


def _router_body(h_ref, rw_ref, c_ref):
    logits = _dot(h_ref[...], rw_ref[...].astype(BF16))
    lane = lax.broadcasted_iota(jnp.int32, logits.shape, 1)
    logits = jnp.where(lane < N_EXPERTS, logits, NEG)
    t1 = jnp.max(logits, axis=1, keepdims=True)
    i1 = jnp.min(jnp.where(logits == t1, lane, LANES), axis=1, keepdims=True)
    rest = jnp.where(lane == i1, NEG, logits)
    t2 = jnp.max(rest, axis=1, keepdims=True)
    i2 = jnp.min(jnp.where(rest == t2, lane, LANES), axis=1, keepdims=True)
    e2 = jnp.exp(t2 - t1)
    g1 = 1.0 / (1.0 + e2)
    c_ref[...] = jnp.where(lane == i1, g1, 0.0) + jnp.where(lane == i2, e2 * g1, 0.0)


def router(h, router_w, bm=512):
    S, D = h.shape
    rw = jnp.pad(router_w, ((0, 0), (0, LANES - N_EXPERTS)))
    return pl.pallas_call(
        _router_body,
        grid=(S // bm,),
        in_specs=[pl.BlockSpec((bm, D), lambda i: (i, 0)), pl.BlockSpec((D, LANES), lambda i: (0, 0))],
        out_specs=pl.BlockSpec((bm, LANES), lambda i: (i, 0)),
        out_shape=jax.ShapeDtypeStruct((S, LANES), F32),
        compiler_params=_cparams("parallel"),
        name="router",
    )(h, rw)


def _moe_body(h_ref, c_ref, w1_ref, w3_ref, w2_ref, o_ref):
    e = pl.program_id(1)

    @pl.when((e == 0) & (pl.program_id(2) == 0))
    def _():
        o_ref[...] = jnp.zeros_like(o_ref)

    c = c_ref[...]
    lane = lax.broadcasted_iota(jnp.int32, c.shape, 1)
    ce = jnp.sum(jnp.where(lane == e, c, 0.0), axis=1, keepdims=True)
    h = h_ref[...]
    a = _dot(h, w1_ref[...].astype(BF16))
    b = _dot(h, w3_ref[...].astype(BF16))
    o_ref[...] += ce * _dot((_silu(a) * b).astype(BF16), w2_ref[...].astype(BF16))


def moe_ffn(h, combine, w1, w3, w2, bm=512, bf=256):
    S, D = h.shape
    E, _, F = w1.shape
    return pl.pallas_call(
        _moe_body,
        grid=(S // bm, E, F // bf),
        in_specs=[pl.BlockSpec((bm, D), lambda i, e, f: (i, 0)),
                  pl.BlockSpec((bm, LANES), lambda i, e, f: (i, 0)),
                  pl.BlockSpec((None, D, bf), lambda i, e, f: (e, 0, f)),
                  pl.BlockSpec((None, D, bf), lambda i, e, f: (e, 0, f)),
                  pl.BlockSpec((None, bf, D), lambda i, e, f: (e, f, 0))],
        out_specs=pl.BlockSpec((bm, D), lambda i, e, f: (i, 0)),
        out_shape=jax.ShapeDtypeStruct((S, D), F32),
        compiler_params=_cparams("parallel", "arbitrary", "arbitrary"),
        name="moe_ffn",
    )(h, combine, w1, w3, w2)


def _rot_cols(w):
    half = w.shape[-1] // 2
    return jnp.concatenate([-w[..., half:], w[..., :half]], axis=-1)


def _rope_tables(seq, dim):
    inv = ROPE_THETA ** (-jnp.arange(0, dim, 2, dtype=F32) / dim)
    ang = jnp.arange(seq, dtype=F32)[:, None] * inv[None, :]
    return jnp.cos(ang), jnp.sin(ang)


def _small_weight(w_in):
    kr = w_in[:, OFF_KROPE:OFF_KROPE + B_ROPE]
    parts = [w_in[:, OFF_CQ:OFF_CQ + B_Q_LORA], w_in[:, OFF_CKV:OFF_CKV + B_KV_LORA], kr, _rot_cols(kr),
             w_in[:, OFF_A_IN:OFF_A_IN + A_HEADS], w_in[:, OFF_B_IN:OFF_B_IN + A_HEADS]]
    w = jnp.concatenate(parts, axis=1)
    return jnp.pad(w, ((0, 0), (0, PS_W - w.shape[1])))


def _mla_weights(w_uq, w_ukv):
    wq = w_uq.reshape(B_Q_LORA, B_HEADS, B_NOPE + B_ROPE)
    pe = wq[..., B_NOPE:]
    zeros = jnp.zeros_like(pe)
    wq_ext = jnp.concatenate([wq[..., :B_NOPE], pe, zeros, _rot_cols(pe), zeros], axis=-1)
    return wq_ext.reshape(B_Q_LORA, B_HEADS * B_QH).astype(BF16), w_ukv.astype(BF16)


def hybrid_mixer(h, layer, p, tables):
    cs_b, cos_c, sin_c = tables
    w_in = p["w_in"][layer]
    p_a = matmul(h, w_in, OFF_A_IN, BF16, bm=1024, bn=512)
    p_s = matmul(h, _small_weight(w_in), PS_W, F32, bm=1024, bn=256)
    p_r = matmul(h, w_in[:, OFF_REST:].astype(BF16), REST_W, BF16, bm=1024, bn=512)
    o_a = gdn_branch(p_a, p_s, p["conv_w"][layer], p["gdn_a_log"][layer], p["gdn_dt_bias"][layer],
                     p["gdn_norm"][layer])
    wq_ext, wkv = _mla_weights(p["mla_w_uq"][layer], p["mla_w_ukv"][layer])
    q_cat, k_cat, v_b = mla_prep(p_s, p["mla_q_norm"][layer], p["mla_kv_norm"][layer], wq_ext, wkv, cs_b)
    o_b = mla_attention(q_cat, k_cat, v_b)
    o_c = dilated_branch(p_r, cos_c, sin_c)
    merged = merge_branches(o_a, o_b, o_c, p_r, p["w_branch_a"][layer], p["w_branch_b"][layer],
                            p["w_branch_c"][layer])
    return matmul(merged, p["w_out"][layer], D_MODEL, F32, bm=1024, bn=512)


def kernel(x, attn_norm, w_in, conv_w, gdn_a_log, gdn_dt_bias, gdn_norm, mla_q_norm, mla_w_uq, mla_kv_norm, mla_w_ukv, w_branch_a, w_branch_b, w_branch_c, w_out, ffn_norm, dense_w1, dense_w3, dense_w2, router_w, moe_w1, moe_w3, moe_w2, final_norm):
    p = dict(w_in=w_in, conv_w=conv_w, gdn_a_log=gdn_a_log, gdn_dt_bias=gdn_dt_bias, gdn_norm=gdn_norm,
             mla_q_norm=mla_q_norm, mla_w_uq=mla_w_uq, mla_kv_norm=mla_kv_norm, mla_w_ukv=mla_w_ukv,
             w_branch_a=w_branch_a, w_branch_b=w_branch_b, w_branch_c=w_branch_c, w_out=w_out)
    Bn, S, D = x.shape
    assert Bn == 1 and D == D_MODEL
    cos_b, sin_b = _rope_tables(S, B_ROPE)
    cos_c, sin_c = _rope_tables(S, C_DH)
    tables = (jnp.concatenate([cos_b, cos_b, sin_b, sin_b], axis=1),
              jnp.concatenate([cos_c, cos_c], axis=1), jnp.concatenate([-sin_c, sin_c], axis=1))
    xr = x.reshape(S, D)
    h = rmsnorm(xr, attn_norm[0], BF16)
    out = None
    for layer in range(DEPTH):
        y = hybrid_mixer(h, layer, p, tables)
        xr, h = add_rmsnorm(xr, y, ffn_norm[layer], BF16)
        i = layer // 2
        if layer % 2 == 0:
            y = dense_ffn(h, dense_w1[i].astype(BF16), dense_w3[i].astype(BF16), dense_w2[i].astype(BF16))
        else:
            combine = router(h, router_w[i])
            y = moe_ffn(h, combine, moe_w1[i].astype(BF16), moe_w3[i].astype(BF16), moe_w2[i].astype(BF16))
        if layer + 1 < DEPTH:
            xr, h = add_rmsnorm(xr, y, attn_norm[layer + 1], BF16)
        else:
            out = add_rmsnorm_final(xr, y, final_norm)
    return out.reshape(Bn, S, D)
```

```python
import functools
import math

import jax
import jax.numpy as jnp
from jax import lax
from jax.experimental import pallas as pl
from jax.experimental.pallas import tpu as pltpu

F32 = jnp.float32
BF16 = jnp.bfloat16

EPS = 1e-6
ROPE_THETA = 10000.0
NEG = -1e30

LANES = 128
VMEM_LIMIT_BYTES = 56 * 1024 * 1024

D_MODEL = 4096
DEPTH = 2
A_HEADS, A_DK, A_DV, A_CONV, A_CHUNK = 16, 128, 128, 4, 64
B_HEADS, B_Q_LORA, B_KV_LORA, B_NOPE, B_ROPE, B_VDIM = 16, 1024, 512, 128, 64, 128
C_HEADS, C_DH = 16, 128
C_PATTERNS = ((128, 1), (512, 4), (2048, 16))
D_FF = 11008
N_EXPERTS, TOP_K, D_EXPERT = 8, 2, 4096

A_QK_W = A_HEADS * A_DK
A_V_W = A_HEADS * A_DV
A_QKV_W = 2 * A_QK_W + A_V_W
C_W = C_HEADS * C_DH
OFF_A_IN = A_QKV_W + A_V_W
OFF_B_IN = OFF_A_IN + A_HEADS
OFF_CQ = OFF_B_IN + A_HEADS
OFF_CKV = OFF_CQ + B_Q_LORA
OFF_KROPE = OFF_CKV + B_KV_LORA
OFF_REST = OFF_KROPE + B_ROPE
REST_W = 3 * C_W + 3 * D_MODEL
PS_CQ, PS_CKV, PS_KR, PS_AB, PS_W = 0, 1024, 1536, 1664, 1792

B_QH = 384
B_KH = 256


def _cparams(*sem):
    return pltpu.CompilerParams(dimension_semantics=sem, vmem_limit_bytes=VMEM_LIMIT_BYTES)


def _dot(a, b):
    return jnp.dot(a, b, preferred_element_type=F32)


def _dot_nt(a, b):
    return lax.dot_general(a, b, (((1,), (1,)), ((), ())), preferred_element_type=F32)


def _sigmoid(x):
    return 1.0 / (1.0 + jnp.exp(-x))


def _silu(x):
    return x * _sigmoid(x)


def _softplus(x):
    return jnp.maximum(x, 0.0) + jnp.log(1.0 + jnp.exp(-jnp.abs(x)))


def _rmsnorm_body(x_ref, g_ref, o_ref):
    x = x_ref[...]
    y = x * lax.rsqrt(jnp.mean(x * x, axis=-1, keepdims=True) + EPS)
    o_ref[...] = (y * g_ref[...]).astype(o_ref.dtype)


def rmsnorm(x, g, out_dtype, bm=256):
    M, D = x.shape
    return pl.pallas_call(
        _rmsnorm_body,
        grid=(M // bm,),
        in_specs=[pl.BlockSpec((bm, D), lambda i: (i, 0)), pl.BlockSpec((1, D), lambda i: (0, 0))],
        out_specs=pl.BlockSpec((bm, D), lambda i: (i, 0)),
        out_shape=jax.ShapeDtypeStruct((M, D), out_dtype),
        compiler_params=_cparams("parallel"),
        name="rmsnorm",
    )(x, g.reshape(1, D))


def _add_rmsnorm_body(x_ref, y_ref, g_ref, xo_ref, ho_ref):
    x = x_ref[...] + y_ref[...].astype(F32)
    xo_ref[...] = x
    y = x * lax.rsqrt(jnp.mean(x * x, axis=-1, keepdims=True) + EPS)
    ho_ref[...] = (y * g_ref[...]).astype(ho_ref.dtype)


def add_rmsnorm(x, y, g, out_dtype, bm=256):
    M, D = x.shape
    row = pl.BlockSpec((bm, D), lambda i: (i, 0))
    return pl.pallas_call(
        _add_rmsnorm_body,
        grid=(M // bm,),
        in_specs=[row, row, pl.BlockSpec((1, D), lambda i: (0, 0))],
        out_specs=[row, row],
        out_shape=[jax.ShapeDtypeStruct((M, D), F32), jax.ShapeDtypeStruct((M, D), out_dtype)],
        compiler_params=_cparams("parallel"),
        name="add_rmsnorm",
    )(x, y, g.reshape(1, D))


def _add_rmsnorm_final_body(x_ref, y_ref, g_ref, o_ref):
    x = x_ref[...] + y_ref[...].astype(F32)
    y = x * lax.rsqrt(jnp.mean(x * x, axis=-1, keepdims=True) + EPS)
    o_ref[...] = y * g_ref[...]


def add_rmsnorm_final(x, y, g, bm=256):
    M, D = x.shape
    row = pl.BlockSpec((bm, D), lambda i: (i, 0))
    return pl.pallas_call(
        _add_rmsnorm_final_body,
        grid=(M // bm,),
        in_specs=[row, row, pl.BlockSpec((1, D), lambda i: (0, 0))],
        out_specs=row,
        out_shape=jax.ShapeDtypeStruct((M, D), F32),
        compiler_params=_cparams("parallel"),
        name="add_rmsnorm_final",
    )(x, y, g.reshape(1, D))


def _mm_body(x_ref, w_ref, o_ref, wbf_ref):
    @pl.when(pl.program_id(1) == 0)
    def _():
        wbf_ref[...] = w_ref[...].astype(BF16)

    o_ref[...] = _dot(x_ref[...], wbf_ref[...]).astype(o_ref.dtype)


def matmul(x, w, n_cols, out_dtype, bm, bn):
    M, K = x.shape
    return pl.pallas_call(
        _mm_body,
        grid=(n_cols // bn, M // bm),
        in_specs=[pl.BlockSpec((bm, K), lambda j, i: (i, 0)), pl.BlockSpec((K, bn), lambda j, i: (0, j))],
        out_specs=pl.BlockSpec((bm, bn), lambda j, i: (i, j)),
        out_shape=jax.ShapeDtypeStruct((M, n_cols), out_dtype),
        scratch_shapes=[pltpu.VMEM((K, bn), BF16)],
        compiler_params=_cparams("parallel", "arbitrary"),
        name="matmul",
    )(x, w)


GDN_ROWS = 256
GDN_HB = 4
GDN_PREV = 16


def _gdn_body(q_ref, k_ref, v_ref, qp_ref, kp_ref, vp_ref, z_ref, ab_ref, cwq_ref, cwk_ref, cwv_ref,
              alog_ref, dtb_ref, nw_ref, o_ref, state_ref):
    hb = pl.program_id(0)
    i = pl.program_id(1)
    R = GDN_ROWS
    C = A_CHUNK
    NC = R // C

    @pl.when(i == 0)
    def _():
        state_ref[...] = jnp.zeros_like(state_ref)

    has_prev = (i > 0).astype(F32)
    lane = lax.broadcasted_iota(jnp.int32, (R, LANES), 1)
    row = lax.broadcasted_iota(jnp.int32, (R, LANES), 0)
    rin = row & (C - 1)
    ri = lax.broadcasted_iota(jnp.int32, (R, R), 0)
    ci = lax.broadcasted_iota(jnp.int32, (R, R), 1)
    same = (ri // C) == (ci // C)
    incl = same & (ri >= ci)
    strict = same & (ri > ci)

    ab = ab_ref[...]
    g_all = -jnp.exp(alog_ref[...]) * _softplus(ab + dtb_ref[...])
    beta_all = _sigmoid(ab)
    nw = nw_ref[...]

    def conv_silu(cur_ref, prev_ref, cw_ref, sl):
        cur = cur_ref[:, sl].astype(F32)
        prev = prev_ref[:, sl].astype(F32) * has_prev
        full = jnp.concatenate([prev, cur], axis=0)
        cw = cw_ref[:, sl]
        y = cw[A_CONV - 1:A_CONV, :] * cur
        for s in range(1, A_CONV):
            y = y + cw[A_CONV - 1 - s:A_CONV - s, :] * pltpu.roll(full, s, axis=0)[GDN_PREV:, :]
        return _silu(y)

    for hh in range(GDN_HB):
        h = hb * GDN_HB + hh
        sl = slice(hh * LANES, (hh + 1) * LANES)
        g = jnp.broadcast_to(jnp.sum(jnp.where(lane == h, g_all, 0.0), axis=1, keepdims=True), (R, LANES))
        beta = jnp.broadcast_to(
            jnp.sum(jnp.where(lane == h + A_HEADS, beta_all, 0.0), axis=1, keepdims=True), (R, LANES))
        gc = g
        s = 1
        while s < C:
            gc = gc + jnp.where(rin >= s, pltpu.roll(gc, s, axis=0), 0.0)
            s *= 2
        g_last = jnp.broadcast_to(jnp.sum(g.reshape(NC, C, LANES), axis=1, keepdims=True),
                                  (NC, C, LANES)).reshape(R, LANES)

        q = conv_silu(q_ref, qp_ref, cwq_ref, sl)
        k = conv_silu(k_ref, kp_ref, cwk_ref, sl)
        v = conv_silu(v_ref, vp_ref, cwv_ref, sl)
        qn = q * lax.rsqrt(jnp.sum(q * q, axis=-1, keepdims=True) + EPS) * (A_DK ** -0.5)
        kn = k * lax.rsqrt(jnp.sum(k * k, axis=-1, keepdims=True) + EPS)
        kb = kn * beta
        eg = jnp.exp(gc)

        gc_t = gc.T
        diff = jnp.concatenate([gc, gc], axis=1) - jnp.concatenate([gc_t, gc_t], axis=0)
        decay = jnp.exp(jnp.where(incl, diff, NEG))
        kn_b = kn.astype(BF16)
        lower = jnp.where(strict, _dot_nt(kb.astype(BF16), kn_b) * decay, 0.0)
        qk = _dot_nt(qn.astype(BF16), kn_b) * decay

        xp = -lower
        qmat = xp
        for _ in range(int(math.log2(C)) - 1):
            xb = xp.astype(BF16)
            xp = _dot(xb, xb)
            qmat = qmat + xp + _dot(qmat.astype(BF16), xp.astype(BF16))
        rhs = jnp.concatenate([v * beta, kb * eg], axis=1)
        sol = rhs + _dot(qmat.astype(BF16), rhs.astype(BF16))
        u = sol[:, :A_DV]
        w = sol[:, A_DV:]
        q_dec = (qn * eg).astype(BF16)
        k_dec = kn * jnp.exp(g_last - gc)
        w_b = w.astype(BF16)

        state = state_ref[hh]
        outs = []
        for c in range(NC):
            rs = slice(c * C, (c + 1) * C)
            sb = state.astype(BF16)
            v_new = u[rs] - _dot(w_b[rs], sb)
            v_new_b = v_new.astype(BF16)
            qk_c = qk[rs, :][:, rs].astype(BF16)
            outs.append(_dot(q_dec[rs], sb) + _dot(qk_c, v_new_b))
            state = state * jnp.exp(g_last[c * C:c * C + 1, :]) + _dot(k_dec[rs].T.astype(BF16), v_new_b)
        state_ref[hh] = state
        out = jnp.concatenate(outs, axis=0)
        out = out * lax.rsqrt(jnp.mean(out * out, axis=-1, keepdims=True) + EPS) * nw
        o_ref[:, sl] = (out * _silu(z_ref[:, sl].astype(F32))).astype(o_ref.dtype)


def gdn_branch(p_a, p_s, conv_w, a_log, dt_bias, norm_w):
    S = p_a.shape[0]
    R, W = GDN_ROWS, GDN_HB * LANES
    nq = A_QK_W // W
    pad = lambda a: jnp.pad(a.astype(F32), (0, LANES - A_HEADS)).reshape(1, LANES)

    def cur(off):
        return pl.BlockSpec((R, W), lambda hb, i: (i, off + hb))

    def prev(off):
        return pl.BlockSpec((GDN_PREV, W), lambda hb, i: (jnp.maximum(i * (R // GDN_PREV) - 1, 0), off + hb))

    def cw(off):
        return pl.BlockSpec((A_CONV, W), lambda hb, i: (0, off + hb))

    vec = pl.BlockSpec((1, LANES), lambda hb, i: (0, 0))
    return pl.pallas_call(
        _gdn_body,
        grid=(A_HEADS // GDN_HB, S // R),
        in_specs=[cur(0), cur(nq), cur(2 * nq), prev(0), prev(nq), prev(2 * nq), cur(3 * nq),
                  pl.BlockSpec((R, LANES), lambda hb, i: (i, PS_AB // LANES)),
                  cw(0), cw(nq), cw(2 * nq), vec, vec, vec],
        out_specs=pl.BlockSpec((R, W), lambda hb, i: (i, hb)),
        out_shape=jax.ShapeDtypeStruct((S, A_V_W), BF16),
        scratch_shapes=[pltpu.VMEM((GDN_HB, A_DK, A_DV), F32)],
        compiler_params=_cparams("parallel", "arbitrary"),
        name="gdn",
    )(p_a, p_a, p_a, p_a, p_a, p_a, p_a, p_s, conv_w, conv_w, conv_w, pad(a_log), pad(dt_bias),
      norm_w.astype(F32).reshape(1, LANES))


MLA_HG = 4


def _mla_prep_body(cq_ref, ckv_ref, kr_ref, qn_ref, kvn_ref, wq_ref, wkv_ref, cs_ref, qo_ref, ko_ref, vo_ref):
    scale = (B_NOPE + B_ROPE) ** -0.5
    cq = cq_ref[...]
    cq = (cq * lax.rsqrt(jnp.mean(cq * cq, axis=-1, keepdims=True) + EPS) * qn_ref[...]).astype(BF16)
    ckv = ckv_ref[...]
    ckv = (ckv * lax.rsqrt(jnp.mean(ckv * ckv, axis=-1, keepdims=True) + EPS) * kvn_ref[...]).astype(BF16)
    rq = _dot(cq, wq_ref[...])
    rkv = _dot(ckv, wkv_ref[...])
    cs = cs_ref[...]
    sc = pltpu.roll(cs, B_ROPE, axis=1)
    prod = kr_ref[...] * cs
    lane = lax.broadcasted_iota(jnp.int32, prod.shape, 1)
    k_pe = jnp.where(lane < B_ROPE, prod + pltpu.roll(prod, B_ROPE, axis=1), 0.0).astype(BF16)
    for hh in range(MLA_HG):
        qb = hh * B_QH
        roped = rq[:, qb + LANES:qb + 2 * LANES] * cs + rq[:, qb + 2 * LANES:qb + 3 * LANES] * sc
        qo_ref[:, hh * B_KH:hh * B_KH + LANES] = (rq[:, qb:qb + LANES] * scale).astype(BF16)
        qo_ref[:, hh * B_KH + LANES:(hh + 1) * B_KH] = (roped * scale).astype(BF16)
        ko_ref[:, hh * B_KH:hh * B_KH + LANES] = rkv[:, hh * 2 * LANES:hh * 2 * LANES + LANES].astype(BF16)
        ko_ref[:, hh * B_KH + LANES:(hh + 1) * B_KH] = k_pe
        vo_ref[:, hh * LANES:(hh + 1) * LANES] = rkv[:, hh * 2 * LANES + LANES:(hh + 1) * 2 * LANES].astype(BF16)


def mla_prep(p_s, q_norm, kv_norm, wq_ext, wkv, cs_b, bm=512):
    S = p_s.shape[0]
    HG = MLA_HG
    return pl.pallas_call(
        _mla_prep_body,
        grid=(B_HEADS // HG, S // bm),
        in_specs=[
            pl.BlockSpec((bm, B_Q_LORA), lambda g, i: (i, PS_CQ // B_Q_LORA)),
            pl.BlockSpec((bm, B_KV_LORA), lambda g, i: (i, PS_CKV // B_KV_LORA)),
            pl.BlockSpec((bm, LANES), lambda g, i: (i, PS_KR // LANES)),
            pl.BlockSpec((1, B_Q_LORA), lambda g, i: (0, 0)),
            pl.BlockSpec((1, B_KV_LORA), lambda g, i: (0, 0)),
            pl.BlockSpec((B_Q_LORA, HG * B_QH), lambda g, i: (0, g)),
            pl.BlockSpec((B_KV_LORA, HG * 2 * LANES), lambda g, i: (0, g)),
            pl.BlockSpec((bm, LANES), lambda g, i: (i, 0)),
        ],
        out_specs=[
            pl.BlockSpec((bm, HG * B_KH), lambda g, i: (i, g)),
            pl.BlockSpec((bm, HG * B_KH), lambda g, i: (i, g)),
            pl.BlockSpec((bm, HG * LANES), lambda g, i: (i, g)),
        ],
        out_shape=[
            jax.ShapeDtypeStruct((S, B_HEADS * B_KH), BF16),
            jax.ShapeDtypeStruct((S, B_HEADS * B_KH), BF16),
            jax.ShapeDtypeStruct((S, B_HEADS * B_VDIM), BF16),
        ],
        compiler_params=_cparams("parallel", "parallel"),
        name="mla_prep",
    )(p_s, p_s, p_s, q_norm.reshape(1, -1), kv_norm.reshape(1, -1), wq_ext, wkv, cs_b)


def _flash_body(q_ref, k_ref, v_ref, o_ref, m_ref, l_ref, acc_ref, *, blk):
    qi = pl.program_id(1)
    kj = pl.program_id(2)

    @pl.when(kj == 0)
    def _():
        m_ref[...] = jnp.full_like(m_ref, NEG)
        l_ref[...] = jnp.zeros_like(l_ref)
        acc_ref[...] = jnp.zeros_like(acc_ref)

    def step(masked):
        s = _dot_nt(q_ref[...], k_ref[...])
        if masked:
            r = lax.broadcasted_iota(jnp.int32, (blk, blk), 0)
            c = lax.broadcasted_iota(jnp.int32, (blk, blk), 1)
            s = jnp.where(c <= r, s, NEG)
        m_prev = m_ref[...]
        m_new = jnp.maximum(m_prev, jnp.max(s, axis=1, keepdims=True))
        p = jnp.exp(s - m_new)
        alpha = jnp.exp(m_prev - m_new)
        l_ref[...] = alpha * l_ref[...] + jnp.sum(p, axis=1, keepdims=True)
        acc_ref[...] = alpha * acc_ref[...] + _dot(p.astype(BF16), v_ref[...])
        m_ref[...] = m_new

    @pl.when(kj < qi)
    def _():
        step(False)

    @pl.when(kj == qi)
    def _():
        step(True)
        o_ref[...] = (acc_ref[...] / l_ref[...]).astype(o_ref.dtype)


def mla_attention(q_cat, k_cat, v, blk=1024):
    S = v.shape[0]
    n = S // blk
    return pl.pallas_call(
        functools.partial(_flash_body, blk=blk),
        grid=(B_HEADS, n, n),
        in_specs=[
            pl.BlockSpec((blk, B_KH), lambda h, qi, kj: (qi, h)),
            pl.BlockSpec((blk, B_KH), lambda h, qi, kj: (jnp.minimum(kj, qi), h)),
            pl.BlockSpec((blk, B_VDIM), lambda h, qi, kj: (jnp.minimum(kj, qi), h)),
        ],
        out_specs=pl.BlockSpec((blk, B_VDIM), lambda h, qi, kj: (qi, h)),
        out_shape=jax.ShapeDtypeStruct((S, B_HEADS * B_VDIM), BF16),
        scratch_shapes=[pltpu.VMEM((blk, 1), F32), pltpu.VMEM((blk, 1), F32), pltpu.VMEM((blk, B_VDIM), F32)],
        compiler_params=_cparams("parallel", "parallel", "arbitrary"),
        name="mla_flash",
    )(q_cat, k_cat, v)


DIL_BLK = 128
DIL_ROWS = 256
DILATIONS = tuple(d for _, d in C_PATTERNS)


def _rope_split_body(q_ref, k_ref, v_ref, c_ref, s_ref, *refs):
    nd = len(DILATIONS)
    outs, (sq, sk, sv) = refs[:3 * nd], refs[3 * nd:]
    c = c_ref[...]
    s = s_ref[...]
    scale = C_DH ** -0.5
    for h in range(C_HEADS):
        sl = slice(h * C_DH, (h + 1) * C_DH)
        x = q_ref[:, sl].astype(F32)
        sq[h] = (x * c + pltpu.roll(x, C_DH // 2, axis=1) * s) * scale
        x = k_ref[:, sl].astype(F32)
        sk[h] = x * c + pltpu.roll(x, C_DH // 2, axis=1) * s
        sv[h] = v_ref[:, sl].astype(F32)
        for di, d in enumerate(DILATIONS):
            for src, dst in zip((sq, sk, sv), outs[3 * di:3 * di + 3]):
                for r in range(d):
                    rows = src[h] if d == 1 else src[h, pl.ds(r, DIL_ROWS // d, stride=d), :]
                    dst[r, :, sl] = rows.astype(BF16)


def rope_split(p_r, cos_c, sin_c):
    S = p_r.shape[0]
    bm = DIL_ROWS
    tab = pl.BlockSpec((bm, C_DH), lambda i: (i, 0))
    col = lambda j: pl.BlockSpec((bm, C_W), lambda i: (i, j))
    out_specs, out_shape = [], []
    for d in DILATIONS:
        out_specs += [pl.BlockSpec((d, bm // d, C_W), lambda i: (0, i, 0))] * 3
        out_shape += [jax.ShapeDtypeStruct((d, S // d, C_W), BF16)] * 3
    outs = pl.pallas_call(
        _rope_split_body,
        grid=(S // bm,),
        in_specs=[col(0), col(1), col(2), tab, tab],
        out_specs=out_specs,
        out_shape=out_shape,
        scratch_shapes=[pltpu.VMEM((C_HEADS, bm, C_DH), F32)] * 3,
        compiler_params=_cparams("parallel"),
        name="rope_split",
    )(p_r, p_r, p_r, cos_c, sin_c)
    return [outs[3 * i:3 * i + 3] for i in range(len(DILATIONS))]


def _dilated_body(q_ref, kc_ref, kp_ref, vc_ref, vp_ref, o_ref, lse_ref):
    B = DIL_BLK
    n = pl.program_id(1)
    qi = lax.broadcasted_iota(jnp.int32, (B, 2 * B), 0)
    kj = lax.broadcasted_iota(jnp.int32, (B, 2 * B), 1)
    dist = qi + B - kj
    valid = (dist >= 0) & (dist <= B) & ((kj >= B) | (n > 0))
    for h in range(C_HEADS):
        sl = slice(h * C_DH, (h + 1) * C_DH)
        kk = jnp.concatenate([kp_ref[:, sl], kc_ref[:, sl]], axis=0)
        vv = jnp.concatenate([vp_ref[:, sl], vc_ref[:, sl]], axis=0)
        s = jnp.where(valid, _dot_nt(q_ref[:, sl], kk), NEG)
        m = jnp.max(s, axis=1, keepdims=True)
        p = jnp.exp(s - m)
        l = jnp.sum(p, axis=1, keepdims=True)
        o_ref[:, sl] = (_dot(p.astype(BF16), vv) / l).astype(o_ref.dtype)
        lse_ref[:, sl] = jnp.broadcast_to(m + jnp.log(l), (B, C_DH))


def dilated_pattern(q, k, v):
    d, L, W = q.shape
    B = DIL_BLK
    cur = pl.BlockSpec((None, B, W), lambda r, n: (r, n, 0))
    prv = pl.BlockSpec((None, B, W), lambda r, n: (r, jnp.maximum(n - 1, 0), 0))
    return pl.pallas_call(
        _dilated_body,
        grid=(d, L // B),
        in_specs=[cur, cur, prv, cur, prv],
        out_specs=[cur, cur],
        out_shape=[jax.ShapeDtypeStruct((d, L, W), BF16), jax.ShapeDtypeStruct((d, L, W), F32)],
        compiler_params=_cparams("parallel", "parallel"),
        name=f"dilated_d{d}",
    )(q, k, k, v, v)


def _dilated_merge_body(*refs):
    nd = len(DILATIONS)
    o_refs, lse_refs, out_ref = refs[:nd], refs[nd:2 * nd], refs[2 * nd]
    so, sl = refs[2 * nd + 1:]
    for h in range(C_HEADS):
        cols = slice(h * C_DH, (h + 1) * C_DH)
        for di, d in enumerate(DILATIONS):
            for r in range(d):
                rows = slice(None) if d == 1 else pl.ds(r, DIL_ROWS // d, stride=d)
                so[di, rows, :] = o_refs[di][r, :, cols].astype(F32)
                sl[di, rows, :] = lse_refs[di][r, :, cols]
        top = sl[0]
        for di in range(1, nd):
            top = jnp.maximum(top, sl[di])
        num = 0.0
        den = 0.0
        for di in range(nd):
            w = jnp.exp(sl[di] - top)
            num = num + w * so[di]
            den = den + w
        out_ref[:, cols] = (num / den).astype(out_ref.dtype)


def dilated_merge(outs, lses):
    S = outs[0].shape[0] * outs[0].shape[1]
    bm = DIL_ROWS
    nd = len(DILATIONS)
    specs = [pl.BlockSpec((d, bm // d, C_W), lambda i: (0, i, 0)) for d in DILATIONS]
    return pl.pallas_call(
        _dilated_merge_body,
        grid=(S // bm,),
        in_specs=specs + specs,
        out_specs=pl.BlockSpec((bm, C_W), lambda i: (i, 0)),
        out_shape=jax.ShapeDtypeStruct((S, C_W), BF16),
        scratch_shapes=[pltpu.VMEM((nd, bm, C_DH), F32)] * 2,
        compiler_params=_cparams("parallel"),
        name="dilated_merge",
    )(*outs, *lses)


def dilated_branch(p_r, cos_c, sin_c):
    assert all(w // d == DIL_BLK for w, d in C_PATTERNS)
    results = [dilated_pattern(q, k, v) for q, k, v in rope_split(p_r, cos_c, sin_c)]
    return dilated_merge([r[0] for r in results], [r[1] for r in results])


def _merge_body(oa_ref, ob_ref, oc_ref, ga_ref, gb_ref, gc_ref, wa_ref, wb_ref, wc_ref, o_ref, wa_s, wb_s, wc_s):
    @pl.when(pl.program_id(1) == 0)
    def _():
        wa_s[...] = wa_ref[...].astype(BF16)
        wb_s[...] = wb_ref[...].astype(BF16)
        wc_s[...] = wc_ref[...].astype(BF16)

    acc = _sigmoid(ga_ref[...].astype(F32)) * _dot(oa_ref[...], wa_s[...])
    acc = acc + _sigmoid(gb_ref[...].astype(F32)) * _dot(ob_ref[...], wb_s[...])
    acc = acc + _sigmoid(gc_ref[...].astype(F32)) * _dot(oc_ref[...], wc_s[...])
    o_ref[...] = acc.astype(o_ref.dtype)


def merge_branches(o_a, o_b, o_c, p_r, w_a, w_b, w_c, bm=512, bn=512):
    S, K = o_a.shape
    g0 = 3 * C_W // bn
    gstep = D_MODEL // bn
    act = pl.BlockSpec((bm, K), lambda j, i: (i, 0))
    wsp = pl.BlockSpec((K, bn), lambda j, i: (0, j))
    gate = lambda b: pl.BlockSpec((bm, bn), lambda j, i: (i, g0 + b * gstep + j))
    return pl.pallas_call(
        _merge_body,
        grid=(D_MODEL // bn, S // bm),
        in_specs=[act, act, act, gate(0), gate(1), gate(2), wsp, wsp, wsp],
        out_specs=pl.BlockSpec((bm, bn), lambda j, i: (i, j)),
        out_shape=jax.ShapeDtypeStruct((S, D_MODEL), BF16),
        scratch_shapes=[pltpu.VMEM((K, bn), BF16)] * 3,
        compiler_params=_cparams("parallel", "arbitrary"),
        name="merge_branches",
    )(o_a, o_b, o_c, p_r, p_r, p_r, w_a, w_b, w_c)


def _ffn_body(h_ref, w1_ref, w3_ref, w2_ref, o_ref):
    @pl.when(pl.program_id(1) == 0)
    def _():
        o_ref[...] = jnp.zeros_like(o_ref)

    h = h_ref[...]
    a = _dot(h, w1_ref[...].astype(BF16))
    b = _dot(h, w3_ref[...].astype(BF16))
    o_ref[...] += _dot((_silu(a) * b).astype(BF16), w2_ref[...].astype(BF16))


def dense_ffn(h, w1, w3, w2, bm=512, bf=256):
    S, D = h.shape
    F = w1.shape[1]
    return pl.pallas_call(
        _ffn_body,
        grid=(S // bm, F // bf),
        in_specs=[pl.BlockSpec((bm, D), lambda i, f: (i, 0)), pl.BlockSpec((D, bf), lambda i, f: (0, f)),
                  pl.BlockSpec((D, bf), lambda i, f: (0, f)), pl.BlockSpec((bf, D), lambda i, f: (f, 0))],
        out_specs=pl.BlockSpec((bm, D), lambda i, f: (i, 0)),
        out_shape=jax.ShapeDtypeStruct((S, D), F32),
        compiler_params=_cparams("parallel", "arbitrary"),
        name="dense_ffn",
    )(h, w1, w3, w2)


def _router_body(h_ref, rw_ref, sel_ref, gate_ref):
    logits = _dot(h_ref[...].astype(BF16), rw_ref[...].astype(BF16))
    lane = lax.broadcasted_iota(jnp.int32, logits.shape, 1)
    logits = jnp.where(lane < N_EXPERTS, logits, NEG)
    t1 = jnp.max(logits, axis=1, keepdims=True)
    i1 = jnp.min(jnp.where(logits == t1, lane, LANES), axis=1, keepdims=True)
    rest = jnp.where(lane == i1, NEG, logits)
    t2 = jnp.max(rest, axis=1, keepdims=True)
    i2 = jnp.min(jnp.where(rest == t2, lane, LANES), axis=1, keepdims=True)
    e2 = jnp.exp(t2 - t1)
    g1 = 1.0 / (1.0 + e2)
    sel_ref[...] = jnp.where(lane == 0, i1, jnp.where(lane == 1, i2, 0))
    gate_ref[...] = jnp.where(lane == 0, g1, jnp.where(lane == 1, e2 * g1, 0.0))


def router(h, router_w, bm=512):
    S, D = h.shape
    rw = jnp.pad(router_w, ((0, 0), (0, LANES - N_EXPERTS)))
    out = pl.BlockSpec((bm, LANES), lambda i: (i, 0))
    return pl.pallas_call(
        _router_body,
        grid=(S // bm,),
        in_specs=[pl.BlockSpec((bm, D), lambda i: (i, 0)), pl.BlockSpec((D, LANES), lambda i: (0, 0))],
        out_specs=[out, out],
        out_shape=[jax.ShapeDtypeStruct((S, LANES), jnp.int32), jax.ShapeDtypeStruct((S, LANES), F32)],
        compiler_params=_cparams("parallel"),
        name="router",
    )(h, rw)


MOE_BM = 512
MOE_BG = 256


def _moe_plan(sel, bm):
    S = sel.shape[0]
    E = N_EXPERTS
    e = sel[:, :TOP_K].reshape(-1)
    onehot = (e[:, None] == jnp.arange(E, dtype=jnp.int32)[None, :]).astype(jnp.int32)
    csum = jnp.cumsum(onehot, axis=0)
    rank = jnp.sum((csum - 1) * onehot, axis=1)
    tiles_per = (csum[-1] + bm - 1) // bm
    tile_end = jnp.cumsum(tiles_per)
    pos = ((tile_end - tiles_per)[e] * bm + rank).astype(jnp.int32)
    n_tiles = tile_end[-1:].astype(jnp.int32)
    T = (TOP_K * S) // bm + E
    tid = jnp.arange(T, dtype=jnp.int32)
    te = jnp.minimum(jnp.sum((tid[:, None] >= tile_end[None, :]).astype(jnp.int32), axis=1), E - 1)
    te = jnp.where(tid < n_tiles[0], te, te[n_tiles[0] - 1]).astype(jnp.int32)
    src = jnp.zeros((T * bm,), jnp.int32).at[pos].set(jnp.arange(TOP_K * S, dtype=jnp.int32) // TOP_K)
    return pos, src, te, n_tiles


def _moe_gather_body(src_ref, h_hbm, o_ref, buf_ref, sem):
    base = pl.program_id(0) * MOE_BG

    def row_copy(j):
        return pltpu.make_async_copy(h_hbm.at[pl.ds(src_ref[base + j], 1), :], buf_ref.at[pl.ds(j, 1), :], sem)

    def start(j, c):
        row_copy(j).start()
        return c

    def wait(j, c):
        row_copy(j).wait()
        return c

    lax.fori_loop(0, MOE_BG, start, 0)
    lax.fori_loop(0, MOE_BG, wait, 0)
    o_ref[...] = buf_ref[...].astype(o_ref.dtype)


def moe_gather(h, src):
    S, D = h.shape
    R = src.shape[0]
    return pl.pallas_call(
        _moe_gather_body,
        grid_spec=pltpu.PrefetchScalarGridSpec(
            num_scalar_prefetch=1,
            grid=(R // MOE_BG,),
            in_specs=[pl.BlockSpec(memory_space=pl.ANY)],
            out_specs=pl.BlockSpec((MOE_BG, D), lambda i, src: (i, 0)),
            scratch_shapes=[pltpu.VMEM((MOE_BG, D), F32), pltpu.SemaphoreType.DMA(())],
        ),
        out_shape=jax.ShapeDtypeStruct((R, D), BF16),
        compiler_params=_cparams("arbitrary"),
        name="moe_gather",
    )(src, h)


def _moe_ffn_body(te_ref, nt_ref, x_ref, w1_ref, w3_ref, w2_ref, o_ref):
    @pl.when(pl.program_id(1) == 0)
    def _():
        o_ref[...] = jnp.zeros_like(o_ref)

    @pl.when(pl.program_id(0) < nt_ref[0])
    def _():
        x = x_ref[...]
        a = _dot(x, w1_ref[...].astype(BF16))
        b = _dot(x, w3_ref[...].astype(BF16))
        o_ref[...] += _dot((_silu(a) * b).astype(BF16), w2_ref[...].astype(BF16))


def moe_ffn(xs, tile_expert, n_tiles, w1, w3, w2, bm=MOE_BM, bf=256):
    R, D = xs.shape
    E, _, F = w1.shape
    nf = F // bf

    def live_f(i, f, nt):
        return jnp.where(i < nt[0], f, nf - 1)

    return pl.pallas_call(
        _moe_ffn_body,
        grid_spec=pltpu.PrefetchScalarGridSpec(
            num_scalar_prefetch=2,
            grid=(R // bm, nf),
            in_specs=[pl.BlockSpec((bm, D), lambda i, f, te, nt: (jnp.minimum(i, nt[0] - 1), 0)),
                      pl.BlockSpec((None, D, bf), lambda i, f, te, nt: (te[i], 0, live_f(i, f, nt))),
                      pl.BlockSpec((None, D, bf), lambda i, f, te, nt: (te[i], 0, live_f(i, f, nt))),
                      pl.BlockSpec((None, bf, D), lambda i, f, te, nt: (te[i], live_f(i, f, nt), 0))],
            out_specs=pl.BlockSpec((bm, D), lambda i, f, te, nt: (i, 0)),
        ),
        out_shape=jax.ShapeDtypeStruct((R, D), F32),
        compiler_params=_cparams("arbitrary", "arbitrary"),
        name="moe_ffn",
    )(tile_expert, n_tiles, xs, w1, w3, w2)


def _moe_combine_body(pos_ref, gate_ref, ys_hbm, o_ref, buf_ref, sem):
    base = pl.program_id(0) * MOE_BG

    def row_copy(j, k):
        row = pos_ref[(base + j) * TOP_K + k]
        return pltpu.make_async_copy(ys_hbm.at[pl.ds(row, 1), :], buf_ref.at[k, pl.ds(j, 1), :], sem)

    def start(j, c):
        for k in range(TOP_K):
            row_copy(j, k).start()
        return c

    def wait(j, c):
        for k in range(TOP_K):
            row_copy(j, k).wait()
        return c

    lax.fori_loop(0, MOE_BG, start, 0)
    lax.fori_loop(0, MOE_BG, wait, 0)
    gate = gate_ref[...]
    acc = gate[:, 0:1] * buf_ref[0]
    for k in range(1, TOP_K):
        acc = acc + gate[:, k:k + 1] * buf_ref[k]
    o_ref[...] = acc


def moe_combine(ys, pos, gate):
    S = gate.shape[0]
    D = ys.shape[1]
    return pl.pallas_call(
        _moe_combine_body,
        grid_spec=pltpu.PrefetchScalarGridSpec(
            num_scalar_prefetch=1,
            grid=(S // MOE_BG,),
            in_specs=[pl.BlockSpec((MOE_BG, LANES), lambda i, pos: (i, 0)), pl.BlockSpec(memory_space=pl.ANY)],
            out_specs=pl.BlockSpec((MOE_BG, D), lambda i, pos: (i, 0)),
            scratch_shapes=[pltpu.VMEM((TOP_K, MOE_BG, D), F32), pltpu.SemaphoreType.DMA(())],
        ),
        out_shape=jax.ShapeDtypeStruct((S, D), F32),
        compiler_params=_cparams("arbitrary"),
        name="moe_combine",
    )(pos, gate, ys)


def moe_layer(h, router_w, w1, w3, w2):
    sel, gate = router(h, router_w)
    pos, src, tile_expert, n_tiles = _moe_plan(sel, MOE_BM)
    xs = moe_gather(h, src)
    ys = moe_ffn(xs, tile_expert, n_tiles, w1, w3, w2)
    return moe_combine(ys, pos, gate)


def _rot_cols(w):
    half = w.shape[-1] // 2
    return jnp.concatenate([-w[..., half:], w[..., :half]], axis=-1)


def _rope_tables(seq, dim):
    inv = ROPE_THETA ** (-jnp.arange(0, dim, 2, dtype=F32) / dim)
    ang = jnp.arange(seq, dtype=F32)[:, None] * inv[None, :]
    return jnp.cos(ang), jnp.sin(ang)


def _small_weight(w_in):
    kr = w_in[:, OFF_KROPE:OFF_KROPE + B_ROPE]
    parts = [w_in[:, OFF_CQ:OFF_CQ + B_Q_LORA], w_in[:, OFF_CKV:OFF_CKV + B_KV_LORA], kr, _rot_cols(kr),
             w_in[:, OFF_A_IN:OFF_A_IN + A_HEADS], w_in[:, OFF_B_IN:OFF_B_IN + A_HEADS]]
    w = jnp.concatenate(parts, axis=1)
    return jnp.pad(w, ((0, 0), (0, PS_W - w.shape[1])))


def _mla_weights(w_uq, w_ukv):
    wq = w_uq.reshape(B_Q_LORA, B_HEADS, B_NOPE + B_ROPE)
    pe = wq[..., B_NOPE:]
    zeros = jnp.zeros_like(pe)
    wq_ext = jnp.concatenate([wq[..., :B_NOPE], pe, zeros, _rot_cols(pe), zeros], axis=-1)
    return wq_ext.reshape(B_Q_LORA, B_HEADS * B_QH).astype(BF16), w_ukv.astype(BF16)


def hybrid_mixer(h, layer, p, tables):
    cs_b, cos_c, sin_c = tables
    w_in = p["w_in"][layer]
    p_a = matmul(h, w_in, OFF_A_IN, BF16, bm=1024, bn=512)
    p_s = matmul(h, _small_weight(w_in), PS_W, F32, bm=1024, bn=256)
    p_r = matmul(h, w_in[:, OFF_REST:].astype(BF16), REST_W, BF16, bm=1024, bn=512)
    o_a = gdn_branch(p_a, p_s, p["conv_w"][layer], p["gdn_a_log"][layer], p["gdn_dt_bias"][layer],
                     p["gdn_norm"][layer])
    wq_ext, wkv = _mla_weights(p["mla_w_uq"][layer], p["mla_w_ukv"][layer])
    q_cat, k_cat, v_b = mla_prep(p_s, p["mla_q_norm"][layer], p["mla_kv_norm"][layer], wq_ext, wkv, cs_b)
    o_b = mla_attention(q_cat, k_cat, v_b)
    o_c = dilated_branch(p_r, cos_c, sin_c)
    merged = merge_branches(o_a, o_b, o_c, p_r, p["w_branch_a"][layer], p["w_branch_b"][layer],
                            p["w_branch_c"][layer])
    return matmul(merged, p["w_out"][layer], D_MODEL, F32, bm=1024, bn=512)


def kernel(x, attn_norm, w_in, conv_w, gdn_a_log, gdn_dt_bias, gdn_norm, mla_q_norm, mla_w_uq, mla_kv_norm, mla_w_ukv, w_branch_a, w_branch_b, w_branch_c, w_out, ffn_norm, dense_w1, dense_w3, dense_w2, router_w, moe_w1, moe_w3, moe_w2, final_norm):
    p = dict(w_in=w_in, conv_w=conv_w, gdn_a_log=gdn_a_log, gdn_dt_bias=gdn_dt_bias, gdn_norm=gdn_norm,
             mla_q_norm=mla_q_norm, mla_w_uq=mla_w_uq, mla_kv_norm=mla_kv_norm, mla_w_ukv=mla_w_ukv,
             w_branch_a=w_branch_a, w_branch_b=w_branch_b, w_branch_c=w_branch_c, w_out=w_out)
    Bn, S, D = x.shape
    assert Bn == 1 and D == D_MODEL
    cos_b, sin_b = _rope_tables(S, B_ROPE)
    cos_c, sin_c = _rope_tables(S, C_DH)
    tables = (jnp.concatenate([cos_b, cos_b, sin_b, sin_b], axis=1),
              jnp.concatenate([cos_c, cos_c], axis=1), jnp.concatenate([-sin_c, sin_c], axis=1))
    xr = x.reshape(S, D)
    h = rmsnorm(xr, attn_norm[0], BF16)
    out = None
    for layer in range(DEPTH):
        y = hybrid_mixer(h, layer, p, tables)
        i = layer // 2
        if layer % 2 == 0:
            xr, h = add_rmsnorm(xr, y, ffn_norm[layer], BF16)
            y = dense_ffn(h, dense_w1[i].astype(BF16), dense_w3[i].astype(BF16), dense_w2[i].astype(BF16))
        else:
            xr, h = add_rmsnorm(xr, y, ffn_norm[layer], F32)
            y = moe_layer(h, router_w[i], moe_w1[i].astype(BF16), moe_w3[i].astype(BF16), moe_w2[i].astype(BF16))
        if layer + 1 < DEPTH:
            xr, h = add_rmsnorm(xr, y, attn_norm[layer + 1], BF16)
        else:
            out = add_rmsnorm_final(xr, y, final_norm)
    return out.reshape(Bn, S, D)
```

```python
import functools
import math

import jax
import jax.numpy as jnp
from jax import lax
from jax.experimental import pallas as pl
from jax.experimental.pallas import tpu as pltpu

F32 = jnp.float32
BF16 = jnp.bfloat16

EPS = 1e-6
ROPE_THETA = 10000.0
NEG = -1e30

LANES = 128
VMEM_LIMIT_BYTES = 56 * 1024 * 1024

D_MODEL = 4096
DEPTH = 2
A_HEADS, A_DK, A_DV, A_CONV, A_CHUNK = 16, 128, 128, 4, 64
B_HEADS, B_Q_LORA, B_KV_LORA, B_NOPE, B_ROPE, B_VDIM = 16, 1024, 512, 128, 64, 128
C_HEADS, C_DH = 16, 128
C_PATTERNS = ((128, 1), (512, 4), (2048, 16))
D_FF = 11008
N_EXPERTS, TOP_K, D_EXPERT = 8, 2, 4096

A_QK_W = A_HEADS * A_DK
A_V_W = A_HEADS * A_DV
A_QKV_W = 2 * A_QK_W + A_V_W
C_W = C_HEADS * C_DH
OFF_A_IN = A_QKV_W + A_V_W
OFF_B_IN = OFF_A_IN + A_HEADS
OFF_CQ = OFF_B_IN + A_HEADS
OFF_CKV = OFF_CQ + B_Q_LORA
OFF_KROPE = OFF_CKV + B_KV_LORA
OFF_REST = OFF_KROPE + B_ROPE
REST_W = 3 * C_W + 3 * D_MODEL
PS_CQ, PS_CKV, PS_KR, PS_AB, PS_W = 0, 1024, 1536, 1664, 1792

B_QH = 384
B_KH = 256


def _cparams(*sem):
    return pltpu.CompilerParams(dimension_semantics=sem, vmem_limit_bytes=VMEM_LIMIT_BYTES)


def _dot(a, b):
    return jnp.dot(a, b, preferred_element_type=F32)


def _dot_nt(a, b):
    return lax.dot_general(a, b, (((1,), (1,)), ((), ())), preferred_element_type=F32)


def _sigmoid(x):
    return 1.0 / (1.0 + jnp.exp(-x))


def _silu(x):
    return x * _sigmoid(x)


def _softplus(x):
    return jnp.maximum(x, 0.0) + jnp.log(1.0 + jnp.exp(-jnp.abs(x)))


def _rmsnorm_body(x_ref, g_ref, o_ref):
    x = x_ref[...]
    y = x * lax.rsqrt(jnp.mean(x * x, axis=-1, keepdims=True) + EPS)
    o_ref[...] = (y * g_ref[...]).astype(o_ref.dtype)


def rmsnorm(x, g, out_dtype, bm=256):
    M, D = x.shape
    return pl.pallas_call(
        _rmsnorm_body,
        grid=(M // bm,),
        in_specs=[pl.BlockSpec((bm, D), lambda i: (i, 0)), pl.BlockSpec((1, D), lambda i: (0, 0))],
        out_specs=pl.BlockSpec((bm, D), lambda i: (i, 0)),
        out_shape=jax.ShapeDtypeStruct((M, D), out_dtype),
        compiler_params=_cparams("parallel"),
        name="rmsnorm",
    )(x, g.reshape(1, D))


def _add_rmsnorm_body(x_ref, y_ref, g_ref, xo_ref, ho_ref):
    x = x_ref[...] + y_ref[...].astype(F32)
    xo_ref[...] = x
    y = x * lax.rsqrt(jnp.mean(x * x, axis=-1, keepdims=True) + EPS)
    ho_ref[...] = (y * g_ref[...]).astype(ho_ref.dtype)


def add_rmsnorm(x, y, g, out_dtype, bm=256):
    M, D = x.shape
    row = pl.BlockSpec((bm, D), lambda i: (i, 0))
    return pl.pallas_call(
        _add_rmsnorm_body,
        grid=(M // bm,),
        in_specs=[row, row, pl.BlockSpec((1, D), lambda i: (0, 0))],
        out_specs=[row, row],
        out_shape=[jax.ShapeDtypeStruct((M, D), F32), jax.ShapeDtypeStruct((M, D), out_dtype)],
        compiler_params=_cparams("parallel"),
        name="add_rmsnorm",
    )(x, y, g.reshape(1, D))


def _add_rmsnorm_final_body(x_ref, y_ref, g_ref, o_ref):
    x = x_ref[...] + y_ref[...].astype(F32)
    y = x * lax.rsqrt(jnp.mean(x * x, axis=-1, keepdims=True) + EPS)
    o_ref[...] = y * g_ref[...]


def add_rmsnorm_final(x, y, g, bm=256):
    M, D = x.shape
    row = pl.BlockSpec((bm, D), lambda i: (i, 0))
    return pl.pallas_call(
        _add_rmsnorm_final_body,
        grid=(M // bm,),
        in_specs=[row, row, pl.BlockSpec((1, D), lambda i: (0, 0))],
        out_specs=row,
        out_shape=jax.ShapeDtypeStruct((M, D), F32),
        compiler_params=_cparams("parallel"),
        name="add_rmsnorm_final",
    )(x, y, g.reshape(1, D))


def _mm_body(x_ref, w_ref, o_ref, wbf_ref):
    @pl.when(pl.program_id(1) == 0)
    def _():
        wbf_ref[...] = w_ref[...].astype(BF16)

    o_ref[...] = _dot(x_ref[...], wbf_ref[...]).astype(o_ref.dtype)


def matmul(x, w, layer, n_cols, out_dtype, bm, bn):
    M, K = x.shape
    return pl.pallas_call(
        _mm_body,
        grid=(n_cols // bn, M // bm),
        in_specs=[pl.BlockSpec((bm, K), lambda j, i: (i, 0)),
                  pl.BlockSpec((None, K, bn), lambda j, i: (layer, 0, j))],
        out_specs=pl.BlockSpec((bm, bn), lambda j, i: (i, j)),
        out_shape=jax.ShapeDtypeStruct((M, n_cols), out_dtype),
        scratch_shapes=[pltpu.VMEM((K, bn), BF16)],
        compiler_params=_cparams("parallel", "arbitrary"),
        name="matmul",
    )(x, w)


def _mm_shifted_body(x_ref, wa_ref, wb_ref, o_ref, wbf_ref, *, shift):
    @pl.when(pl.program_id(1) == 0)
    def _():
        w = jnp.concatenate([wa_ref[:, shift:], wb_ref[:, :shift]], axis=1)
        wbf_ref[...] = w.astype(BF16)

    o_ref[...] = _dot(x_ref[...], wbf_ref[...]).astype(o_ref.dtype)


def matmul_shifted(x, w, layer, col0, n_cols, out_dtype, bm, bn):
    M, K = x.shape
    base, shift = divmod(col0, LANES)
    assert (base * LANES) % bn == 0 and shift > 0
    ja = base * LANES // bn
    per = bn // LANES
    return pl.pallas_call(
        functools.partial(_mm_shifted_body, shift=shift),
        grid=(n_cols // bn, M // bm),
        in_specs=[pl.BlockSpec((bm, K), lambda j, i: (i, 0)),
                  pl.BlockSpec((None, K, bn), lambda j, i: (layer, 0, ja + j)),
                  pl.BlockSpec((None, K, LANES), lambda j, i: (layer, 0, base + per * (j + 1)))],
        out_specs=pl.BlockSpec((bm, bn), lambda j, i: (i, j)),
        out_shape=jax.ShapeDtypeStruct((M, n_cols), out_dtype),
        scratch_shapes=[pltpu.VMEM((K, bn), BF16)],
        compiler_params=_cparams("parallel", "arbitrary"),
        name="matmul_shifted",
    )(x, w, w)


GDN_ROWS = 256
GDN_HB = 8
GDN_PREV = 16


def _gdn_body(q_ref, k_ref, v_ref, qp_ref, kp_ref, vp_ref, z_ref, ab_ref, cwq_ref, cwk_ref, cwv_ref,
              alog_ref, dtb_ref, nw_ref, o_ref, state_ref):
    hb = pl.program_id(0)
    i = pl.program_id(1)
    R = GDN_ROWS
    C = A_CHUNK
    NC = R // C

    @pl.when(i == 0)
    def _():
        state_ref[...] = jnp.zeros_like(state_ref)

    has_prev = (i > 0).astype(F32)
    lane = lax.broadcasted_iota(jnp.int32, (R, LANES), 1)
    row = lax.broadcasted_iota(jnp.int32, (R, LANES), 0)
    rin = row & (C - 1)
    ri = lax.broadcasted_iota(jnp.int32, (R, R), 0)
    ci = lax.broadcasted_iota(jnp.int32, (R, R), 1)
    same = (ri // C) == (ci // C)
    incl = same & (ri >= ci)
    strict = same & (ri > ci)

    ab = ab_ref[...]
    g_all = -jnp.exp(alog_ref[...]) * _softplus(ab + dtb_ref[...])
    beta_all = _sigmoid(ab)
    nw = nw_ref[...]

    def conv_silu(cur_ref, prev_ref, cw_ref, sl):
        cur = cur_ref[:, sl].astype(F32)
        prev = prev_ref[:, sl].astype(F32) * has_prev
        full = jnp.concatenate([prev, cur], axis=0)
        cw = cw_ref[:, sl]
        y = cw[A_CONV - 1:A_CONV, :] * cur
        for s in range(1, A_CONV):
            y = y + cw[A_CONV - 1 - s:A_CONV - s, :] * pltpu.roll(full, s, axis=0)[GDN_PREV:, :]
        return _silu(y)

    heads = range(GDN_HB)
    lanes_of = [slice(hh * LANES, (hh + 1) * LANES) for hh in heads]
    st = []
    for hh in heads:
        h = hb * GDN_HB + hh
        sl = lanes_of[hh]
        g = jnp.broadcast_to(jnp.sum(jnp.where(lane == h, g_all, 0.0), axis=1, keepdims=True), (R, LANES))
        beta = jnp.broadcast_to(
            jnp.sum(jnp.where(lane == h + A_HEADS, beta_all, 0.0), axis=1, keepdims=True), (R, LANES))
        gc = g
        s = 1
        while s < C:
            gc = gc + jnp.where(rin >= s, pltpu.roll(gc, s, axis=0), 0.0)
            s *= 2
        g_last = jnp.broadcast_to(jnp.sum(g.reshape(NC, C, LANES), axis=1, keepdims=True),
                                  (NC, C, LANES)).reshape(R, LANES)

        q = conv_silu(q_ref, qp_ref, cwq_ref, sl)
        k = conv_silu(k_ref, kp_ref, cwk_ref, sl)
        v = conv_silu(v_ref, vp_ref, cwv_ref, sl)
        qn = q * lax.rsqrt(jnp.sum(q * q, axis=-1, keepdims=True) + EPS) * (A_DK ** -0.5)
        kn = k * lax.rsqrt(jnp.sum(k * k, axis=-1, keepdims=True) + EPS)
        kb = kn * beta
        eg = jnp.exp(gc)

        gc_t = gc.T
        diff = jnp.concatenate([gc, gc], axis=1) - jnp.concatenate([gc_t, gc_t], axis=0)
        decay = jnp.exp(jnp.where(incl, diff, NEG))
        kn_b = kn.astype(BF16)
        lower = jnp.where(strict, _dot_nt(kb.astype(BF16), kn_b) * decay, 0.0)
        st.append(dict(
            xp=-lower, qmat=-lower,
            qk=(_dot_nt(qn.astype(BF16), kn_b) * decay).astype(BF16),
            rhs=jnp.concatenate([v * beta, kb * eg], axis=1),
            q_dec=(qn * eg).astype(BF16),
            k_dec=kn * jnp.exp(g_last - gc),
            g_last=g_last,
        ))

    for _ in range(int(math.log2(C)) - 1):
        for t in st:
            xb = t["xp"].astype(BF16)
            t["xp"] = _dot(xb, xb)
            t["qmat"] = t["qmat"] + t["xp"] + _dot(t["qmat"].astype(BF16), t["xp"].astype(BF16))
    for t in st:
        sol = t["rhs"] + _dot(t["qmat"].astype(BF16), t["rhs"].astype(BF16))
        t["u"] = sol[:, :A_DV]
        t["w_b"] = sol[:, A_DV:].astype(BF16)
        t["outs"] = []

    states = [state_ref[hh] for hh in heads]
    for c in range(NC):
        rs = slice(c * C, (c + 1) * C)
        for hh, t in enumerate(st):
            sb = states[hh].astype(BF16)
            v_new_b = (t["u"][rs] - _dot(t["w_b"][rs], sb)).astype(BF16)
            t["outs"].append(_dot(t["q_dec"][rs], sb) + _dot(t["qk"][rs, :][:, rs], v_new_b))
            states[hh] = (states[hh] * jnp.exp(t["g_last"][c * C:c * C + 1, :])
                          + _dot(t["k_dec"][rs].T.astype(BF16), v_new_b))
    for hh, t in enumerate(st):
        sl = lanes_of[hh]
        state_ref[hh] = states[hh]
        out = jnp.concatenate(t["outs"], axis=0)
        out = out * lax.rsqrt(jnp.mean(out * out, axis=-1, keepdims=True) + EPS) * nw
        o_ref[:, sl] = (out * _silu(z_ref[:, sl].astype(F32))).astype(o_ref.dtype)


def gdn_branch(p_a, p_s, conv_w, a_log, dt_bias, norm_w):
    S = p_a.shape[0]
    R, W = GDN_ROWS, GDN_HB * LANES
    nq = A_QK_W // W
    pad = lambda a: jnp.pad(a.astype(F32), (0, LANES - A_HEADS)).reshape(1, LANES)

    def cur(off):
        return pl.BlockSpec((R, W), lambda hb, i: (i, off + hb))

    def prev(off):
        return pl.BlockSpec((GDN_PREV, W), lambda hb, i: (jnp.maximum(i * (R // GDN_PREV) - 1, 0), off + hb))

    def cw(off):
        return pl.BlockSpec((A_CONV, W), lambda hb, i: (0, off + hb))

    vec = pl.BlockSpec((1, LANES), lambda hb, i: (0, 0))
    return pl.pallas_call(
        _gdn_body,
        grid=(A_HEADS // GDN_HB, S // R),
        in_specs=[cur(0), cur(nq), cur(2 * nq), prev(0), prev(nq), prev(2 * nq), cur(3 * nq),
                  pl.BlockSpec((R, LANES), lambda hb, i: (i, PS_AB // LANES)),
                  cw(0), cw(nq), cw(2 * nq), vec, vec, vec],
        out_specs=pl.BlockSpec((R, W), lambda hb, i: (i, hb)),
        out_shape=jax.ShapeDtypeStruct((S, A_V_W), BF16),
        scratch_shapes=[pltpu.VMEM((GDN_HB, A_DK, A_DV), F32)],
        compiler_params=_cparams("parallel", "arbitrary"),
        name="gdn",
    )(p_a, p_a, p_a, p_a, p_a, p_a, p_a, p_s, conv_w, conv_w, conv_w, pad(a_log), pad(dt_bias),
      norm_w.astype(F32).reshape(1, LANES))


MLA_HG = 4


def _mla_prep_body(cq_ref, ckv_ref, kr_ref, qn_ref, kvn_ref, wq_ref, wkv_ref, cs_ref, qo_ref, ko_ref, vo_ref):
    scale = (B_NOPE + B_ROPE) ** -0.5 * math.log2(math.e)
    cq = cq_ref[...]
    cq = (cq * lax.rsqrt(jnp.mean(cq * cq, axis=-1, keepdims=True) + EPS) * qn_ref[...]).astype(BF16)
    ckv = ckv_ref[...]
    ckv = (ckv * lax.rsqrt(jnp.mean(ckv * ckv, axis=-1, keepdims=True) + EPS) * kvn_ref[...]).astype(BF16)
    rq = _dot(cq, wq_ref[...])
    rkv = _dot(ckv, wkv_ref[...])
    cs = cs_ref[...]
    sc = pltpu.roll(cs, B_ROPE, axis=1)
    prod = kr_ref[...] * cs
    lane = lax.broadcasted_iota(jnp.int32, prod.shape, 1)
    k_pe = jnp.where(lane < B_ROPE, prod + pltpu.roll(prod, B_ROPE, axis=1), 0.0).astype(BF16)
    for hh in range(MLA_HG):
        qb = hh * B_QH
        roped = rq[:, qb + LANES:qb + 2 * LANES] * cs + rq[:, qb + 2 * LANES:qb + 3 * LANES] * sc
        qo_ref[:, hh * B_KH:hh * B_KH + LANES] = (rq[:, qb:qb + LANES] * scale).astype(BF16)
        qo_ref[:, hh * B_KH + LANES:(hh + 1) * B_KH] = (roped * scale).astype(BF16)
        ko_ref[:, hh * B_KH:hh * B_KH + LANES] = rkv[:, hh * 2 * LANES:hh * 2 * LANES + LANES].astype(BF16)
        ko_ref[:, hh * B_KH + LANES:(hh + 1) * B_KH] = k_pe
        vo_ref[:, hh * 2 * LANES:hh * 2 * LANES + LANES] = (
            rkv[:, hh * 2 * LANES + LANES:(hh + 1) * 2 * LANES].astype(BF16))
        vo_ref[:, hh * 2 * LANES + LANES:(hh + 1) * 2 * LANES] = jnp.ones((rkv.shape[0], LANES), BF16)


def mla_prep(p_s, q_norm, kv_norm, wq_ext, wkv, cs_b, bm=512):
    S = p_s.shape[0]
    HG = MLA_HG
    return pl.pallas_call(
        _mla_prep_body,
        grid=(B_HEADS // HG, S // bm),
        in_specs=[
            pl.BlockSpec((bm, B_Q_LORA), lambda g, i: (i, PS_CQ // B_Q_LORA)),
            pl.BlockSpec((bm, B_KV_LORA), lambda g, i: (i, PS_CKV // B_KV_LORA)),
            pl.BlockSpec((bm, LANES), lambda g, i: (i, PS_KR // LANES)),
            pl.BlockSpec((1, B_Q_LORA), lambda g, i: (0, 0)),
            pl.BlockSpec((1, B_KV_LORA), lambda g, i: (0, 0)),
            pl.BlockSpec((B_Q_LORA, HG * B_QH), lambda g, i: (0, g)),
            pl.BlockSpec((B_KV_LORA, HG * 2 * LANES), lambda g, i: (0, g)),
            pl.BlockSpec((bm, LANES), lambda g, i: (i, 0)),
        ],
        out_specs=[
            pl.BlockSpec((bm, HG * B_KH), lambda g, i: (i, g)),
            pl.BlockSpec((bm, HG * B_KH), lambda g, i: (i, g)),
            pl.BlockSpec((bm, HG * 2 * LANES), lambda g, i: (i, g)),
        ],
        out_shape=[
            jax.ShapeDtypeStruct((S, B_HEADS * B_KH), BF16),
            jax.ShapeDtypeStruct((S, B_HEADS * B_KH), BF16),
            jax.ShapeDtypeStruct((S, B_HEADS * 2 * B_VDIM), BF16),
        ],
        compiler_params=_cparams("parallel", "parallel"),
        name="mla_prep",
    )(p_s, p_s, p_s, q_norm.reshape(1, -1), kv_norm.reshape(1, -1), wq_ext, wkv, cs_b)


FLASH_BLK = 1024
FLASH_RC = 256


def _flash_body(qi_ref, kj_ref, q_ref, k_ref, v_ref, o_ref, m_ref, acc_ref):
    t = pl.program_id(1)
    qi = qi_ref[t]
    kj = kj_ref[t]

    @pl.when(kj == 0)
    def _():
        m_ref[...] = jnp.full_like(m_ref, NEG)
        acc_ref[...] = jnp.zeros_like(acc_ref)

    def step(diag):
        nc = FLASH_BLK // FLASH_RC
        rows_of = [slice(c * FLASH_RC, (c + 1) * FLASH_RC) for c in range(nc)]
        ncol_of = [(c + 1) * FLASH_RC if diag else FLASH_BLK for c in range(nc)]

        def scores(c):
            s = _dot_nt(q_ref[rows_of[c], :], k_ref[:ncol_of[c], :])
            if diag:
                r = lax.broadcasted_iota(jnp.int32, s.shape, 0) + c * FLASH_RC
                col = lax.broadcasted_iota(jnp.int32, s.shape, 1)
                s = jnp.where(col <= r, s, NEG)
            return s

        s_next = scores(0)
        for c in range(nc):
            s = s_next
            if c + 1 < nc:
                s_next = scores(c + 1)
            rows = rows_of[c]
            m_prev = m_ref[rows, :]
            m_new = jnp.maximum(m_prev, jnp.max(s, axis=1, keepdims=True))
            p = jnp.exp2(s - m_new).astype(BF16)
            acc_ref[rows, :] = jnp.exp2(m_prev - m_new) * acc_ref[rows, :] + _dot(p, v_ref[:ncol_of[c], :])
            m_ref[rows, :] = m_new

    @pl.when(kj < qi)
    def _():
        step(False)

    @pl.when(kj == qi)
    def _():
        step(True)
        acc = acc_ref[...]
        o_ref[...] = (acc[:, :B_VDIM] / acc[:, B_VDIM:]).astype(o_ref.dtype)


def mla_attention(q_cat, k_cat, v_ext):
    S = q_cat.shape[0]
    blk = FLASH_BLK
    n = S // blk
    pairs = [(qi, kj) for qi in range(n) for kj in range(qi + 1)]
    qi_tab = jnp.asarray([p[0] for p in pairs], jnp.int32)
    kj_tab = jnp.asarray([p[1] for p in pairs], jnp.int32)
    return pl.pallas_call(
        _flash_body,
        grid_spec=pltpu.PrefetchScalarGridSpec(
            num_scalar_prefetch=2,
            grid=(B_HEADS, len(pairs)),
            in_specs=[
                pl.BlockSpec((blk, B_KH), lambda h, t, qi, kj: (qi[t], h)),
                pl.BlockSpec((blk, B_KH), lambda h, t, qi, kj: (kj[t], h)),
                pl.BlockSpec((blk, 2 * B_VDIM), lambda h, t, qi, kj: (kj[t], h)),
            ],
            out_specs=pl.BlockSpec((blk, B_VDIM), lambda h, t, qi, kj: (qi[t], h)),
            scratch_shapes=[pltpu.VMEM((blk, 1), F32), pltpu.VMEM((blk, 2 * B_VDIM), F32)],
        ),
        out_shape=jax.ShapeDtypeStruct((S, B_HEADS * B_VDIM), BF16),
        compiler_params=_cparams("parallel", "arbitrary"),
        name="mla_flash",
    )(qi_tab, kj_tab, q_cat, k_cat, v_ext)


DIL_BLK = 128
DIL_ROWS = 256
DILATIONS = tuple(d for _, d in C_PATTERNS)


def _rope_split_body(q_ref, k_ref, v_ref, c_ref, s_ref, *refs):
    nd = len(DILATIONS)
    outs, (sq, sk, sv) = refs[:3 * nd], refs[3 * nd:]
    c = c_ref[...]
    s = s_ref[...]
    scale = C_DH ** -0.5
    for h in range(C_HEADS):
        sl = slice(h * C_DH, (h + 1) * C_DH)
        x = q_ref[:, sl].astype(F32)
        sq[h] = (x * c + pltpu.roll(x, C_DH // 2, axis=1) * s) * scale
        x = k_ref[:, sl].astype(F32)
        sk[h] = x * c + pltpu.roll(x, C_DH // 2, axis=1) * s
        sv[h] = v_ref[:, sl].astype(F32)
        for di, d in enumerate(DILATIONS):
            for src, dst in zip((sq, sk, sv), outs[3 * di:3 * di + 3]):
                for r in range(d):
                    rows = src[h] if d == 1 else src[h, pl.ds(r, DIL_ROWS // d, stride=d), :]
                    dst[r, :, sl] = rows.astype(BF16)


def rope_split(p_r, cos_c, sin_c):
    S = p_r.shape[0]
    bm = DIL_ROWS
    tab = pl.BlockSpec((bm, C_DH), lambda i: (i, 0))
    col = lambda j: pl.BlockSpec((bm, C_W), lambda i: (i, j))
    out_specs, out_shape = [], []
    for d in DILATIONS:
        out_specs += [pl.BlockSpec((d, bm // d, C_W), lambda i: (0, i, 0))] * 3
        out_shape += [jax.ShapeDtypeStruct((d, S // d, C_W), BF16)] * 3
    outs = pl.pallas_call(
        _rope_split_body,
        grid=(S // bm,),
        in_specs=[col(0), col(1), col(2), tab, tab],
        out_specs=out_specs,
        out_shape=out_shape,
        scratch_shapes=[pltpu.VMEM((C_HEADS, bm, C_DH), F32)] * 3,
        compiler_params=_cparams("parallel"),
        name="rope_split",
    )(p_r, p_r, p_r, cos_c, sin_c)
    return [outs[3 * i:3 * i + 3] for i in range(len(DILATIONS))]


def _dilated_body(q_ref, kc_ref, kp_ref, vc_ref, vp_ref, o_ref, lse_ref):
    B = DIL_BLK
    n = pl.program_id(1)
    qi = lax.broadcasted_iota(jnp.int32, (B, 2 * B), 0)
    kj = lax.broadcasted_iota(jnp.int32, (B, 2 * B), 1)
    dist = qi + B - kj
    valid = (dist >= 0) & (dist <= B) & ((kj >= B) | (n > 0))
    for h in range(C_HEADS):
        sl = slice(h * C_DH, (h + 1) * C_DH)
        kk = jnp.concatenate([kp_ref[:, sl], kc_ref[:, sl]], axis=0)
        vv = jnp.concatenate([vp_ref[:, sl], vc_ref[:, sl]], axis=0)
        s = jnp.where(valid, _dot_nt(q_ref[:, sl], kk), NEG)
        m = jnp.max(s, axis=1, keepdims=True)
        p = jnp.exp(s - m)
        l = jnp.sum(p, axis=1, keepdims=True)
        o_ref[:, sl] = (_dot(p.astype(BF16), vv) / l).astype(o_ref.dtype)
        lse_ref[:, sl] = jnp.broadcast_to(m + jnp.log(l), (B, C_DH))


def dilated_pattern(q, k, v):
    d, L, W = q.shape
    B = DIL_BLK
    cur = pl.BlockSpec((None, B, W), lambda r, n: (r, n, 0))
    prv = pl.BlockSpec((None, B, W), lambda r, n: (r, jnp.maximum(n - 1, 0), 0))
    return pl.pallas_call(
        _dilated_body,
        grid=(d, L // B),
        in_specs=[cur, cur, prv, cur, prv],
        out_specs=[cur, cur],
        out_shape=[jax.ShapeDtypeStruct((d, L, W), BF16), jax.ShapeDtypeStruct((d, L, W), F32)],
        compiler_params=_cparams("parallel", "parallel"),
        name=f"dilated_d{d}",
    )(q, k, k, v, v)


def _dilated_merge_body(*refs):
    nd = len(DILATIONS)
    o_refs, lse_refs, out_ref = refs[:nd], refs[nd:2 * nd], refs[2 * nd]
    so, sl = refs[2 * nd + 1:]
    for h in range(C_HEADS):
        cols = slice(h * C_DH, (h + 1) * C_DH)
        for di, d in enumerate(DILATIONS):
            for r in range(d):
                rows = slice(None) if d == 1 else pl.ds(r, DIL_ROWS // d, stride=d)
                so[di, rows, :] = o_refs[di][r, :, cols].astype(F32)
                sl[di, rows, :] = lse_refs[di][r, :, cols]
        top = sl[0]
        for di in range(1, nd):
            top = jnp.maximum(top, sl[di])
        num = 0.0
        den = 0.0
        for di in range(nd):
            w = jnp.exp(sl[di] - top)
            num = num + w * so[di]
            den = den + w
        out_ref[:, cols] = (num / den).astype(out_ref.dtype)


def dilated_merge(outs, lses):
    S = outs[0].shape[0] * outs[0].shape[1]
    bm = DIL_ROWS
    nd = len(DILATIONS)
    specs = [pl.BlockSpec((d, bm // d, C_W), lambda i: (0, i, 0)) for d in DILATIONS]
    return pl.pallas_call(
        _dilated_merge_body,
        grid=(S // bm,),
        in_specs=specs + specs,
        out_specs=pl.BlockSpec((bm, C_W), lambda i: (i, 0)),
        out_shape=jax.ShapeDtypeStruct((S, C_W), BF16),
        scratch_shapes=[pltpu.VMEM((nd, bm, C_DH), F32)] * 2,
        compiler_params=_cparams("parallel"),
        name="dilated_merge",
    )(*outs, *lses)


def dilated_branch(p_r, cos_c, sin_c):
    assert all(w // d == DIL_BLK for w, d in C_PATTERNS)
    results = [dilated_pattern(q, k, v) for q, k, v in rope_split(p_r, cos_c, sin_c)]
    return dilated_merge([r[0] for r in results], [r[1] for r in results])


def _merge_body(oa_ref, ob_ref, oc_ref, ga_ref, gb_ref, gc_ref, wa_ref, wb_ref, wc_ref, o_ref, wa_s, wb_s, wc_s):
    @pl.when(pl.program_id(1) == 0)
    def _():
        wa_s[...] = wa_ref[...].astype(BF16)
        wb_s[...] = wb_ref[...].astype(BF16)
        wc_s[...] = wc_ref[...].astype(BF16)

    acc = _sigmoid(ga_ref[...].astype(F32)) * _dot(oa_ref[...], wa_s[...])
    acc = acc + _sigmoid(gb_ref[...].astype(F32)) * _dot(ob_ref[...], wb_s[...])
    acc = acc + _sigmoid(gc_ref[...].astype(F32)) * _dot(oc_ref[...], wc_s[...])
    o_ref[...] = acc.astype(o_ref.dtype)


def merge_branches(o_a, o_b, o_c, p_r, w_a, w_b, w_c, layer, bm=512, bn=512):
    S, K = o_a.shape
    g0 = 3 * C_W // bn
    gstep = D_MODEL // bn
    act = pl.BlockSpec((bm, K), lambda j, i: (i, 0))
    wsp = pl.BlockSpec((None, K, bn), lambda j, i: (layer, 0, j))
    gate = lambda b: pl.BlockSpec((bm, bn), lambda j, i: (i, g0 + b * gstep + j))
    return pl.pallas_call(
        _merge_body,
        grid=(D_MODEL // bn, S // bm),
        in_specs=[act, act, act, gate(0), gate(1), gate(2), wsp, wsp, wsp],
        out_specs=pl.BlockSpec((bm, bn), lambda j, i: (i, j)),
        out_shape=jax.ShapeDtypeStruct((S, D_MODEL), BF16),
        scratch_shapes=[pltpu.VMEM((K, bn), BF16)] * 3,
        compiler_params=_cparams("parallel", "arbitrary"),
        name="merge_branches",
    )(o_a, o_b, o_c, p_r, p_r, p_r, w_a, w_b, w_c)


def _new_group(tg_ref):
    i = pl.program_id(1)
    return (i == 0) | (tg_ref[i] != tg_ref[jnp.maximum(i - 1, 0)])


def _swiglu_up_body(tg_ref, nt_ref, x_ref, w1_ref, w3_ref, o_ref, w1_s, w3_s):
    @pl.when(_new_group(tg_ref))
    def _():
        w1_s[...] = w1_ref[...].astype(BF16)
        w3_s[...] = w3_ref[...].astype(BF16)

    @pl.when(pl.program_id(1) < nt_ref[0])
    def _():
        x = x_ref[...]
        o_ref[...] = (_silu(_dot(x, w1_s[...])) * _dot(x, w3_s[...])).astype(o_ref.dtype)

    @pl.when(pl.program_id(1) >= nt_ref[0])
    def _():
        o_ref[...] = jnp.zeros_like(o_ref)


def swiglu_up(x, w1, w3, tile_group, n_tiles, bm, bf):
    R, D = x.shape
    F = w1.shape[2]
    wspec = pl.BlockSpec((None, D, bf), lambda f, i, tg, nt: (tg[i], 0, f))
    return pl.pallas_call(
        _swiglu_up_body,
        grid_spec=pltpu.PrefetchScalarGridSpec(
            num_scalar_prefetch=2,
            grid=(F // bf, R // bm),
            in_specs=[pl.BlockSpec((bm, D), lambda f, i, tg, nt: (jnp.minimum(i, nt[0] - 1), 0)), wspec, wspec],
            out_specs=pl.BlockSpec((bm, bf), lambda f, i, tg, nt: (i, f)),
            scratch_shapes=[pltpu.VMEM((D, bf), BF16)] * 2,
        ),
        out_shape=jax.ShapeDtypeStruct((R, F), BF16),
        compiler_params=_cparams("arbitrary", "arbitrary"),
        name="swiglu_up",
    )(tile_group, n_tiles, x, w1, w3)


def _grouped_mm_body(tg_ref, nt_ref, x_ref, w_ref, *rest, has_acc):
    if has_acc:
        acc_ref, o_ref, w_s = rest
    else:
        o_ref, w_s = rest

    @pl.when(_new_group(tg_ref))
    def _():
        w_s[...] = w_ref[...].astype(BF16)

    @pl.when(pl.program_id(1) < nt_ref[0])
    def _():
        y = _dot(x_ref[...], w_s[...])
        o_ref[...] = acc_ref[...] + y if has_acc else y

    @pl.when(pl.program_id(1) >= nt_ref[0])
    def _():
        o_ref[...] = jnp.zeros_like(o_ref)


def grouped_matmul(x, w, tile_group, n_tiles, bm, bn, k_blocks=1, k_index=0, acc=None):
    R, K = x.shape
    N = w.shape[2]
    kb = K // k_blocks
    in_specs = [pl.BlockSpec((bm, kb), lambda j, i, tg, nt: (jnp.minimum(i, nt[0] - 1), k_index)),
                pl.BlockSpec((None, kb, bn), lambda j, i, tg, nt: (tg[i], k_index, j))]
    out_spec = pl.BlockSpec((bm, bn), lambda j, i, tg, nt: (i, j))
    args = [tile_group, n_tiles, x, w]
    if acc is not None:
        in_specs.append(out_spec)
        args.append(acc)
    return pl.pallas_call(
        functools.partial(_grouped_mm_body, has_acc=acc is not None),
        grid_spec=pltpu.PrefetchScalarGridSpec(
            num_scalar_prefetch=2,
            grid=(N // bn, R // bm),
            in_specs=in_specs,
            out_specs=out_spec,
            scratch_shapes=[pltpu.VMEM((kb, bn), BF16)],
        ),
        out_shape=jax.ShapeDtypeStruct((R, N), F32),
        input_output_aliases={4: 0} if acc is not None else {},
        compiler_params=_cparams("arbitrary", "arbitrary"),
        name="grouped_matmul",
    )(*args)


DENSE_UP_BM = 1024
DENSE_DOWN_BM = 512
DENSE_KSPLIT = 2


def _one_group(n_rows, bm, group):
    return jnp.full((n_rows // bm,), group, jnp.int32), jnp.full((1,), n_rows // bm, jnp.int32)


def dense_ffn(h, w1, w3, w2, index):
    S = h.shape[0]
    g = swiglu_up(h, w1, w3, *_one_group(S, DENSE_UP_BM, index), bm=DENSE_UP_BM, bf=256)
    y = None
    for k in range(DENSE_KSPLIT):
        y = grouped_matmul(g, w2, *_one_group(S, DENSE_DOWN_BM, index), bm=DENSE_DOWN_BM, bn=512,
                           k_blocks=DENSE_KSPLIT, k_index=k, acc=y)
    return y


def _router_body(h_ref, rw_ref, sel_ref, gate_ref):
    logits = _dot(h_ref[...].astype(BF16), rw_ref[...].astype(BF16))
    lane = lax.broadcasted_iota(jnp.int32, logits.shape, 1)
    logits = jnp.where(lane < N_EXPERTS, logits, NEG)
    t1 = jnp.max(logits, axis=1, keepdims=True)
    i1 = jnp.min(jnp.where(logits == t1, lane, LANES), axis=1, keepdims=True)
    rest = jnp.where(lane == i1, NEG, logits)
    t2 = jnp.max(rest, axis=1, keepdims=True)
    i2 = jnp.min(jnp.where(rest == t2, lane, LANES), axis=1, keepdims=True)
    e2 = jnp.exp(t2 - t1)
    g1 = 1.0 / (1.0 + e2)
    sel_ref[...] = jnp.where(lane == 0, i1, jnp.where(lane == 1, i2, 0))
    gate_ref[...] = jnp.where(lane == 0, g1, jnp.where(lane == 1, e2 * g1, 0.0))


def router(h, router_w, bm=512):
    S, D = h.shape
    rw = jnp.pad(router_w, ((0, 0), (0, LANES - N_EXPERTS)))
    out = pl.BlockSpec((bm, LANES), lambda i: (i, 0))
    return pl.pallas_call(
        _router_body,
        grid=(S // bm,),
        in_specs=[pl.BlockSpec((bm, D), lambda i: (i, 0)), pl.BlockSpec((D, LANES), lambda i: (0, 0))],
        out_specs=[out, out],
        out_shape=[jax.ShapeDtypeStruct((S, LANES), jnp.int32), jax.ShapeDtypeStruct((S, LANES), F32)],
        compiler_params=_cparams("parallel"),
        name="router",
    )(h, rw)


MOE_BM = 512
MOE_BG = 256


def _moe_plan(sel, bm):
    S = sel.shape[0]
    E = N_EXPERTS
    e = sel[:, :TOP_K].reshape(-1)
    onehot = (e[:, None] == jnp.arange(E, dtype=jnp.int32)[None, :]).astype(jnp.int32)
    csum = jnp.cumsum(onehot, axis=0)
    rank = jnp.sum((csum - 1) * onehot, axis=1)
    tiles_per = (csum[-1] + bm - 1) // bm
    tile_end = jnp.cumsum(tiles_per)
    pos = ((tile_end - tiles_per)[e] * bm + rank).astype(jnp.int32)
    n_tiles = tile_end[-1:].astype(jnp.int32)
    T = (TOP_K * S) // bm + E
    tid = jnp.arange(T, dtype=jnp.int32)
    te = jnp.minimum(jnp.sum((tid[:, None] >= tile_end[None, :]).astype(jnp.int32), axis=1), E - 1)
    te = jnp.where(tid < n_tiles[0], te, te[n_tiles[0] - 1]).astype(jnp.int32)
    src = jnp.zeros((T * bm,), jnp.int32).at[pos].set(jnp.arange(TOP_K * S, dtype=jnp.int32) // TOP_K)
    return pos, src, te, n_tiles


def _moe_gather_body(src_ref, h_hbm, o_ref, buf_ref, sem):
    base = pl.program_id(0) * MOE_BG

    def row_copy(j):
        return pltpu.make_async_copy(h_hbm.at[pl.ds(src_ref[base + j], 1), :], buf_ref.at[pl.ds(j, 1), :], sem)

    def start(j, c):
        row_copy(j).start()
        return c

    def wait(j, c):
        row_copy(j).wait()
        return c

    lax.fori_loop(0, MOE_BG, start, 0)
    lax.fori_loop(0, MOE_BG, wait, 0)
    o_ref[...] = buf_ref[...].astype(o_ref.dtype)


def moe_gather(h, src):
    S, D = h.shape
    R = src.shape[0]
    return pl.pallas_call(
        _moe_gather_body,
        grid_spec=pltpu.PrefetchScalarGridSpec(
            num_scalar_prefetch=1,
            grid=(R // MOE_BG,),
            in_specs=[pl.BlockSpec(memory_space=pl.ANY)],
            out_specs=pl.BlockSpec((MOE_BG, D), lambda i, src: (i, 0)),
            scratch_shapes=[pltpu.VMEM((MOE_BG, D), F32), pltpu.SemaphoreType.DMA(())],
        ),
        out_shape=jax.ShapeDtypeStruct((R, D), BF16),
        compiler_params=_cparams("arbitrary"),
        name="moe_gather",
    )(src, h)


def _moe_combine_body(pos_ref, gate_ref, ys_hbm, o_ref, buf_ref, sem):
    base = pl.program_id(0) * MOE_BG

    def row_copy(j, k):
        row = pos_ref[(base + j) * TOP_K + k]
        return pltpu.make_async_copy(ys_hbm.at[pl.ds(row, 1), :], buf_ref.at[k, pl.ds(j, 1), :], sem)

    def start(j, c):
        for k in range(TOP_K):
            row_copy(j, k).start()
        return c

    def wait(j, c):
        for k in range(TOP_K):
            row_copy(j, k).wait()
        return c

    lax.fori_loop(0, MOE_BG, start, 0)
    lax.fori_loop(0, MOE_BG, wait, 0)
    gate = gate_ref[...]
    acc = gate[:, 0:1] * buf_ref[0]
    for k in range(1, TOP_K):
        acc = acc + gate[:, k:k + 1] * buf_ref[k]
    o_ref[...] = acc


def moe_combine(ys, pos, gate):
    S = gate.shape[0]
    D = ys.shape[1]
    return pl.pallas_call(
        _moe_combine_body,
        grid_spec=pltpu.PrefetchScalarGridSpec(
            num_scalar_prefetch=1,
            grid=(S // MOE_BG,),
            in_specs=[pl.BlockSpec((MOE_BG, LANES), lambda i, pos: (i, 0)), pl.BlockSpec(memory_space=pl.ANY)],
            out_specs=pl.BlockSpec((MOE_BG, D), lambda i, pos: (i, 0)),
            scratch_shapes=[pltpu.VMEM((TOP_K, MOE_BG, D), F32), pltpu.SemaphoreType.DMA(())],
        ),
        out_shape=jax.ShapeDtypeStruct((S, D), F32),
        compiler_params=_cparams("arbitrary"),
        name="moe_combine",
    )(pos, gate, ys)


def moe_layer(h, router_w, w1, w3, w2):
    sel, gate = router(h, router_w)
    pos, src, tile_expert, n_tiles = _moe_plan(sel, MOE_BM)
    xs = moe_gather(h, src)
    g = swiglu_up(xs, w1, w3, tile_expert, n_tiles, bm=MOE_BM, bf=256)
    ys = grouped_matmul(g, w2, tile_expert, n_tiles, bm=MOE_BM, bn=512)
    return moe_combine(ys, pos, gate)


def _rot_cols(w):
    half = w.shape[-1] // 2
    return jnp.concatenate([-w[..., half:], w[..., :half]], axis=-1)


def _rope_tables(seq, dim):
    inv = ROPE_THETA ** (-jnp.arange(0, dim, 2, dtype=F32) / dim)
    ang = jnp.arange(seq, dtype=F32)[:, None] * inv[None, :]
    return jnp.cos(ang), jnp.sin(ang)


def _small_weight(w_mid):
    col = lambda off, n: w_mid[:, off - OFF_A_IN:off - OFF_A_IN + n]
    kr = col(OFF_KROPE, B_ROPE)
    parts = [col(OFF_CQ, B_Q_LORA), col(OFF_CKV, B_KV_LORA), kr, _rot_cols(kr),
             col(OFF_A_IN, A_HEADS), col(OFF_B_IN, A_HEADS)]
    w = jnp.concatenate(parts, axis=1)
    return jnp.pad(w, ((0, 0), (0, PS_W - w.shape[1])))


def _mla_weights(w_uq, w_ukv):
    wq = w_uq.reshape(B_Q_LORA, B_HEADS, B_NOPE + B_ROPE)
    pe = wq[..., B_NOPE:]
    zeros = jnp.zeros_like(pe)
    wq_ext = jnp.concatenate([wq[..., :B_NOPE], pe, zeros, _rot_cols(pe), zeros], axis=-1)
    return wq_ext.reshape(B_Q_LORA, B_HEADS * B_QH).astype(BF16), w_ukv.astype(BF16)


def hybrid_mixer(h, layer, p, tables):
    cs_b, cos_c, sin_c = tables
    w_in = p["w_in"]
    p_a = matmul(h, w_in, layer, OFF_A_IN, BF16, bm=1024, bn=512)
    p_s = matmul(h, _small_weight(w_in[layer, :, OFF_A_IN:OFF_REST])[None], 0, PS_W, F32, bm=1024, bn=256)
    p_r = matmul_shifted(h, w_in, layer, OFF_REST, REST_W, BF16, bm=1024, bn=512)
    o_a = gdn_branch(p_a, p_s, p["conv_w"][layer], p["gdn_a_log"][layer], p["gdn_dt_bias"][layer],
                     p["gdn_norm"][layer])
    wq_ext, wkv = _mla_weights(p["mla_w_uq"][layer], p["mla_w_ukv"][layer])
    q_cat, k_cat, v_b = mla_prep(p_s, p["mla_q_norm"][layer], p["mla_kv_norm"][layer], wq_ext, wkv, cs_b)
    o_b = mla_attention(q_cat, k_cat, v_b)
    o_c = dilated_branch(p_r, cos_c, sin_c)
    merged = merge_branches(o_a, o_b, o_c, p_r, p["w_branch_a"], p["w_branch_b"], p["w_branch_c"], layer)
    return matmul(merged, p["w_out"], layer, D_MODEL, F32, bm=1024, bn=512)


def kernel(x, attn_norm, w_in, conv_w, gdn_a_log, gdn_dt_bias, gdn_norm, mla_q_norm, mla_w_uq, mla_kv_norm, mla_w_ukv, w_branch_a, w_branch_b, w_branch_c, w_out, ffn_norm, dense_w1, dense_w3, dense_w2, router_w, moe_w1, moe_w3, moe_w2, final_norm):
    p = dict(w_in=w_in, conv_w=conv_w, gdn_a_log=gdn_a_log, gdn_dt_bias=gdn_dt_bias, gdn_norm=gdn_norm,
             mla_q_norm=mla_q_norm, mla_w_uq=mla_w_uq, mla_kv_norm=mla_kv_norm, mla_w_ukv=mla_w_ukv,
             w_branch_a=w_branch_a, w_branch_b=w_branch_b, w_branch_c=w_branch_c, w_out=w_out)
    Bn, S, D = x.shape
    assert Bn == 1 and D == D_MODEL
    cos_b, sin_b = _rope_tables(S, B_ROPE)
    cos_c, sin_c = _rope_tables(S, C_DH)
    tables = (jnp.concatenate([cos_b, cos_b, sin_b, sin_b], axis=1),
              jnp.concatenate([cos_c, cos_c], axis=1), jnp.concatenate([-sin_c, sin_c], axis=1))
    xr = x.reshape(S, D)
    h = rmsnorm(xr, attn_norm[0], BF16)
    out = None
    for layer in range(DEPTH):
        y = hybrid_mixer(h, layer, p, tables)
        i = layer // 2
        if layer % 2 == 0:
            xr, h = add_rmsnorm(xr, y, ffn_norm[layer], BF16)
            y = dense_ffn(h, dense_w1, dense_w3, dense_w2, i)
        else:
            xr, h = add_rmsnorm(xr, y, ffn_norm[layer], F32)
            y = moe_layer(h, router_w[i], moe_w1[i], moe_w3[i], moe_w2[i])
        if layer + 1 < DEPTH:
            xr, h = add_rmsnorm(xr, y, attn_norm[layer + 1], BF16)
        else:
            out = add_rmsnorm_final(xr, y, final_norm)
    return out.reshape(Bn, S, D)
```

```python
import functools
import math

import jax
import jax.numpy as jnp
from jax import lax
from jax.experimental import pallas as pl
from jax.experimental.pallas import tpu as pltpu

F32 = jnp.float32
BF16 = jnp.bfloat16

EPS = 1e-6
ROPE_THETA = 10000.0
NEG = -1e30

LANES = 128
VMEM_LIMIT_BYTES = 56 * 1024 * 1024

D_MODEL = 4096
DEPTH = 2
A_HEADS, A_DK, A_DV, A_CONV, A_CHUNK = 16, 128, 128, 4, 64
B_HEADS, B_Q_LORA, B_KV_LORA, B_NOPE, B_ROPE, B_VDIM = 16, 1024, 512, 128, 64, 128
C_HEADS, C_DH = 16, 128
C_PATTERNS = ((128, 1), (512, 4), (2048, 16))
D_FF = 11008
N_EXPERTS, TOP_K, D_EXPERT = 8, 2, 4096

A_QK_W = A_HEADS * A_DK
A_V_W = A_HEADS * A_DV
A_QKV_W = 2 * A_QK_W + A_V_W
C_W = C_HEADS * C_DH
OFF_A_IN = A_QKV_W + A_V_W
OFF_B_IN = OFF_A_IN + A_HEADS
OFF_CQ = OFF_B_IN + A_HEADS
OFF_CKV = OFF_CQ + B_Q_LORA
OFF_KROPE = OFF_CKV + B_KV_LORA
OFF_REST = OFF_KROPE + B_ROPE
REST_W = 3 * C_W + 3 * D_MODEL
PS_CQ, PS_CKV, PS_KR, PS_AB, PS_W = 0, 1024, 1536, 1664, 1792

B_QH = 384
B_KH = 256


def _cparams(*sem):
    return pltpu.CompilerParams(dimension_semantics=sem, vmem_limit_bytes=VMEM_LIMIT_BYTES)


def _dot(a, b):
    return jnp.dot(a, b, preferred_element_type=F32)


def _dot_nt(a, b):
    return lax.dot_general(a, b, (((1,), (1,)), ((), ())), preferred_element_type=F32)


def _sigmoid(x):
    return 1.0 / (1.0 + jnp.exp(-x))


def _silu(x):
    return x * _sigmoid(x)


def _softplus(x):
    return jnp.maximum(x, 0.0) + jnp.log(1.0 + jnp.exp(-jnp.abs(x)))


def _rmsnorm_body(x_ref, g_ref, o_ref):
    x = x_ref[...]
    y = x * lax.rsqrt(jnp.mean(x * x, axis=-1, keepdims=True) + EPS)
    o_ref[...] = (y * g_ref[...]).astype(o_ref.dtype)


def rmsnorm(x, g, out_dtype, bm=256):
    M, D = x.shape
    return pl.pallas_call(
        _rmsnorm_body,
        grid=(M // bm,),
        in_specs=[pl.BlockSpec((bm, D), lambda i: (i, 0)), pl.BlockSpec((1, D), lambda i: (0, 0))],
        out_specs=pl.BlockSpec((bm, D), lambda i: (i, 0)),
        out_shape=jax.ShapeDtypeStruct((M, D), out_dtype),
        compiler_params=_cparams("parallel"),
        name="rmsnorm",
    )(x, g.reshape(1, D))


def _add_rmsnorm_body(x_ref, y_ref, g_ref, xo_ref, ho_ref):
    x = x_ref[...] + y_ref[...].astype(F32)
    xo_ref[...] = x
    y = x * lax.rsqrt(jnp.mean(x * x, axis=-1, keepdims=True) + EPS)
    ho_ref[...] = (y * g_ref[...]).astype(ho_ref.dtype)


def add_rmsnorm(x, y, g, out_dtype, bm=256):
    M, D = x.shape
    row = pl.BlockSpec((bm, D), lambda i: (i, 0))
    return pl.pallas_call(
        _add_rmsnorm_body,
        grid=(M // bm,),
        in_specs=[row, row, pl.BlockSpec((1, D), lambda i: (0, 0))],
        out_specs=[row, row],
        out_shape=[jax.ShapeDtypeStruct((M, D), F32), jax.ShapeDtypeStruct((M, D), out_dtype)],
        compiler_params=_cparams("parallel"),
        name="add_rmsnorm",
    )(x, y, g.reshape(1, D))


def _add_rmsnorm_final_body(x_ref, y_ref, g_ref, o_ref):
    x = x_ref[...] + y_ref[...].astype(F32)
    y = x * lax.rsqrt(jnp.mean(x * x, axis=-1, keepdims=True) + EPS)
    o_ref[...] = y * g_ref[...]


def add_rmsnorm_final(x, y, g, bm=256):
    M, D = x.shape
    row = pl.BlockSpec((bm, D), lambda i: (i, 0))
    return pl.pallas_call(
        _add_rmsnorm_final_body,
        grid=(M // bm,),
        in_specs=[row, row, pl.BlockSpec((1, D), lambda i: (0, 0))],
        out_specs=row,
        out_shape=jax.ShapeDtypeStruct((M, D), F32),
        compiler_params=_cparams("parallel"),
        name="add_rmsnorm_final",
    )(x, y, g.reshape(1, D))


def _mm_body(x_ref, w_ref, o_ref, wbf_ref):
    @pl.when(pl.program_id(1) == 0)
    def _():
        wbf_ref[...] = w_ref[...].astype(BF16)

    o_ref[...] = _dot(x_ref[...], wbf_ref[...]).astype(o_ref.dtype)


def matmul(x, w, layer, n_cols, out_dtype, bm, bn):
    M, K = x.shape
    return pl.pallas_call(
        _mm_body,
        grid=(n_cols // bn, M // bm),
        in_specs=[pl.BlockSpec((bm, K), lambda j, i: (i, 0)),
                  pl.BlockSpec((None, K, bn), lambda j, i: (layer, 0, j))],
        out_specs=pl.BlockSpec((bm, bn), lambda j, i: (i, j)),
        out_shape=jax.ShapeDtypeStruct((M, n_cols), out_dtype),
        scratch_shapes=[pltpu.VMEM((K, bn), BF16)],
        compiler_params=_cparams("parallel", "arbitrary"),
        name="matmul",
    )(x, w)


def _mm_nt_body(x_ref, wt_ref, o_ref, wbf_ref):
    @pl.when(pl.program_id(1) == 0)
    def _():
        wbf_ref[...] = wt_ref[...].astype(BF16)

    o_ref[...] = _dot_nt(x_ref[...], wbf_ref[...]).astype(o_ref.dtype)


def matmul_nt(x, wt, layer, n_cols, out_dtype, bm, bn):
    M, K = x.shape
    return pl.pallas_call(
        _mm_nt_body,
        grid=(n_cols // bn, M // bm),
        in_specs=[pl.BlockSpec((bm, K), lambda j, i: (i, 0)),
                  pl.BlockSpec((None, bn, K), lambda j, i: (layer, j, 0))],
        out_specs=pl.BlockSpec((bm, bn), lambda j, i: (i, j)),
        out_shape=jax.ShapeDtypeStruct((M, n_cols), out_dtype),
        scratch_shapes=[pltpu.VMEM((bn, K), BF16)],
        compiler_params=_cparams("parallel", "arbitrary"),
        name="matmul_nt",
    )(x, wt)


MM_TAIL = 128


def _mm_nt_shifted_body(x_ref, wa_ref, wb_ref, o_ref, wbf_ref, *, shift):
    @pl.when(pl.program_id(1) == 0)
    def _():
        keep = wa_ref.shape[0] - shift
        wbf_ref[:keep, :] = wa_ref[shift:, :].astype(BF16)
        wbf_ref[keep:, :] = wb_ref[:shift, :].astype(BF16)

    o_ref[...] = _dot_nt(x_ref[...], wbf_ref[...]).astype(o_ref.dtype)


def matmul_nt_shifted(x, wt, layer, row0, n_cols, out_dtype, bm, bn):
    M, K = x.shape
    base, shift = divmod(row0, MM_TAIL)
    assert (base * MM_TAIL) % bn == 0 and 0 < shift and shift % 16 == 0
    ja = base * MM_TAIL // bn
    per = bn // MM_TAIL
    return pl.pallas_call(
        functools.partial(_mm_nt_shifted_body, shift=shift),
        grid=(n_cols // bn, M // bm),
        in_specs=[pl.BlockSpec((bm, K), lambda j, i: (i, 0)),
                  pl.BlockSpec((None, bn, K), lambda j, i: (layer, ja + j, 0)),
                  pl.BlockSpec((None, MM_TAIL, K), lambda j, i: (layer, base + per * (j + 1), 0))],
        out_specs=pl.BlockSpec((bm, bn), lambda j, i: (i, j)),
        out_shape=jax.ShapeDtypeStruct((M, n_cols), out_dtype),
        scratch_shapes=[pltpu.VMEM((bn, K), BF16)],
        compiler_params=_cparams("parallel", "arbitrary"),
        name="matmul_nt_shifted",
    )(x, wt, wt)


GDN_ROWS = 256
GDN_HB = 8
GDN_PREV = 16


def _gdn_body(q_ref, k_ref, v_ref, qp_ref, kp_ref, vp_ref, z_ref, ab_ref, cwq_ref, cwk_ref, cwv_ref,
              alog_ref, dtb_ref, nw_ref, o_ref, state_ref):
    hb = pl.program_id(0)
    i = pl.program_id(1)
    R = GDN_ROWS
    C = A_CHUNK
    NC = R // C

    @pl.when(i == 0)
    def _():
        state_ref[...] = jnp.zeros_like(state_ref)

    has_prev = (i > 0).astype(F32)
    lane = lax.broadcasted_iota(jnp.int32, (R, LANES), 1)
    row = lax.broadcasted_iota(jnp.int32, (R, LANES), 0)
    rin = row & (C - 1)
    ri = lax.broadcasted_iota(jnp.int32, (R, R), 0)
    ci = lax.broadcasted_iota(jnp.int32, (R, R), 1)
    same = (ri // C) == (ci // C)
    incl = same & (ri >= ci)
    strict = same & (ri > ci)

    ab = ab_ref[...]
    g_all = -jnp.exp(alog_ref[...]) * _softplus(ab + dtb_ref[...])
    beta_all = _sigmoid(ab)
    nw = nw_ref[...]

    def conv_silu(cur_ref, prev_ref, cw_ref, sl):
        cur = cur_ref[:, sl].astype(F32)
        prev = prev_ref[:, sl].astype(F32) * has_prev
        full = jnp.concatenate([prev, cur], axis=0)
        cw = cw_ref[:, sl]
        y = cw[A_CONV - 1:A_CONV, :] * cur
        for s in range(1, A_CONV):
            y = y + cw[A_CONV - 1 - s:A_CONV - s, :] * pltpu.roll(full, s, axis=0)[GDN_PREV:, :]
        return _silu(y)

    heads = range(GDN_HB)
    lanes_of = [slice(hh * LANES, (hh + 1) * LANES) for hh in heads]
    st = []
    for hh in heads:
        h = hb * GDN_HB + hh
        sl = lanes_of[hh]
        g = jnp.broadcast_to(jnp.sum(jnp.where(lane == h, g_all, 0.0), axis=1, keepdims=True), (R, LANES))
        beta = jnp.broadcast_to(
            jnp.sum(jnp.where(lane == h + A_HEADS, beta_all, 0.0), axis=1, keepdims=True), (R, LANES))
        gc = g
        s = 1
        while s < C:
            gc = gc + jnp.where(rin >= s, pltpu.roll(gc, s, axis=0), 0.0)
            s *= 2
        g_last = jnp.broadcast_to(jnp.sum(g.reshape(NC, C, LANES), axis=1, keepdims=True),
                                  (NC, C, LANES)).reshape(R, LANES)

        q = conv_silu(q_ref, qp_ref, cwq_ref, sl)
        k = conv_silu(k_ref, kp_ref, cwk_ref, sl)
        v = conv_silu(v_ref, vp_ref, cwv_ref, sl)
        qn = q * lax.rsqrt(jnp.sum(q * q, axis=-1, keepdims=True) + EPS) * (A_DK ** -0.5)
        kn = k * lax.rsqrt(jnp.sum(k * k, axis=-1, keepdims=True) + EPS)
        kb = kn * beta
        eg = jnp.exp(gc)

        gc_t = gc.T
        diff = jnp.concatenate([gc, gc], axis=1) - jnp.concatenate([gc_t, gc_t], axis=0)
        decay = jnp.exp(jnp.where(incl, diff, NEG))
        kn_b = kn.astype(BF16)
        lower = jnp.where(strict, _dot_nt(kb.astype(BF16), kn_b) * decay, 0.0)
        st.append(dict(
            xp=-lower, qmat=-lower,
            qk=(_dot_nt(qn.astype(BF16), kn_b) * decay).astype(BF16),
            rhs=jnp.concatenate([v * beta, kb * eg], axis=1),
            q_dec=(qn * eg).astype(BF16),
            k_dec=kn * jnp.exp(g_last - gc),
            g_last=g_last,
        ))

    for _ in range(int(math.log2(C)) - 1):
        for t in st:
            xb = t["xp"].astype(BF16)
            t["xp"] = _dot(xb, xb)
            t["qmat"] = t["qmat"] + t["xp"] + _dot(t["qmat"].astype(BF16), t["xp"].astype(BF16))
    for t in st:
        sol = t["rhs"] + _dot(t["qmat"].astype(BF16), t["rhs"].astype(BF16))
        t["u"] = sol[:, :A_DV]
        t["w_b"] = sol[:, A_DV:].astype(BF16)
        t["outs"] = []

    states = [state_ref[hh] for hh in heads]
    for c in range(NC):
        rs = slice(c * C, (c + 1) * C)
        for hh, t in enumerate(st):
            sb = states[hh].astype(BF16)
            v_new_b = (t["u"][rs] - _dot(t["w_b"][rs], sb)).astype(BF16)
            t["outs"].append(_dot(t["q_dec"][rs], sb) + _dot(t["qk"][rs, :][:, rs], v_new_b))
            states[hh] = (states[hh] * jnp.exp(t["g_last"][c * C:c * C + 1, :])
                          + _dot(t["k_dec"][rs].T.astype(BF16), v_new_b))
    for hh, t in enumerate(st):
        sl = lanes_of[hh]
        state_ref[hh] = states[hh]
        out = jnp.concatenate(t["outs"], axis=0)
        out = out * lax.rsqrt(jnp.mean(out * out, axis=-1, keepdims=True) + EPS) * nw
        o_ref[:, sl] = (out * _silu(z_ref[:, sl].astype(F32))).astype(o_ref.dtype)


def gdn_branch(p_a, p_s, conv_w, a_log, dt_bias, norm_w):
    S = p_a.shape[0]
    R, W = GDN_ROWS, GDN_HB * LANES
    nq = A_QK_W // W
    pad = lambda a: jnp.pad(a.astype(F32), (0, LANES - A_HEADS)).reshape(1, LANES)

    def cur(off):
        return pl.BlockSpec((R, W), lambda hb, i: (i, off + hb))

    def prev(off):
        return pl.BlockSpec((GDN_PREV, W), lambda hb, i: (jnp.maximum(i * (R // GDN_PREV) - 1, 0), off + hb))

    def cw(off):
        return pl.BlockSpec((A_CONV, W), lambda hb, i: (0, off + hb))

    vec = pl.BlockSpec((1, LANES), lambda hb, i: (0, 0))
    return pl.pallas_call(
        _gdn_body,
        grid=(A_HEADS // GDN_HB, S // R),
        in_specs=[cur(0), cur(nq), cur(2 * nq), prev(0), prev(nq), prev(2 * nq), cur(3 * nq),
                  pl.BlockSpec((R, LANES), lambda hb, i: (i, PS_AB // LANES)),
                  cw(0), cw(nq), cw(2 * nq), vec, vec, vec],
        out_specs=pl.BlockSpec((R, W), lambda hb, i: (i, hb)),
        out_shape=jax.ShapeDtypeStruct((S, A_V_W), BF16),
        scratch_shapes=[pltpu.VMEM((GDN_HB, A_DK, A_DV), F32)],
        compiler_params=_cparams("parallel", "arbitrary"),
        name="gdn",
    )(p_a, p_a, p_a, p_a, p_a, p_a, p_a, p_s, conv_w, conv_w, conv_w, pad(a_log), pad(dt_bias),
      norm_w.astype(F32).reshape(1, LANES))


MLA_HG = 4


def _mla_prep_body(cq_ref, ckv_ref, kr_ref, qn_ref, kvn_ref, wq_ref, wkv_ref, cs_ref, qo_ref, ko_ref, vo_ref):
    scale = (B_NOPE + B_ROPE) ** -0.5 * math.log2(math.e)
    cq = cq_ref[...]
    cq = (cq * lax.rsqrt(jnp.mean(cq * cq, axis=-1, keepdims=True) + EPS) * qn_ref[...]).astype(BF16)
    ckv = ckv_ref[...]
    ckv = (ckv * lax.rsqrt(jnp.mean(ckv * ckv, axis=-1, keepdims=True) + EPS) * kvn_ref[...]).astype(BF16)
    rq = _dot(cq, wq_ref[...])
    rkv = _dot(ckv, wkv_ref[...])
    cs = cs_ref[...]
    sc = pltpu.roll(cs, B_ROPE, axis=1)
    prod = kr_ref[...] * cs
    lane = lax.broadcasted_iota(jnp.int32, prod.shape, 1)
    k_pe = jnp.where(lane < B_ROPE, prod + pltpu.roll(prod, B_ROPE, axis=1), 0.0).astype(BF16)
    for hh in range(MLA_HG):
        qb = hh * B_QH
        roped = rq[:, qb + LANES:qb + 2 * LANES] * cs + rq[:, qb + 2 * LANES:qb + 3 * LANES] * sc
        qo_ref[:, hh * B_KH:hh * B_KH + LANES] = (rq[:, qb:qb + LANES] * scale).astype(BF16)
        qo_ref[:, hh * B_KH + LANES:(hh + 1) * B_KH] = (roped * scale).astype(BF16)
        ko_ref[:, hh * B_KH:hh * B_KH + LANES] = rkv[:, hh * 2 * LANES:hh * 2 * LANES + LANES].astype(BF16)
        ko_ref[:, hh * B_KH + LANES:(hh + 1) * B_KH] = k_pe
        vo_ref[:, hh * 2 * LANES:hh * 2 * LANES + LANES] = (
            rkv[:, hh * 2 * LANES + LANES:(hh + 1) * 2 * LANES].astype(BF16))
        vo_ref[:, hh * 2 * LANES + LANES:(hh + 1) * 2 * LANES] = jnp.ones((rkv.shape[0], LANES), BF16)


def mla_prep(p_s, q_norm, kv_norm, wq_ext, wkv, cs_b, bm=512):
    S = p_s.shape[0]
    HG = MLA_HG
    return pl.pallas_call(
        _mla_prep_body,
        grid=(B_HEADS // HG, S // bm),
        in_specs=[
            pl.BlockSpec((bm, B_Q_LORA), lambda g, i: (i, PS_CQ // B_Q_LORA)),
            pl.BlockSpec((bm, B_KV_LORA), lambda g, i: (i, PS_CKV // B_KV_LORA)),
            pl.BlockSpec((bm, LANES), lambda g, i: (i, PS_KR // LANES)),
            pl.BlockSpec((1, B_Q_LORA), lambda g, i: (0, 0)),
            pl.BlockSpec((1, B_KV_LORA), lambda g, i: (0, 0)),
            pl.BlockSpec((B_Q_LORA, HG * B_QH), lambda g, i: (0, g)),
            pl.BlockSpec((B_KV_LORA, HG * 2 * LANES), lambda g, i: (0, g)),
            pl.BlockSpec((bm, LANES), lambda g, i: (i, 0)),
        ],
        out_specs=[
            pl.BlockSpec((bm, HG * B_KH), lambda g, i: (i, g)),
            pl.BlockSpec((bm, HG * B_KH), lambda g, i: (i, g)),
            pl.BlockSpec((bm, HG * 2 * LANES), lambda g, i: (i, g)),
        ],
        out_shape=[
            jax.ShapeDtypeStruct((S, B_HEADS * B_KH), BF16),
            jax.ShapeDtypeStruct((S, B_HEADS * B_KH), BF16),
            jax.ShapeDtypeStruct((S, B_HEADS * 2 * B_VDIM), BF16),
        ],
        compiler_params=_cparams("parallel", "parallel"),
        name="mla_prep",
    )(p_s, p_s, p_s, q_norm.reshape(1, -1), kv_norm.reshape(1, -1), wq_ext, wkv, cs_b)


FLASH_BLK = 1024
FLASH_RC = 256


def _flash_body(qi_ref, kj_ref, q_ref, k_ref, v_ref, o_ref, m_ref, acc_ref):
    t = pl.program_id(1)
    qi = qi_ref[t]
    kj = kj_ref[t]

    @pl.when(kj == 0)
    def _():
        m_ref[...] = jnp.full_like(m_ref, NEG)
        acc_ref[...] = jnp.zeros_like(acc_ref)

    def step(diag):
        nc = FLASH_BLK // FLASH_RC
        rows_of = [slice(c * FLASH_RC, (c + 1) * FLASH_RC) for c in range(nc)]
        ncol_of = [(c + 1) * FLASH_RC if diag else FLASH_BLK for c in range(nc)]

        def scores(c):
            s = _dot_nt(q_ref[rows_of[c], :], k_ref[:ncol_of[c], :])
            if diag:
                r = lax.broadcasted_iota(jnp.int32, s.shape, 0) + c * FLASH_RC
                col = lax.broadcasted_iota(jnp.int32, s.shape, 1)
                s = jnp.where(col <= r, s, NEG)
            return s

        s_next = scores(0)
        for c in range(nc):
            s = s_next
            if c + 1 < nc:
                s_next = scores(c + 1)
            rows = rows_of[c]
            m_prev = m_ref[rows, :]
            m_new = jnp.maximum(m_prev, jnp.max(s, axis=1, keepdims=True))
            p = jnp.exp2(s - m_new).astype(BF16)
            acc_ref[rows, :] = jnp.exp2(m_prev - m_new) * acc_ref[rows, :] + _dot(p, v_ref[:ncol_of[c], :])
            m_ref[rows, :] = m_new

    @pl.when(kj < qi)
    def _():
        step(False)

    @pl.when(kj == qi)
    def _():
        step(True)
        acc = acc_ref[...]
        o_ref[...] = (acc[:, :B_VDIM] / acc[:, B_VDIM:]).astype(o_ref.dtype)


def mla_attention(q_cat, k_cat, v_ext):
    S = q_cat.shape[0]
    blk = FLASH_BLK
    n = S // blk
    pairs = [(qi, kj) for qi in range(n) for kj in range(qi + 1)]
    qi_tab = jnp.asarray([p[0] for p in pairs], jnp.int32)
    kj_tab = jnp.asarray([p[1] for p in pairs], jnp.int32)
    return pl.pallas_call(
        _flash_body,
        grid_spec=pltpu.PrefetchScalarGridSpec(
            num_scalar_prefetch=2,
            grid=(B_HEADS, len(pairs)),
            in_specs=[
                pl.BlockSpec((blk, B_KH), lambda h, t, qi, kj: (qi[t], h)),
                pl.BlockSpec((blk, B_KH), lambda h, t, qi, kj: (kj[t], h)),
                pl.BlockSpec((blk, 2 * B_VDIM), lambda h, t, qi, kj: (kj[t], h)),
            ],
            out_specs=pl.BlockSpec((blk, B_VDIM), lambda h, t, qi, kj: (qi[t], h)),
            scratch_shapes=[pltpu.VMEM((blk, 1), F32), pltpu.VMEM((blk, 2 * B_VDIM), F32)],
        ),
        out_shape=jax.ShapeDtypeStruct((S, B_HEADS * B_VDIM), BF16),
        compiler_params=_cparams("parallel", "arbitrary"),
        name="mla_flash",
    )(qi_tab, kj_tab, q_cat, k_cat, v_ext)


DIL_BLK = 128
DIL_ROWS = 256
DILATIONS = tuple(d for _, d in C_PATTERNS)


def _rope_split_body(q_ref, k_ref, v_ref, c_ref, s_ref, *refs):
    nd = len(DILATIONS)
    outs, (sq, sk, sv) = refs[:3 * nd], refs[3 * nd:]
    c = c_ref[...]
    s = s_ref[...]
    scale = C_DH ** -0.5
    for h in range(C_HEADS):
        sl = slice(h * C_DH, (h + 1) * C_DH)
        x = q_ref[:, sl].astype(F32)
        sq[h] = (x * c + pltpu.roll(x, C_DH // 2, axis=1) * s) * scale
        x = k_ref[:, sl].astype(F32)
        sk[h] = x * c + pltpu.roll(x, C_DH // 2, axis=1) * s
        sv[h] = v_ref[:, sl].astype(F32)
        for di, d in enumerate(DILATIONS):
            for src, dst in zip((sq, sk, sv), outs[3 * di:3 * di + 3]):
                for r in range(d):
                    rows = src[h] if d == 1 else src[h, pl.ds(r, DIL_ROWS // d, stride=d), :]
                    dst[r, :, sl] = rows.astype(BF16)


def rope_split(p_r, cos_c, sin_c):
    S = p_r.shape[0]
    bm = DIL_ROWS
    tab = pl.BlockSpec((bm, C_DH), lambda i: (i, 0))
    col = lambda j: pl.BlockSpec((bm, C_W), lambda i: (i, j))
    out_specs, out_shape = [], []
    for d in DILATIONS:
        out_specs += [pl.BlockSpec((d, bm // d, C_W), lambda i: (0, i, 0))] * 3
        out_shape += [jax.ShapeDtypeStruct((d, S // d, C_W), BF16)] * 3
    outs = pl.pallas_call(
        _rope_split_body,
        grid=(S // bm,),
        in_specs=[col(0), col(1), col(2), tab, tab],
        out_specs=out_specs,
        out_shape=out_shape,
        scratch_shapes=[pltpu.VMEM((C_HEADS, bm, C_DH), F32)] * 3,
        compiler_params=_cparams("parallel"),
        name="rope_split",
    )(p_r, p_r, p_r, cos_c, sin_c)
    return [outs[3 * i:3 * i + 3] for i in range(len(DILATIONS))]


def _dilated_body(q_ref, kc_ref, kp_ref, vc_ref, vp_ref, o_ref, lse_ref):
    B = DIL_BLK
    n = pl.program_id(1)
    qi = lax.broadcasted_iota(jnp.int32, (B, 2 * B), 0)
    kj = lax.broadcasted_iota(jnp.int32, (B, 2 * B), 1)
    dist = qi + B - kj
    valid = (dist >= 0) & (dist <= B) & ((kj >= B) | (n > 0))
    lanes_of = [slice(h * C_DH, (h + 1) * C_DH) for h in range(C_HEADS)]
    scores = [jnp.where(valid, _dot_nt(q_ref[:, sl], jnp.concatenate([kp_ref[:, sl], kc_ref[:, sl]], axis=0)), NEG)
              for sl in lanes_of]
    probs = []
    lane = lax.broadcasted_iota(jnp.int32, (B, LANES), 1)
    lse = jnp.zeros((B, LANES), F32)
    for h, s in enumerate(scores):
        m = jnp.max(s, axis=1, keepdims=True)
        p = jnp.exp(s - m)
        l = jnp.sum(p, axis=1, keepdims=True)
        lse = jnp.where(lane == h, m + jnp.log(l), lse)
        probs.append((p.astype(BF16), l))
    lse_ref[...] = lse
    for sl, (p, l) in zip(lanes_of, probs):
        vv = jnp.concatenate([vp_ref[:, sl], vc_ref[:, sl]], axis=0)
        o_ref[:, sl] = (_dot(p, vv) / l).astype(o_ref.dtype)


def dilated_pattern(q, k, v):
    d, L, W = q.shape
    B = DIL_BLK
    cur = pl.BlockSpec((None, B, W), lambda r, n: (r, n, 0))
    prv = pl.BlockSpec((None, B, W), lambda r, n: (r, jnp.maximum(n - 1, 0), 0))
    return pl.pallas_call(
        _dilated_body,
        grid=(d, L // B),
        in_specs=[cur, cur, prv, cur, prv],
        out_specs=[cur, pl.BlockSpec((None, B, LANES), lambda r, n: (r, n, 0))],
        out_shape=[jax.ShapeDtypeStruct((d, L, W), BF16), jax.ShapeDtypeStruct((d, L, LANES), F32)],
        compiler_params=_cparams("parallel", "parallel"),
        name=f"dilated_d{d}",
    )(q, k, k, v, v)


def _dilated_merge_body(*refs):
    nd = len(DILATIONS)
    o_refs, lse_refs, out_ref = refs[:nd], refs[nd:2 * nd], refs[2 * nd]
    so, sl = refs[2 * nd + 1:]
    rows_of = lambda d, r: slice(None) if d == 1 else pl.ds(r, DIL_ROWS // d, stride=d)
    for di, d in enumerate(DILATIONS):
        for r in range(d):
            sl[di, rows_of(d, r), :] = lse_refs[di][r]
    top = sl[0]
    for di in range(1, nd):
        top = jnp.maximum(top, sl[di])
    weights = [jnp.exp(sl[di] - top) for di in range(nd)]
    lane = lax.broadcasted_iota(jnp.int32, top.shape, 1)
    for h in range(C_HEADS):
        cols = slice(h * C_DH, (h + 1) * C_DH)
        for di, d in enumerate(DILATIONS):
            for r in range(d):
                so[di, rows_of(d, r), :] = o_refs[di][r, :, cols].astype(F32)
        num = 0.0
        den = 0.0
        for di in range(nd):
            w = jnp.sum(jnp.where(lane == h, weights[di], 0.0), axis=1, keepdims=True)
            num = num + w * so[di]
            den = den + w
        out_ref[:, cols] = (num / den).astype(out_ref.dtype)


def dilated_merge(outs, lses):
    S = outs[0].shape[0] * outs[0].shape[1]
    bm = DIL_ROWS
    nd = len(DILATIONS)
    specs = [pl.BlockSpec((d, bm // d, C_W), lambda i: (0, i, 0)) for d in DILATIONS]
    lse_specs = [pl.BlockSpec((d, bm // d, LANES), lambda i: (0, i, 0)) for d in DILATIONS]
    return pl.pallas_call(
        _dilated_merge_body,
        grid=(S // bm,),
        in_specs=specs + lse_specs,
        out_specs=pl.BlockSpec((bm, C_W), lambda i: (i, 0)),
        out_shape=jax.ShapeDtypeStruct((S, C_W), BF16),
        scratch_shapes=[pltpu.VMEM((nd, bm, C_DH), F32)] * 2,
        compiler_params=_cparams("parallel"),
        name="dilated_merge",
    )(*outs, *lses)


def dilated_branch(p_r, cos_c, sin_c):
    assert all(w // d == DIL_BLK for w, d in C_PATTERNS)
    results = [dilated_pattern(q, k, v) for q, k, v in rope_split(p_r, cos_c, sin_c)]
    return dilated_merge([r[0] for r in results], [r[1] for r in results])


def _merge_body(oa_ref, ob_ref, oc_ref, ga_ref, gb_ref, gc_ref, wa_ref, wb_ref, wc_ref, o_ref, wa_s, wb_s, wc_s):
    @pl.when(pl.program_id(1) == 0)
    def _():
        wa_s[...] = wa_ref[...].astype(BF16)
        wb_s[...] = wb_ref[...].astype(BF16)
        wc_s[...] = wc_ref[...].astype(BF16)

    acc = _sigmoid(ga_ref[...].astype(F32)) * _dot(oa_ref[...], wa_s[...])
    acc = acc + _sigmoid(gb_ref[...].astype(F32)) * _dot(ob_ref[...], wb_s[...])
    acc = acc + _sigmoid(gc_ref[...].astype(F32)) * _dot(oc_ref[...], wc_s[...])
    o_ref[...] = acc.astype(o_ref.dtype)


def merge_branches(o_a, o_b, o_c, p_r, w_a, w_b, w_c, layer, bm=512, bn=512):
    S, K = o_a.shape
    g0 = 3 * C_W // bn
    gstep = D_MODEL // bn
    act = pl.BlockSpec((bm, K), lambda j, i: (i, 0))
    wsp = pl.BlockSpec((None, K, bn), lambda j, i: (layer, 0, j))
    gate = lambda b: pl.BlockSpec((bm, bn), lambda j, i: (i, g0 + b * gstep + j))
    return pl.pallas_call(
        _merge_body,
        grid=(D_MODEL // bn, S // bm),
        in_specs=[act, act, act, gate(0), gate(1), gate(2), wsp, wsp, wsp],
        out_specs=pl.BlockSpec((bm, bn), lambda j, i: (i, j)),
        out_shape=jax.ShapeDtypeStruct((S, D_MODEL), BF16),
        scratch_shapes=[pltpu.VMEM((K, bn), BF16)] * 3,
        compiler_params=_cparams("parallel", "arbitrary"),
        name="merge_branches",
    )(o_a, o_b, o_c, p_r, p_r, p_r, w_a, w_b, w_c)


def _new_group(tg_ref):
    i = pl.program_id(1)
    return (i == 0) | (tg_ref[i] != tg_ref[jnp.maximum(i - 1, 0)])


def _swiglu_up_body(tg_ref, nt_ref, x_ref, w1_ref, w3_ref, o_ref, w1_s, w3_s):
    @pl.when(_new_group(tg_ref))
    def _():
        w1_s[...] = w1_ref[...].astype(BF16)
        w3_s[...] = w3_ref[...].astype(BF16)

    @pl.when(pl.program_id(1) < nt_ref[0])
    def _():
        x = x_ref[...]
        o_ref[...] = (_silu(_dot(x, w1_s[...])) * _dot(x, w3_s[...])).astype(o_ref.dtype)

    @pl.when(pl.program_id(1) >= nt_ref[0])
    def _():
        o_ref[...] = jnp.zeros_like(o_ref)


def swiglu_up(x, w1, w3, tile_group, n_tiles, bm, bf):
    R, D = x.shape
    F = w1.shape[2]
    wspec = pl.BlockSpec((None, D, bf), lambda f, i, tg, nt: (tg[i], 0, f))
    return pl.pallas_call(
        _swiglu_up_body,
        grid_spec=pltpu.PrefetchScalarGridSpec(
            num_scalar_prefetch=2,
            grid=(F // bf, R // bm),
            in_specs=[pl.BlockSpec((bm, D), lambda f, i, tg, nt: (jnp.minimum(i, nt[0] - 1), 0)), wspec, wspec],
            out_specs=pl.BlockSpec((bm, bf), lambda f, i, tg, nt: (i, f)),
            scratch_shapes=[pltpu.VMEM((D, bf), BF16)] * 2,
        ),
        out_shape=jax.ShapeDtypeStruct((R, F), BF16),
        compiler_params=_cparams("arbitrary", "arbitrary"),
        name="swiglu_up",
    )(tile_group, n_tiles, x, w1, w3)


def _grouped_mm_body(tg_ref, nt_ref, x_ref, w_ref, *rest, has_acc):
    if has_acc:
        acc_ref, o_ref, w_s = rest
    else:
        o_ref, w_s = rest

    @pl.when(_new_group(tg_ref))
    def _():
        w_s[...] = w_ref[...].astype(BF16)

    @pl.when(pl.program_id(1) < nt_ref[0])
    def _():
        y = _dot(x_ref[...], w_s[...])
        o_ref[...] = acc_ref[...] + y if has_acc else y

    @pl.when(pl.program_id(1) >= nt_ref[0])
    def _():
        o_ref[...] = jnp.zeros_like(o_ref)


def grouped_matmul(x, w, tile_group, n_tiles, bm, bn, k_blocks=1, k_index=0, acc=None):
    R, K = x.shape
    N = w.shape[2]
    kb = K // k_blocks
    in_specs = [pl.BlockSpec((bm, kb), lambda j, i, tg, nt: (jnp.minimum(i, nt[0] - 1), k_index)),
                pl.BlockSpec((None, kb, bn), lambda j, i, tg, nt: (tg[i], k_index, j))]
    out_spec = pl.BlockSpec((bm, bn), lambda j, i, tg, nt: (i, j))
    args = [tile_group, n_tiles, x, w]
    if acc is not None:
        in_specs.append(out_spec)
        args.append(acc)
    return pl.pallas_call(
        functools.partial(_grouped_mm_body, has_acc=acc is not None),
        grid_spec=pltpu.PrefetchScalarGridSpec(
            num_scalar_prefetch=2,
            grid=(N // bn, R // bm),
            in_specs=in_specs,
            out_specs=out_spec,
            scratch_shapes=[pltpu.VMEM((kb, bn), BF16)],
        ),
        out_shape=jax.ShapeDtypeStruct((R, N), F32),
        input_output_aliases={4: 0} if acc is not None else {},
        compiler_params=_cparams("arbitrary", "arbitrary"),
        name="grouped_matmul",
    )(*args)


DENSE_UP_BM = 1024
DENSE_DOWN_BM = 512
DENSE_KSPLIT = 2


def _one_group(n_rows, bm, group):
    return jnp.full((n_rows // bm,), group, jnp.int32), jnp.full((1,), n_rows // bm, jnp.int32)


def dense_ffn(h, w1, w3, w2, index):
    S = h.shape[0]
    g = swiglu_up(h, w1, w3, *_one_group(S, DENSE_UP_BM, index), bm=DENSE_UP_BM, bf=256)
    y = None
    for k in range(DENSE_KSPLIT):
        y = grouped_matmul(g, w2, *_one_group(S, DENSE_DOWN_BM, index), bm=DENSE_DOWN_BM, bn=512,
                           k_blocks=DENSE_KSPLIT, k_index=k, acc=y)
    return y


def _router_body(h_ref, rw_ref, sel_ref, gate_ref):
    logits = _dot(h_ref[...].astype(BF16), rw_ref[...].astype(BF16))
    lane = lax.broadcasted_iota(jnp.int32, logits.shape, 1)
    logits = jnp.where(lane < N_EXPERTS, logits, NEG)
    t1 = jnp.max(logits, axis=1, keepdims=True)
    i1 = jnp.min(jnp.where(logits == t1, lane, LANES), axis=1, keepdims=True)
    rest = jnp.where(lane == i1, NEG, logits)
    t2 = jnp.max(rest, axis=1, keepdims=True)
    i2 = jnp.min(jnp.where(rest == t2, lane, LANES), axis=1, keepdims=True)
    e2 = jnp.exp(t2 - t1)
    g1 = 1.0 / (1.0 + e2)
    sel_ref[...] = jnp.where(lane == 0, i1, jnp.where(lane == 1, i2, 0))
    gate_ref[...] = jnp.where(lane == 0, g1, jnp.where(lane == 1, e2 * g1, 0.0))


def router(h, router_w, bm=512):
    S, D = h.shape
    rw = jnp.pad(router_w, ((0, 0), (0, LANES - N_EXPERTS)))
    out = pl.BlockSpec((bm, LANES), lambda i: (i, 0))
    return pl.pallas_call(
        _router_body,
        grid=(S // bm,),
        in_specs=[pl.BlockSpec((bm, D), lambda i: (i, 0)), pl.BlockSpec((D, LANES), lambda i: (0, 0))],
        out_specs=[out, out],
        out_shape=[jax.ShapeDtypeStruct((S, LANES), jnp.int32), jax.ShapeDtypeStruct((S, LANES), F32)],
        compiler_params=_cparams("parallel"),
        name="router",
    )(h, rw)


MOE_BM = 512
MOE_BG = 256


def _moe_plan(sel, bm):
    S = sel.shape[0]
    E = N_EXPERTS
    e = sel[:, :TOP_K].reshape(-1)
    onehot = (e[:, None] == jnp.arange(E, dtype=jnp.int32)[None, :]).astype(jnp.int32)
    csum = jnp.cumsum(onehot, axis=0)
    rank = jnp.sum((csum - 1) * onehot, axis=1)
    tiles_per = (csum[-1] + bm - 1) // bm
    tile_end = jnp.cumsum(tiles_per)
    pos = ((tile_end - tiles_per)[e] * bm + rank).astype(jnp.int32)
    n_tiles = tile_end[-1:].astype(jnp.int32)
    T = (TOP_K * S) // bm + E
    tid = jnp.arange(T, dtype=jnp.int32)
    te = jnp.minimum(jnp.sum((tid[:, None] >= tile_end[None, :]).astype(jnp.int32), axis=1), E - 1)
    te = jnp.where(tid < n_tiles[0], te, te[n_tiles[0] - 1]).astype(jnp.int32)
    src = jnp.zeros((T * bm,), jnp.int32).at[pos].set(jnp.arange(TOP_K * S, dtype=jnp.int32) // TOP_K)
    return pos, src, te, n_tiles


def _moe_gather_body(src_ref, h_hbm, o_ref, buf_ref, sem):
    base = pl.program_id(0) * MOE_BG

    def row_copy(j):
        return pltpu.make_async_copy(h_hbm.at[pl.ds(src_ref[base + j], 1), :], buf_ref.at[pl.ds(j, 1), :], sem)

    def start(j, c):
        row_copy(j).start()
        return c

    def wait(j, c):
        row_copy(j).wait()
        return c

    lax.fori_loop(0, MOE_BG, start, 0)
    lax.fori_loop(0, MOE_BG, wait, 0)
    o_ref[...] = buf_ref[...].astype(o_ref.dtype)


def moe_gather(h, src):
    S, D = h.shape
    R = src.shape[0]
    return pl.pallas_call(
        _moe_gather_body,
        grid_spec=pltpu.PrefetchScalarGridSpec(
            num_scalar_prefetch=1,
            grid=(R // MOE_BG,),
            in_specs=[pl.BlockSpec(memory_space=pl.ANY)],
            out_specs=pl.BlockSpec((MOE_BG, D), lambda i, src: (i, 0)),
            scratch_shapes=[pltpu.VMEM((MOE_BG, D), F32), pltpu.SemaphoreType.DMA(())],
        ),
        out_shape=jax.ShapeDtypeStruct((R, D), BF16),
        compiler_params=_cparams("arbitrary"),
        name="moe_gather",
    )(src, h)


def _moe_combine_body(pos_ref, gate_ref, ys_hbm, o_ref, buf_ref, sem):
    base = pl.program_id(0) * MOE_BG

    def row_copy(j, k):
        row = pos_ref[(base + j) * TOP_K + k]
        return pltpu.make_async_copy(ys_hbm.at[pl.ds(row, 1), :], buf_ref.at[k, pl.ds(j, 1), :], sem)

    def start(j, c):
        for k in range(TOP_K):
            row_copy(j, k).start()
        return c

    def wait(j, c):
        for k in range(TOP_K):
            row_copy(j, k).wait()
        return c

    lax.fori_loop(0, MOE_BG, start, 0)
    lax.fori_loop(0, MOE_BG, wait, 0)
    gate = gate_ref[...]
    acc = gate[:, 0:1] * buf_ref[0]
    for k in range(1, TOP_K):
        acc = acc + gate[:, k:k + 1] * buf_ref[k]
    o_ref[...] = acc


def moe_combine(ys, pos, gate):
    S = gate.shape[0]
    D = ys.shape[1]
    return pl.pallas_call(
        _moe_combine_body,
        grid_spec=pltpu.PrefetchScalarGridSpec(
            num_scalar_prefetch=1,
            grid=(S // MOE_BG,),
            in_specs=[pl.BlockSpec((MOE_BG, LANES), lambda i, pos: (i, 0)), pl.BlockSpec(memory_space=pl.ANY)],
            out_specs=pl.BlockSpec((MOE_BG, D), lambda i, pos: (i, 0)),
            scratch_shapes=[pltpu.VMEM((TOP_K, MOE_BG, D), F32), pltpu.SemaphoreType.DMA(())],
        ),
        out_shape=jax.ShapeDtypeStruct((S, D), F32),
        compiler_params=_cparams("arbitrary"),
        name="moe_combine",
    )(pos, gate, ys)


def moe_layer(h, router_w, w1, w3, w2, index):
    sel, gate = router(h, router_w)
    pos, src, tile_expert, n_tiles = _moe_plan(sel, MOE_BM)
    tile_expert = tile_expert + index * N_EXPERTS
    w1, w3, w2 = (w.reshape((-1,) + w.shape[2:]) for w in (w1, w3, w2))
    xs = moe_gather(h, src)
    g = swiglu_up(xs, w1, w3, tile_expert, n_tiles, bm=MOE_BM, bf=256)
    ys = grouped_matmul(g, w2, tile_expert, n_tiles, bm=MOE_BM, bn=512)
    return moe_combine(ys, pos, gate)


def _rot_cols(w):
    half = w.shape[-1] // 2
    return jnp.concatenate([-w[..., half:], w[..., :half]], axis=-1)


def _rope_tables(seq, dim):
    inv = ROPE_THETA ** (-jnp.arange(0, dim, 2, dtype=F32) / dim)
    ang = jnp.arange(seq, dtype=F32)[:, None] * inv[None, :]
    return jnp.cos(ang), jnp.sin(ang)


def _small_weight_t(wt_mid):
    row = lambda off, n: wt_mid[off - OFF_A_IN:off - OFF_A_IN + n]
    kr = row(OFF_KROPE, B_ROPE)
    parts = [row(OFF_CQ, B_Q_LORA), row(OFF_CKV, B_KV_LORA), kr, _rot_cols(kr.T).T,
             row(OFF_A_IN, A_HEADS), row(OFF_B_IN, A_HEADS)]
    w = jnp.concatenate(parts, axis=0)
    return jnp.pad(w, ((0, PS_W - w.shape[0]), (0, 0)))


def _mla_weights(w_uq, w_ukv):
    wq = w_uq.reshape(B_Q_LORA, B_HEADS, B_NOPE + B_ROPE)
    pe = wq[..., B_NOPE:]
    zeros = jnp.zeros_like(pe)
    wq_ext = jnp.concatenate([wq[..., :B_NOPE], pe, zeros, _rot_cols(pe), zeros], axis=-1)
    return wq_ext.reshape(B_Q_LORA, B_HEADS * B_QH).astype(BF16), w_ukv.astype(BF16)


def hybrid_mixer(h, layer, p, tables):
    cs_b, cos_c, sin_c = tables
    wt_in = jnp.swapaxes(p["w_in"], 1, 2)
    p_a = matmul_nt(h, wt_in, layer, OFF_A_IN, BF16, bm=1024, bn=512)
    p_s = matmul_nt(h, _small_weight_t(wt_in[layer, OFF_A_IN:OFF_REST])[None], 0, PS_W, F32, bm=1024, bn=256)
    p_r = matmul_nt_shifted(h, wt_in, layer, OFF_REST, REST_W, BF16, bm=1024, bn=512)
    o_a = gdn_branch(p_a, p_s, p["conv_w"][layer], p["gdn_a_log"][layer], p["gdn_dt_bias"][layer],
                     p["gdn_norm"][layer])
    wq_ext, wkv = _mla_weights(p["mla_w_uq"][layer], p["mla_w_ukv"][layer])
    q_cat, k_cat, v_b = mla_prep(p_s, p["mla_q_norm"][layer], p["mla_kv_norm"][layer], wq_ext, wkv, cs_b)
    o_b = mla_attention(q_cat, k_cat, v_b)
    o_c = dilated_branch(p_r, cos_c, sin_c)
    merged = merge_branches(o_a, o_b, o_c, p_r, p["w_branch_a"], p["w_branch_b"], p["w_branch_c"], layer)
    return matmul(merged, p["w_out"], layer, D_MODEL, F32, bm=1024, bn=512)


def kernel(x, attn_norm, w_in, conv_w, gdn_a_log, gdn_dt_bias, gdn_norm, mla_q_norm, mla_w_uq, mla_kv_norm, mla_w_ukv, w_branch_a, w_branch_b, w_branch_c, w_out, ffn_norm, dense_w1, dense_w3, dense_w2, router_w, moe_w1, moe_w3, moe_w2, final_norm):
    p = dict(w_in=w_in, conv_w=conv_w, gdn_a_log=gdn_a_log, gdn_dt_bias=gdn_dt_bias, gdn_norm=gdn_norm,
             mla_q_norm=mla_q_norm, mla_w_uq=mla_w_uq, mla_kv_norm=mla_kv_norm, mla_w_ukv=mla_w_ukv,
             w_branch_a=w_branch_a, w_branch_b=w_branch_b, w_branch_c=w_branch_c, w_out=w_out)
    Bn, S, D = x.shape
    assert Bn == 1 and D == D_MODEL
    cos_b, sin_b = _rope_tables(S, B_ROPE)
    cos_c, sin_c = _rope_tables(S, C_DH)
    tables = (jnp.concatenate([cos_b, cos_b, sin_b, sin_b], axis=1),
              jnp.concatenate([cos_c, cos_c], axis=1), jnp.concatenate([-sin_c, sin_c], axis=1))
    xr = x.reshape(S, D)
    h = rmsnorm(xr, attn_norm[0], BF16)
    out = None
    for layer in range(DEPTH):
        y = hybrid_mixer(h, layer, p, tables)
        i = layer // 2
        if layer % 2 == 0:
            xr, h = add_rmsnorm(xr, y, ffn_norm[layer], BF16)
            y = dense_ffn(h, dense_w1, dense_w3, dense_w2, i)
        else:
            xr, h = add_rmsnorm(xr, y, ffn_norm[layer], F32)
            y = moe_layer(h, router_w[i], moe_w1, moe_w3, moe_w2, i)
        if layer + 1 < DEPTH:
            xr, h = add_rmsnorm(xr, y, attn_norm[layer + 1], BF16)
        else:
            out = add_rmsnorm_final(xr, y, final_norm)
    return out.reshape(Bn, S, D)
```

```python
import functools
import math

import jax
import jax.numpy as jnp
from jax import lax
from jax.experimental import pallas as pl
from jax.experimental.pallas import tpu as pltpu

F32 = jnp.float32
BF16 = jnp.bfloat16

EPS = 1e-6
ROPE_THETA = 10000.0
NEG = -1e30

LANES = 128
VMEM_LIMIT_BYTES = 56 * 1024 * 1024

D_MODEL = 4096
DEPTH = 2
A_HEADS, A_DK, A_DV, A_CONV, A_CHUNK = 16, 128, 128, 4, 64
B_HEADS, B_Q_LORA, B_KV_LORA, B_NOPE, B_ROPE, B_VDIM = 16, 1024, 512, 128, 64, 128
C_HEADS, C_DH = 16, 128
C_PATTERNS = ((128, 1), (512, 4), (2048, 16))
D_FF = 11008
N_EXPERTS, TOP_K, D_EXPERT = 8, 2, 4096

A_QK_W = A_HEADS * A_DK
A_V_W = A_HEADS * A_DV
A_QKV_W = 2 * A_QK_W + A_V_W
C_W = C_HEADS * C_DH
OFF_A_IN = A_QKV_W + A_V_W
OFF_B_IN = OFF_A_IN + A_HEADS
OFF_CQ = OFF_B_IN + A_HEADS
OFF_CKV = OFF_CQ + B_Q_LORA
OFF_KROPE = OFF_CKV + B_KV_LORA
OFF_REST = OFF_KROPE + B_ROPE
REST_W = 3 * C_W + 3 * D_MODEL
PS_CQ, PS_CKV, PS_KR, PS_AB, PS_W = 0, 1024, 1536, 1664, 1792

B_QH = 384
B_KH = 256


def _cparams(*sem):
    return pltpu.CompilerParams(dimension_semantics=sem, vmem_limit_bytes=VMEM_LIMIT_BYTES)


def _dot(a, b):
    return jnp.dot(a, b, preferred_element_type=F32)


def _dot_nt(a, b):
    return lax.dot_general(a, b, (((1,), (1,)), ((), ())), preferred_element_type=F32)


def _sigmoid(x):
    return 1.0 / (1.0 + jnp.exp(-x))


def _silu(x):
    return x * _sigmoid(x)


def _softplus(x):
    return jnp.maximum(x, 0.0) + jnp.log(1.0 + jnp.exp(-jnp.abs(x)))


def _rmsnorm_body(x_ref, g_ref, o_ref):
    x = x_ref[...]
    y = x * lax.rsqrt(jnp.mean(x * x, axis=-1, keepdims=True) + EPS)
    o_ref[...] = (y * g_ref[...]).astype(o_ref.dtype)


def rmsnorm(x, g, out_dtype, bm=256):
    M, D = x.shape
    return pl.pallas_call(
        _rmsnorm_body,
        grid=(M // bm,),
        in_specs=[pl.BlockSpec((bm, D), lambda i: (i, 0)), pl.BlockSpec((1, D), lambda i: (0, 0))],
        out_specs=pl.BlockSpec((bm, D), lambda i: (i, 0)),
        out_shape=jax.ShapeDtypeStruct((M, D), out_dtype),
        compiler_params=_cparams("parallel"),
        name="rmsnorm",
    )(x, g.reshape(1, D))


def _add_rmsnorm_body(x_ref, y_ref, g_ref, xo_ref, ho_ref):
    x = x_ref[...] + y_ref[...].astype(F32)
    xo_ref[...] = x
    y = x * lax.rsqrt(jnp.mean(x * x, axis=-1, keepdims=True) + EPS)
    ho_ref[...] = (y * g_ref[...]).astype(ho_ref.dtype)


def add_rmsnorm(x, y, g, out_dtype, bm=256):
    M, D = x.shape
    row = pl.BlockSpec((bm, D), lambda i: (i, 0))
    return pl.pallas_call(
        _add_rmsnorm_body,
        grid=(M // bm,),
        in_specs=[row, row, pl.BlockSpec((1, D), lambda i: (0, 0))],
        out_specs=[row, row],
        out_shape=[jax.ShapeDtypeStruct((M, D), F32), jax.ShapeDtypeStruct((M, D), out_dtype)],
        compiler_params=_cparams("parallel"),
        name="add_rmsnorm",
    )(x, y, g.reshape(1, D))


def _add_rmsnorm_final_body(x_ref, y_ref, g_ref, o_ref):
    x = x_ref[...] + y_ref[...].astype(F32)
    y = x * lax.rsqrt(jnp.mean(x * x, axis=-1, keepdims=True) + EPS)
    o_ref[...] = y * g_ref[...]


def add_rmsnorm_final(x, y, g, bm=256):
    M, D = x.shape
    row = pl.BlockSpec((bm, D), lambda i: (i, 0))
    return pl.pallas_call(
        _add_rmsnorm_final_body,
        grid=(M // bm,),
        in_specs=[row, row, pl.BlockSpec((1, D), lambda i: (0, 0))],
        out_specs=row,
        out_shape=jax.ShapeDtypeStruct((M, D), F32),
        compiler_params=_cparams("parallel"),
        name="add_rmsnorm_final",
    )(x, y, g.reshape(1, D))


def _mm_acc_body(x_ref, w_ref, acc_ref, o_ref, wbf_ref):
    @pl.when(pl.program_id(1) == 0)
    def _():
        wbf_ref[...] = w_ref[...].astype(BF16)

    o_ref[...] = acc_ref[...] + _dot(x_ref[...], wbf_ref[...])


def matmul_acc(x, w, layer, acc, bm, bn):
    M, K = x.shape
    N = acc.shape[1]
    tile = pl.BlockSpec((bm, bn), lambda j, i: (i, j))
    return pl.pallas_call(
        _mm_acc_body,
        grid=(N // bn, M // bm),
        in_specs=[pl.BlockSpec((bm, K), lambda j, i: (i, 0)),
                  pl.BlockSpec((None, K, bn), lambda j, i: (layer, 0, j)), tile],
        out_specs=tile,
        out_shape=jax.ShapeDtypeStruct((M, N), F32),
        scratch_shapes=[pltpu.VMEM((K, bn), BF16)],
        compiler_params=_cparams("parallel", "arbitrary"),
        name="matmul_acc",
    )(x, w, acc)


def _mm_nt_body(x_ref, wt_ref, o_ref, wbf_ref):
    @pl.when(pl.program_id(1) == 0)
    def _():
        wbf_ref[...] = wt_ref[...].astype(BF16)

    o_ref[...] = _dot_nt(x_ref[...], wbf_ref[...]).astype(o_ref.dtype)


def matmul_nt(x, wt, layer, n_cols, out_dtype, bm, bn):
    M, K = x.shape
    return pl.pallas_call(
        _mm_nt_body,
        grid=(n_cols // bn, M // bm),
        in_specs=[pl.BlockSpec((bm, K), lambda j, i: (i, 0)),
                  pl.BlockSpec((None, bn, K), lambda j, i: (layer, j, 0))],
        out_specs=pl.BlockSpec((bm, bn), lambda j, i: (i, j)),
        out_shape=jax.ShapeDtypeStruct((M, n_cols), out_dtype),
        scratch_shapes=[pltpu.VMEM((bn, K), BF16)],
        compiler_params=_cparams("parallel", "arbitrary"),
        name="matmul_nt",
    )(x, wt)


MM_TAIL = 128


def _mm_nt_shifted_body(x_ref, wa_ref, wb_ref, o_ref, wbf_ref, *, shift):
    @pl.when(pl.program_id(1) == 0)
    def _():
        keep = wa_ref.shape[0] - shift
        wbf_ref[:keep, :] = wa_ref[shift:, :].astype(BF16)
        wbf_ref[keep:, :] = wb_ref[:shift, :].astype(BF16)

    o_ref[...] = _dot_nt(x_ref[...], wbf_ref[...]).astype(o_ref.dtype)


def matmul_nt_shifted(x, wt, layer, row0, n_cols, out_dtype, bm, bn):
    M, K = x.shape
    base, shift = divmod(row0, MM_TAIL)
    assert (base * MM_TAIL) % bn == 0 and 0 < shift and shift % 16 == 0
    ja = base * MM_TAIL // bn
    per = bn // MM_TAIL
    return pl.pallas_call(
        functools.partial(_mm_nt_shifted_body, shift=shift),
        grid=(n_cols // bn, M // bm),
        in_specs=[pl.BlockSpec((bm, K), lambda j, i: (i, 0)),
                  pl.BlockSpec((None, bn, K), lambda j, i: (layer, ja + j, 0)),
                  pl.BlockSpec((None, MM_TAIL, K), lambda j, i: (layer, base + per * (j + 1), 0))],
        out_specs=pl.BlockSpec((bm, bn), lambda j, i: (i, j)),
        out_shape=jax.ShapeDtypeStruct((M, n_cols), out_dtype),
        scratch_shapes=[pltpu.VMEM((bn, K), BF16)],
        compiler_params=_cparams("parallel", "arbitrary"),
        name="matmul_nt_shifted",
    )(x, wt, wt)


GDN_ROWS = 256
GDN_HB = 8
GDN_PREV = 16


def _gdn_gates_body(ab_ref, alog_ref, dtb_ref, o_ref):
    ab = ab_ref[...]
    lane = lax.broadcasted_iota(jnp.int32, ab.shape, 1)
    g = -jnp.exp(alog_ref[...]) * _softplus(ab + dtb_ref[...])
    o_ref[...] = jnp.where(lane < A_HEADS, g, _sigmoid(ab))


def gdn_gates(p_s, a_log, dt_bias, bm=512):
    S = p_s.shape[0]
    pad = lambda a: jnp.pad(a.astype(F32), (0, LANES - A_HEADS)).reshape(1, LANES)
    vec = pl.BlockSpec((1, LANES), lambda i: (0, 0))
    return pl.pallas_call(
        _gdn_gates_body,
        grid=(S // bm,),
        in_specs=[pl.BlockSpec((bm, LANES), lambda i: (i, PS_AB // LANES)), vec, vec],
        out_specs=pl.BlockSpec((bm, LANES), lambda i: (i, 0)),
        out_shape=jax.ShapeDtypeStruct((S, LANES), F32),
        compiler_params=_cparams("parallel"),
        name="gdn_gates",
    )(p_s, pad(a_log), pad(dt_bias))


def _gdn_body(q_ref, k_ref, v_ref, qp_ref, kp_ref, vp_ref, z_ref, gb_ref, cwq_ref, cwk_ref, cwv_ref,
              nw_ref, o_ref, state_ref):
    hb = pl.program_id(0)
    i = pl.program_id(1)
    R = GDN_ROWS
    C = A_CHUNK
    NC = R // C

    @pl.when(i == 0)
    def _():
        state_ref[...] = jnp.zeros_like(state_ref)

    has_prev = (i > 0).astype(F32)
    lane = lax.broadcasted_iota(jnp.int32, (R, LANES), 1)
    row = lax.broadcasted_iota(jnp.int32, (R, LANES), 0)
    rin = row & (C - 1)
    ri = lax.broadcasted_iota(jnp.int32, (R, R), 0)
    ci = lax.broadcasted_iota(jnp.int32, (R, R), 1)
    same = (ri // C) == (ci // C)
    incl = same & (ri >= ci)
    strict = same & (ri > ci)

    g_all = beta_all = gb_ref[...]
    nw = nw_ref[...]

    def conv_silu(cur_ref, prev_ref, cw_ref, sl):
        cur = cur_ref[:, sl].astype(F32)
        prev = prev_ref[:, sl].astype(F32) * has_prev
        full = jnp.concatenate([prev, cur], axis=0)
        cw = cw_ref[:, sl]
        y = cw[A_CONV - 1:A_CONV, :] * cur
        for s in range(1, A_CONV):
            y = y + cw[A_CONV - 1 - s:A_CONV - s, :] * pltpu.roll(full, s, axis=0)[GDN_PREV:, :]
        return _silu(y)

    heads = range(GDN_HB)
    lanes_of = [slice(hh * LANES, (hh + 1) * LANES) for hh in heads]
    st = []
    for hh in heads:
        h = hb * GDN_HB + hh
        sl = lanes_of[hh]
        g = jnp.broadcast_to(jnp.sum(jnp.where(lane == h, g_all, 0.0), axis=1, keepdims=True), (R, LANES))
        beta = jnp.broadcast_to(
            jnp.sum(jnp.where(lane == h + A_HEADS, beta_all, 0.0), axis=1, keepdims=True), (R, LANES))
        gc = g
        s = 1
        while s < C:
            gc = gc + jnp.where(rin >= s, pltpu.roll(gc, s, axis=0), 0.0)
            s *= 2
        g_last = jnp.broadcast_to(jnp.sum(g.reshape(NC, C, LANES), axis=1, keepdims=True),
                                  (NC, C, LANES)).reshape(R, LANES)

        q = conv_silu(q_ref, qp_ref, cwq_ref, sl)
        k = conv_silu(k_ref, kp_ref, cwk_ref, sl)
        v = conv_silu(v_ref, vp_ref, cwv_ref, sl)
        qn = q * lax.rsqrt(jnp.sum(q * q, axis=-1, keepdims=True) + EPS) * (A_DK ** -0.5)
        kn = k * lax.rsqrt(jnp.sum(k * k, axis=-1, keepdims=True) + EPS)
        kb = kn * beta
        eg = jnp.exp(gc)

        gc_t = gc.T
        diff = jnp.concatenate([gc, gc], axis=1) - jnp.concatenate([gc_t, gc_t], axis=0)
        decay = jnp.exp(jnp.where(incl, diff, NEG))
        kn_b = kn.astype(BF16)
        lower = jnp.where(strict, _dot_nt(kb.astype(BF16), kn_b) * decay, 0.0)
        st.append(dict(
            xp=-lower, qmat=-lower,
            qk=(_dot_nt(qn.astype(BF16), kn_b) * decay).astype(BF16),
            rhs=jnp.concatenate([v * beta, kb * eg], axis=1),
            q_dec=(qn * eg).astype(BF16),
            k_dec=kn * jnp.exp(g_last - gc),
            g_last=g_last,
        ))

    for _ in range(int(math.log2(C)) - 1):
        for t in st:
            xb = t["xp"].astype(BF16)
            t["xp"] = _dot(xb, xb)
            t["qmat"] = t["qmat"] + t["xp"] + _dot(t["qmat"].astype(BF16), t["xp"].astype(BF16))
    for t in st:
        sol = t["rhs"] + _dot(t["qmat"].astype(BF16), t["rhs"].astype(BF16))
        t["u"] = sol[:, :A_DV]
        t["w_b"] = sol[:, A_DV:].astype(BF16)
        t["outs"] = []

    states = [state_ref[hh] for hh in heads]
    for c in range(NC):
        rs = slice(c * C, (c + 1) * C)
        for hh, t in enumerate(st):
            sb = states[hh].astype(BF16)
            v_new_b = (t["u"][rs] - _dot(t["w_b"][rs], sb)).astype(BF16)
            t["outs"].append(_dot(t["q_dec"][rs], sb) + _dot(t["qk"][rs, :][:, rs], v_new_b))
            states[hh] = (states[hh] * jnp.exp(t["g_last"][c * C:c * C + 1, :])
                          + _dot(t["k_dec"][rs].T.astype(BF16), v_new_b))
    for hh, t in enumerate(st):
        sl = lanes_of[hh]
        state_ref[hh] = states[hh]
        out = jnp.concatenate(t["outs"], axis=0)
        out = out * lax.rsqrt(jnp.mean(out * out, axis=-1, keepdims=True) + EPS) * nw
        o_ref[:, sl] = (out * _silu(z_ref[:, sl].astype(F32))).astype(o_ref.dtype)


def gdn_branch(p_a, p_s, conv_w, a_log, dt_bias, norm_w):
    S = p_a.shape[0]
    R, W = GDN_ROWS, GDN_HB * LANES
    nq = A_QK_W // W
    gb = gdn_gates(p_s, a_log, dt_bias)

    def cur(off):
        return pl.BlockSpec((R, W), lambda hb, i: (i, off + hb))

    def prev(off):
        return pl.BlockSpec((GDN_PREV, W), lambda hb, i: (jnp.maximum(i * (R // GDN_PREV) - 1, 0), off + hb))

    def cw(off):
        return pl.BlockSpec((A_CONV, W), lambda hb, i: (0, off + hb))

    vec = pl.BlockSpec((1, LANES), lambda hb, i: (0, 0))
    return pl.pallas_call(
        _gdn_body,
        grid=(A_HEADS // GDN_HB, S // R),
        in_specs=[cur(0), cur(nq), cur(2 * nq), prev(0), prev(nq), prev(2 * nq), cur(3 * nq),
                  pl.BlockSpec((R, LANES), lambda hb, i: (i, 0)),
                  cw(0), cw(nq), cw(2 * nq), vec],
        out_specs=pl.BlockSpec((R, W), lambda hb, i: (i, hb)),
        out_shape=jax.ShapeDtypeStruct((S, A_V_W), BF16),
        scratch_shapes=[pltpu.VMEM((GDN_HB, A_DK, A_DV), F32)],
        compiler_params=_cparams("parallel", "arbitrary"),
        name="gdn",
    )(p_a, p_a, p_a, p_a, p_a, p_a, p_a, gb, conv_w, conv_w, conv_w, norm_w.astype(F32).reshape(1, LANES))


MLA_HG = 4


def _mla_prep_body(cq_ref, ckv_ref, kr_ref, qn_ref, kvn_ref, wq_ref, wkv_ref, cs_ref, qo_ref, ko_ref, vo_ref):
    scale = (B_NOPE + B_ROPE) ** -0.5 * math.log2(math.e)
    cq = cq_ref[...]
    cq = (cq * lax.rsqrt(jnp.mean(cq * cq, axis=-1, keepdims=True) + EPS) * qn_ref[...]).astype(BF16)
    ckv = ckv_ref[...]
    ckv = (ckv * lax.rsqrt(jnp.mean(ckv * ckv, axis=-1, keepdims=True) + EPS) * kvn_ref[...]).astype(BF16)
    rq = _dot(cq, wq_ref[...])
    rkv = _dot(ckv, wkv_ref[...])
    cs = cs_ref[...]
    sc = pltpu.roll(cs, B_ROPE, axis=1)
    prod = kr_ref[...] * cs
    lane = lax.broadcasted_iota(jnp.int32, prod.shape, 1)
    k_pe = jnp.where(lane < B_ROPE, prod + pltpu.roll(prod, B_ROPE, axis=1), 0.0).astype(BF16)
    for hh in range(MLA_HG):
        qb = hh * B_QH
        roped = rq[:, qb + LANES:qb + 2 * LANES] * cs + rq[:, qb + 2 * LANES:qb + 3 * LANES] * sc
        qo_ref[:, hh * B_KH:hh * B_KH + LANES] = (rq[:, qb:qb + LANES] * scale).astype(BF16)
        qo_ref[:, hh * B_KH + LANES:(hh + 1) * B_KH] = (roped * scale).astype(BF16)
        ko_ref[:, hh * B_KH:hh * B_KH + LANES] = rkv[:, hh * 2 * LANES:hh * 2 * LANES + LANES].astype(BF16)
        ko_ref[:, hh * B_KH + LANES:(hh + 1) * B_KH] = k_pe
        vo_ref[:, hh * 2 * LANES:hh * 2 * LANES + LANES] = (
            rkv[:, hh * 2 * LANES + LANES:(hh + 1) * 2 * LANES].astype(BF16))
        vo_ref[:, hh * 2 * LANES + LANES:(hh + 1) * 2 * LANES] = jnp.ones((rkv.shape[0], LANES), BF16)


def mla_prep(p_s, q_norm, kv_norm, wq_ext, wkv, cs_b, bm=512):
    S = p_s.shape[0]
    HG = MLA_HG
    return pl.pallas_call(
        _mla_prep_body,
        grid=(B_HEADS // HG, S // bm),
        in_specs=[
            pl.BlockSpec((bm, B_Q_LORA), lambda g, i: (i, PS_CQ // B_Q_LORA)),
            pl.BlockSpec((bm, B_KV_LORA), lambda g, i: (i, PS_CKV // B_KV_LORA)),
            pl.BlockSpec((bm, LANES), lambda g, i: (i, PS_KR // LANES)),
            pl.BlockSpec((1, B_Q_LORA), lambda g, i: (0, 0)),
            pl.BlockSpec((1, B_KV_LORA), lambda g, i: (0, 0)),
            pl.BlockSpec((B_Q_LORA, HG * B_QH), lambda g, i: (0, g)),
            pl.BlockSpec((B_KV_LORA, HG * 2 * LANES), lambda g, i: (0, g)),
            pl.BlockSpec((bm, LANES), lambda g, i: (i, 0)),
        ],
        out_specs=[
            pl.BlockSpec((bm, HG * B_KH), lambda g, i: (i, g)),
            pl.BlockSpec((bm, HG * B_KH), lambda g, i: (i, g)),
            pl.BlockSpec((bm, HG * 2 * LANES), lambda g, i: (i, g)),
        ],
        out_shape=[
            jax.ShapeDtypeStruct((S, B_HEADS * B_KH), BF16),
            jax.ShapeDtypeStruct((S, B_HEADS * B_KH), BF16),
            jax.ShapeDtypeStruct((S, B_HEADS * 2 * B_VDIM), BF16),
        ],
        compiler_params=_cparams("parallel", "parallel"),
        name="mla_prep",
    )(p_s, p_s, p_s, q_norm.reshape(1, -1), kv_norm.reshape(1, -1), wq_ext, wkv, cs_b)


FLASH_BLK = 1024
FLASH_RC = 256


def _flash_body(qi_ref, kj_ref, q_ref, k_ref, v_ref, o_ref, m_ref, acc_ref):
    t = pl.program_id(1)
    qi = qi_ref[t]
    kj = kj_ref[t]

    @pl.when(kj == 0)
    def _():
        m_ref[...] = jnp.full_like(m_ref, NEG)
        acc_ref[...] = jnp.zeros_like(acc_ref)

    def step(diag):
        nc = FLASH_BLK // FLASH_RC
        rows_of = [slice(c * FLASH_RC, (c + 1) * FLASH_RC) for c in range(nc)]
        ncol_of = [(c + 1) * FLASH_RC if diag else FLASH_BLK for c in range(nc)]

        def scores(c):
            s = _dot_nt(q_ref[rows_of[c], :], k_ref[:ncol_of[c], :])
            if diag:
                r = lax.broadcasted_iota(jnp.int32, s.shape, 0) + c * FLASH_RC
                col = lax.broadcasted_iota(jnp.int32, s.shape, 1)
                s = jnp.where(col <= r, s, NEG)
            return s

        s_next = scores(0)
        for c in range(nc):
            s = s_next
            if c + 1 < nc:
                s_next = scores(c + 1)
            rows = rows_of[c]
            m_prev = m_ref[rows, :]
            m_new = jnp.maximum(m_prev, jnp.max(s, axis=1, keepdims=True))
            p = jnp.exp2(s - m_new).astype(BF16)
            acc_ref[rows, :] = jnp.exp2(m_prev - m_new) * acc_ref[rows, :] + _dot(p, v_ref[:ncol_of[c], :])
            m_ref[rows, :] = m_new

    @pl.when(kj < qi)
    def _():
        step(False)

    @pl.when(kj == qi)
    def _():
        step(True)
        acc = acc_ref[...]
        o_ref[...] = (acc[:, :B_VDIM] / acc[:, B_VDIM:]).astype(o_ref.dtype)


def mla_attention(q_cat, k_cat, v_ext):
    S = q_cat.shape[0]
    blk = FLASH_BLK
    n = S // blk
    pairs = [(qi, kj) for qi in range(n) for kj in range(qi + 1)]
    qi_tab = jnp.asarray([p[0] for p in pairs], jnp.int32)
    kj_tab = jnp.asarray([p[1] for p in pairs], jnp.int32)
    return pl.pallas_call(
        _flash_body,
        grid_spec=pltpu.PrefetchScalarGridSpec(
            num_scalar_prefetch=2,
            grid=(B_HEADS, len(pairs)),
            in_specs=[
                pl.BlockSpec((blk, B_KH), lambda h, t, qi, kj: (qi[t], h)),
                pl.BlockSpec((blk, B_KH), lambda h, t, qi, kj: (kj[t], h)),
                pl.BlockSpec((blk, 2 * B_VDIM), lambda h, t, qi, kj: (kj[t], h)),
            ],
            out_specs=pl.BlockSpec((blk, B_VDIM), lambda h, t, qi, kj: (qi[t], h)),
            scratch_shapes=[pltpu.VMEM((blk, 1), F32), pltpu.VMEM((blk, 2 * B_VDIM), F32)],
        ),
        out_shape=jax.ShapeDtypeStruct((S, B_HEADS * B_VDIM), BF16),
        compiler_params=_cparams("parallel", "arbitrary"),
        name="mla_flash",
    )(qi_tab, kj_tab, q_cat, k_cat, v_ext)


DIL_BLK = 128
DIL_ROWS = 256
DILATIONS = tuple(d for _, d in C_PATTERNS)


def _rope_split_body(q_ref, k_ref, v_ref, c_ref, s_ref, *refs):
    nd = len(DILATIONS)
    outs, (sq, sk, sv) = refs[:3 * nd], refs[3 * nd:]
    c = c_ref[...]
    s = s_ref[...]
    scale = C_DH ** -0.5
    for h in range(C_HEADS):
        sl = slice(h * C_DH, (h + 1) * C_DH)
        x = q_ref[:, sl].astype(F32)
        sq[h] = (x * c + pltpu.roll(x, C_DH // 2, axis=1) * s) * scale
        x = k_ref[:, sl].astype(F32)
        sk[h] = x * c + pltpu.roll(x, C_DH // 2, axis=1) * s
        sv[h] = v_ref[:, sl].astype(F32)
        for di, d in enumerate(DILATIONS):
            for src, dst in zip((sq, sk, sv), outs[3 * di:3 * di + 3]):
                for r in range(d):
                    rows = src[h] if d == 1 else src[h, pl.ds(r, DIL_ROWS // d, stride=d), :]
                    dst[r, :, sl] = rows.astype(BF16)


def rope_split(p_r, cos_c, sin_c):
    S = p_r.shape[0]
    bm = DIL_ROWS
    tab = pl.BlockSpec((bm, C_DH), lambda i: (i, 0))
    col = lambda j: pl.BlockSpec((bm, C_W), lambda i: (i, j))
    out_specs, out_shape = [], []
    for d in DILATIONS:
        out_specs += [pl.BlockSpec((d, bm // d, C_W), lambda i: (0, i, 0))] * 3
        out_shape += [jax.ShapeDtypeStruct((d, S // d, C_W), BF16)] * 3
    outs = pl.pallas_call(
        _rope_split_body,
        grid=(S // bm,),
        in_specs=[col(0), col(1), col(2), tab, tab],
        out_specs=out_specs,
        out_shape=out_shape,
        scratch_shapes=[pltpu.VMEM((C_HEADS, bm, C_DH), F32)] * 3,
        compiler_params=_cparams("parallel"),
        name="rope_split",
    )(p_r, p_r, p_r, cos_c, sin_c)
    return [outs[3 * i:3 * i + 3] for i in range(len(DILATIONS))]


def _dilated_body(q_ref, kc_ref, kp_ref, vc_ref, vp_ref, o_ref, lse_ref):
    B = DIL_BLK
    n = pl.program_id(1)
    qi = lax.broadcasted_iota(jnp.int32, (B, 2 * B), 0)
    kj = lax.broadcasted_iota(jnp.int32, (B, 2 * B), 1)
    dist = qi + B - kj
    valid = (dist >= 0) & (dist <= B) & ((kj >= B) | (n > 0))
    lanes_of = [slice(h * C_DH, (h + 1) * C_DH) for h in range(C_HEADS)]
    scores = [jnp.where(valid, _dot_nt(q_ref[:, sl], jnp.concatenate([kp_ref[:, sl], kc_ref[:, sl]], axis=0)), NEG)
              for sl in lanes_of]
    probs = []
    lane = lax.broadcasted_iota(jnp.int32, (B, LANES), 1)
    lse = jnp.zeros((B, LANES), F32)
    for h, s in enumerate(scores):
        m = jnp.max(s, axis=1, keepdims=True)
        p = jnp.exp(s - m)
        l = jnp.sum(p, axis=1, keepdims=True)
        lse = jnp.where(lane == h, m + jnp.log(l), lse)
        probs.append((p.astype(BF16), l))
    lse_ref[...] = lse
    for sl, (p, l) in zip(lanes_of, probs):
        vv = jnp.concatenate([vp_ref[:, sl], vc_ref[:, sl]], axis=0)
        o_ref[:, sl] = (_dot(p, vv) / l).astype(o_ref.dtype)


def dilated_pattern(q, k, v):
    d, L, W = q.shape
    B = DIL_BLK
    cur = pl.BlockSpec((None, B, W), lambda r, n: (r, n, 0))
    prv = pl.BlockSpec((None, B, W), lambda r, n: (r, jnp.maximum(n - 1, 0), 0))
    return pl.pallas_call(
        _dilated_body,
        grid=(d, L // B),
        in_specs=[cur, cur, prv, cur, prv],
        out_specs=[cur, pl.BlockSpec((None, B, LANES), lambda r, n: (r, n, 0))],
        out_shape=[jax.ShapeDtypeStruct((d, L, W), BF16), jax.ShapeDtypeStruct((d, L, LANES), F32)],
        compiler_params=_cparams("parallel", "parallel"),
        name=f"dilated_d{d}",
    )(q, k, k, v, v)


def _dilated_merge_body(*refs):
    nd = len(DILATIONS)
    o_refs, lse_refs, out_ref = refs[:nd], refs[nd:2 * nd], refs[2 * nd]
    so, sl = refs[2 * nd + 1:]
    rows_of = lambda d, r: slice(None) if d == 1 else pl.ds(r, DIL_ROWS // d, stride=d)
    for di, d in enumerate(DILATIONS):
        for r in range(d):
            sl[di, rows_of(d, r), :] = lse_refs[di][r]
    top = sl[0]
    for di in range(1, nd):
        top = jnp.maximum(top, sl[di])
    weights = [jnp.exp(sl[di] - top) for di in range(nd)]
    lane = lax.broadcasted_iota(jnp.int32, top.shape, 1)
    for h in range(C_HEADS):
        cols = slice(h * C_DH, (h + 1) * C_DH)
        for di, d in enumerate(DILATIONS):
            for r in range(d):
                so[di, rows_of(d, r), :] = o_refs[di][r, :, cols].astype(F32)
        num = 0.0
        den = 0.0
        for di in range(nd):
            w = jnp.sum(jnp.where(lane == h, weights[di], 0.0), axis=1, keepdims=True)
            num = num + w * so[di]
            den = den + w
        out_ref[:, cols] = (num / den).astype(out_ref.dtype)


def dilated_merge(outs, lses):
    S = outs[0].shape[0] * outs[0].shape[1]
    bm = DIL_ROWS
    nd = len(DILATIONS)
    specs = [pl.BlockSpec((d, bm // d, C_W), lambda i: (0, i, 0)) for d in DILATIONS]
    lse_specs = [pl.BlockSpec((d, bm // d, LANES), lambda i: (0, i, 0)) for d in DILATIONS]
    return pl.pallas_call(
        _dilated_merge_body,
        grid=(S // bm,),
        in_specs=specs + lse_specs,
        out_specs=pl.BlockSpec((bm, C_W), lambda i: (i, 0)),
        out_shape=jax.ShapeDtypeStruct((S, C_W), BF16),
        scratch_shapes=[pltpu.VMEM((nd, bm, C_DH), F32)] * 2,
        compiler_params=_cparams("parallel"),
        name="dilated_merge",
    )(*outs, *lses)


def dilated_branch(p_r, cos_c, sin_c):
    assert all(w // d == DIL_BLK for w, d in C_PATTERNS)
    results = [dilated_pattern(q, k, v) for q, k, v in rope_split(p_r, cos_c, sin_c)]
    return dilated_merge([r[0] for r in results], [r[1] for r in results])


def _merge_body(oa_ref, ob_ref, oc_ref, ga_ref, gb_ref, gc_ref, wa_ref, wb_ref, wc_ref, o_ref, wa_s, wb_s, wc_s):
    @pl.when(pl.program_id(1) == 0)
    def _():
        wa_s[...] = wa_ref[...].astype(BF16)
        wb_s[...] = wb_ref[...].astype(BF16)
        wc_s[...] = wc_ref[...].astype(BF16)

    acc = _sigmoid(ga_ref[...].astype(F32)) * _dot(oa_ref[...], wa_s[...])
    acc = acc + _sigmoid(gb_ref[...].astype(F32)) * _dot(ob_ref[...], wb_s[...])
    acc = acc + _sigmoid(gc_ref[...].astype(F32)) * _dot(oc_ref[...], wc_s[...])
    o_ref[...] = acc.astype(o_ref.dtype)


def merge_branches(o_a, o_b, o_c, p_r, w_a, w_b, w_c, layer, bm=512, bn=512):
    S, K = o_a.shape
    g0 = 3 * C_W // bn
    gstep = D_MODEL // bn
    act = pl.BlockSpec((bm, K), lambda j, i: (i, 0))
    wsp = pl.BlockSpec((None, K, bn), lambda j, i: (layer, 0, j))
    gate = lambda b: pl.BlockSpec((bm, bn), lambda j, i: (i, g0 + b * gstep + j))
    return pl.pallas_call(
        _merge_body,
        grid=(D_MODEL // bn, S // bm),
        in_specs=[act, act, act, gate(0), gate(1), gate(2), wsp, wsp, wsp],
        out_specs=pl.BlockSpec((bm, bn), lambda j, i: (i, j)),
        out_shape=jax.ShapeDtypeStruct((S, D_MODEL), BF16),
        scratch_shapes=[pltpu.VMEM((K, bn), BF16)] * 3,
        compiler_params=_cparams("parallel", "arbitrary"),
        name="merge_branches",
    )(o_a, o_b, o_c, p_r, p_r, p_r, w_a, w_b, w_c)


def _new_group(tg_ref):
    i = pl.program_id(1)
    return (i == 0) | (tg_ref[i] != tg_ref[jnp.maximum(i - 1, 0)])


def _swiglu_up_body(tg_ref, nt_ref, x_ref, w1_ref, w3_ref, o_ref, w1_s, w3_s):
    @pl.when(_new_group(tg_ref))
    def _():
        w1_s[...] = w1_ref[...].astype(BF16)
        w3_s[...] = w3_ref[...].astype(BF16)

    @pl.when(pl.program_id(1) < nt_ref[0])
    def _():
        x = x_ref[...]
        o_ref[...] = (_silu(_dot(x, w1_s[...])) * _dot(x, w3_s[...])).astype(o_ref.dtype)

    @pl.when(pl.program_id(1) >= nt_ref[0])
    def _():
        o_ref[...] = jnp.zeros_like(o_ref)


def swiglu_up(x, w1, w3, tile_group, n_tiles, bm, bf):
    R, D = x.shape
    F = w1.shape[2]
    wspec = pl.BlockSpec((None, D, bf), lambda f, i, tg, nt: (tg[i], 0, f))
    return pl.pallas_call(
        _swiglu_up_body,
        grid_spec=pltpu.PrefetchScalarGridSpec(
            num_scalar_prefetch=2,
            grid=(F // bf, R // bm),
            in_specs=[pl.BlockSpec((bm, D), lambda f, i, tg, nt: (jnp.minimum(i, nt[0] - 1), 0)), wspec, wspec],
            out_specs=pl.BlockSpec((bm, bf), lambda f, i, tg, nt: (i, f)),
            scratch_shapes=[pltpu.VMEM((D, bf), BF16)] * 2,
        ),
        out_shape=jax.ShapeDtypeStruct((R, F), BF16),
        compiler_params=_cparams("arbitrary", "arbitrary"),
        name="swiglu_up",
    )(tile_group, n_tiles, x, w1, w3)


def _grouped_mm_body(tg_ref, nt_ref, x_ref, w_ref, *rest, has_acc):
    if has_acc:
        acc_ref, o_ref, w_s = rest
    else:
        o_ref, w_s = rest

    @pl.when(_new_group(tg_ref))
    def _():
        w_s[...] = w_ref[...].astype(BF16)

    @pl.when(pl.program_id(1) < nt_ref[0])
    def _():
        y = _dot(x_ref[...], w_s[...])
        o_ref[...] = acc_ref[...] + y if has_acc else y

    @pl.when(pl.program_id(1) >= nt_ref[0])
    def _():
        o_ref[...] = jnp.zeros_like(o_ref)


def grouped_matmul(x, w, tile_group, n_tiles, bm, bn, k_blocks=1, k_index=0, acc=None):
    R, K = x.shape
    N = w.shape[2]
    kb = K // k_blocks
    in_specs = [pl.BlockSpec((bm, kb), lambda j, i, tg, nt: (jnp.minimum(i, nt[0] - 1), k_index)),
                pl.BlockSpec((None, kb, bn), lambda j, i, tg, nt: (tg[i], k_index, j))]
    out_spec = pl.BlockSpec((bm, bn), lambda j, i, tg, nt: (i, j))
    args = [tile_group, n_tiles, x, w]
    if acc is not None:
        in_specs.append(out_spec)
        args.append(acc)
    return pl.pallas_call(
        functools.partial(_grouped_mm_body, has_acc=acc is not None),
        grid_spec=pltpu.PrefetchScalarGridSpec(
            num_scalar_prefetch=2,
            grid=(N // bn, R // bm),
            in_specs=in_specs,
            out_specs=out_spec,
            scratch_shapes=[pltpu.VMEM((kb, bn), BF16)],
        ),
        out_shape=jax.ShapeDtypeStruct((R, N), F32),
        input_output_aliases={4: 0} if acc is not None else {},
        compiler_params=_cparams("arbitrary", "arbitrary"),
        name="grouped_matmul",
    )(*args)


DENSE_UP_BM = 1024
DENSE_DOWN_BM = 512
DENSE_KSPLIT = 2


def _one_group(n_rows, bm, group):
    return jnp.full((n_rows // bm,), group, jnp.int32), jnp.full((1,), n_rows // bm, jnp.int32)


def dense_ffn(h, w1, w3, w2, index, x_res):
    S = h.shape[0]
    g = swiglu_up(h, w1, w3, *_one_group(S, DENSE_UP_BM, index), bm=DENSE_UP_BM, bf=256)
    y = x_res
    for k in range(DENSE_KSPLIT):
        y = grouped_matmul(g, w2, *_one_group(S, DENSE_DOWN_BM, index), bm=DENSE_DOWN_BM, bn=512,
                           k_blocks=DENSE_KSPLIT, k_index=k, acc=y)
    return y


def _router_body(h_ref, rw_ref, sel_ref, gate_ref):
    logits = _dot(h_ref[...].astype(BF16), rw_ref[...].astype(BF16))
    lane = lax.broadcasted_iota(jnp.int32, logits.shape, 1)
    logits = jnp.where(lane < N_EXPERTS, logits, NEG)
    t1 = jnp.max(logits, axis=1, keepdims=True)
    i1 = jnp.min(jnp.where(logits == t1, lane, LANES), axis=1, keepdims=True)
    rest = jnp.where(lane == i1, NEG, logits)
    t2 = jnp.max(rest, axis=1, keepdims=True)
    i2 = jnp.min(jnp.where(rest == t2, lane, LANES), axis=1, keepdims=True)
    e2 = jnp.exp(t2 - t1)
    g1 = 1.0 / (1.0 + e2)
    sel_ref[...] = jnp.where(lane == 0, i1, jnp.where(lane == 1, i2, 0))
    gate_ref[...] = jnp.where(lane == 0, g1, jnp.where(lane == 1, e2 * g1, 0.0))


def router(h, router_w, bm=512):
    S, D = h.shape
    rw = jnp.pad(router_w, ((0, 0), (0, LANES - N_EXPERTS)))
    out = pl.BlockSpec((bm, LANES), lambda i: (i, 0))
    return pl.pallas_call(
        _router_body,
        grid=(S // bm,),
        in_specs=[pl.BlockSpec((bm, D), lambda i: (i, 0)), pl.BlockSpec((D, LANES), lambda i: (0, 0))],
        out_specs=[out, out],
        out_shape=[jax.ShapeDtypeStruct((S, LANES), jnp.int32), jax.ShapeDtypeStruct((S, LANES), F32)],
        compiler_params=_cparams("parallel"),
        name="router",
    )(h, rw)


MOE_BM = 512
MOE_BG = 256
DMA_QUEUES = 2


def _moe_plan(sel, bm):
    S = sel.shape[0]
    E = N_EXPERTS
    e = sel[:, :TOP_K].reshape(-1)
    onehot = (e[:, None] == jnp.arange(E, dtype=jnp.int32)[None, :]).astype(jnp.int32)
    csum = jnp.cumsum(onehot, axis=0)
    rank = jnp.sum((csum - 1) * onehot, axis=1)
    tiles_per = (csum[-1] + bm - 1) // bm
    tile_end = jnp.cumsum(tiles_per)
    pos = ((tile_end - tiles_per)[e] * bm + rank).astype(jnp.int32)
    n_tiles = tile_end[-1:].astype(jnp.int32)
    T = (TOP_K * S) // bm + E
    tid = jnp.arange(T, dtype=jnp.int32)
    te = jnp.minimum(jnp.sum((tid[:, None] >= tile_end[None, :]).astype(jnp.int32), axis=1), E - 1)
    te = jnp.where(tid < n_tiles[0], te, te[n_tiles[0] - 1]).astype(jnp.int32)
    src = jnp.zeros((T * bm,), jnp.int32).at[pos].set(jnp.arange(TOP_K * S, dtype=jnp.int32) // TOP_K)
    return pos, src, te, n_tiles


def _moe_gather_body(src_ref, h_hbm, o_ref, buf_ref, sem):
    base = pl.program_id(0) * MOE_BG

    def row_copy(j):
        return pltpu.make_async_copy(h_hbm.at[pl.ds(src_ref[base + j], 1), :], buf_ref.at[pl.ds(j, 1), :], sem)

    def start(jj, c):
        for par in range(DMA_QUEUES):
            row_copy(jj * DMA_QUEUES + par).start(priority=par)
        return c

    def wait(j, c):
        row_copy(j).wait()
        return c

    lax.fori_loop(0, MOE_BG // DMA_QUEUES, start, 0)
    lax.fori_loop(0, MOE_BG, wait, 0)
    o_ref[...] = buf_ref[...].astype(o_ref.dtype)


def moe_gather(h, src):
    S, D = h.shape
    R = src.shape[0]
    return pl.pallas_call(
        _moe_gather_body,
        grid_spec=pltpu.PrefetchScalarGridSpec(
            num_scalar_prefetch=1,
            grid=(R // MOE_BG,),
            in_specs=[pl.BlockSpec(memory_space=pl.ANY)],
            out_specs=pl.BlockSpec((MOE_BG, D), lambda i, src: (i, 0)),
            scratch_shapes=[pltpu.VMEM((MOE_BG, D), F32), pltpu.SemaphoreType.DMA(())],
        ),
        out_shape=jax.ShapeDtypeStruct((R, D), BF16),
        compiler_params=_cparams("arbitrary"),
        name="moe_gather",
    )(src, h)


def _moe_combine_body(pos_ref, gate_ref, ys_hbm, o_ref, buf_ref, sem):
    base = pl.program_id(0) * MOE_BG

    def row_copy(j, k):
        row = pos_ref[(base + j) * TOP_K + k]
        return pltpu.make_async_copy(ys_hbm.at[pl.ds(row, 1), :], buf_ref.at[k, pl.ds(j, 1), :], sem)

    def start(j, c):
        for k in range(TOP_K):
            row_copy(j, k).start(priority=k % DMA_QUEUES)
        return c

    def wait(j, c):
        for k in range(TOP_K):
            row_copy(j, k).wait()
        return c

    lax.fori_loop(0, MOE_BG, start, 0)
    lax.fori_loop(0, MOE_BG, wait, 0)
    gate = gate_ref[...]
    acc = gate[:, 0:1] * buf_ref[0]
    for k in range(1, TOP_K):
        acc = acc + gate[:, k:k + 1] * buf_ref[k]
    o_ref[...] = acc


def moe_combine(ys, pos, gate):
    S = gate.shape[0]
    D = ys.shape[1]
    return pl.pallas_call(
        _moe_combine_body,
        grid_spec=pltpu.PrefetchScalarGridSpec(
            num_scalar_prefetch=1,
            grid=(S // MOE_BG,),
            in_specs=[pl.BlockSpec((MOE_BG, LANES), lambda i, pos: (i, 0)), pl.BlockSpec(memory_space=pl.ANY)],
            out_specs=pl.BlockSpec((MOE_BG, D), lambda i, pos: (i, 0)),
            scratch_shapes=[pltpu.VMEM((TOP_K, MOE_BG, D), F32), pltpu.SemaphoreType.DMA(())],
        ),
        out_shape=jax.ShapeDtypeStruct((S, D), F32),
        compiler_params=_cparams("arbitrary"),
        name="moe_combine",
    )(pos, gate, ys)


def moe_layer(h, router_w, w1, w3, w2, index):
    sel, gate = router(h, router_w)
    pos, src, tile_expert, n_tiles = _moe_plan(sel, MOE_BM)
    tile_expert = tile_expert + index * N_EXPERTS
    w1, w3, w2 = (w.reshape((-1,) + w.shape[2:]) for w in (w1, w3, w2))
    xs = moe_gather(h, src)
    g = swiglu_up(xs, w1, w3, tile_expert, n_tiles, bm=MOE_BM, bf=512)
    ys = grouped_matmul(g, w2, tile_expert, n_tiles, bm=MOE_BM, bn=512)
    return moe_combine(ys, pos, gate)


def _rot_cols(w):
    half = w.shape[-1] // 2
    return jnp.concatenate([-w[..., half:], w[..., :half]], axis=-1)


def _rope_tables(seq, dim):
    inv = ROPE_THETA ** (-jnp.arange(0, dim, 2, dtype=F32) / dim)
    ang = jnp.arange(seq, dtype=F32)[:, None] * inv[None, :]
    return jnp.cos(ang), jnp.sin(ang)


def _small_weight_t(wt_mid):
    row = lambda off, n: wt_mid[off - OFF_A_IN:off - OFF_A_IN + n]
    kr = row(OFF_KROPE, B_ROPE)
    parts = [row(OFF_CQ, B_Q_LORA), row(OFF_CKV, B_KV_LORA), kr, _rot_cols(kr.T).T,
             row(OFF_A_IN, A_HEADS), row(OFF_B_IN, A_HEADS)]
    w = jnp.concatenate(parts, axis=0)
    return jnp.pad(w, ((0, PS_W - w.shape[0]), (0, 0)))


def _mla_weights(w_uq, w_ukv):
    wq = w_uq.reshape(B_Q_LORA, B_HEADS, B_NOPE + B_ROPE)
    pe = wq[..., B_NOPE:]
    zeros = jnp.zeros_like(pe)
    wq_ext = jnp.concatenate([wq[..., :B_NOPE], pe, zeros, _rot_cols(pe), zeros], axis=-1)
    return wq_ext.reshape(B_Q_LORA, B_HEADS * B_QH).astype(BF16), w_ukv.astype(BF16)


def hybrid_mixer(h, x_res, layer, p, tables):
    cs_b, cos_c, sin_c = tables
    wt_in = jnp.swapaxes(p["w_in"], 1, 2)
    p_a = matmul_nt(h, wt_in, layer, OFF_A_IN, BF16, bm=1024, bn=512)
    p_s = matmul_nt(h, _small_weight_t(wt_in[layer, OFF_A_IN:OFF_REST])[None], 0, PS_W, F32, bm=1024, bn=256)
    p_r = matmul_nt_shifted(h, wt_in, layer, OFF_REST, REST_W, BF16, bm=1024, bn=512)
    o_a = gdn_branch(p_a, p_s, p["conv_w"][layer], p["gdn_a_log"][layer], p["gdn_dt_bias"][layer],
                     p["gdn_norm"][layer])
    wq_ext, wkv = _mla_weights(p["mla_w_uq"][layer], p["mla_w_ukv"][layer])
    q_cat, k_cat, v_b = mla_prep(p_s, p["mla_q_norm"][layer], p["mla_kv_norm"][layer], wq_ext, wkv, cs_b)
    o_b = mla_attention(q_cat, k_cat, v_b)
    o_c = dilated_branch(p_r, cos_c, sin_c)
    merged = merge_branches(o_a, o_b, o_c, p_r, p["w_branch_a"], p["w_branch_b"], p["w_branch_c"], layer)
    return matmul_acc(merged, p["w_out"], layer, x_res, bm=1024, bn=512)


def kernel(x, attn_norm, w_in, conv_w, gdn_a_log, gdn_dt_bias, gdn_norm, mla_q_norm, mla_w_uq, mla_kv_norm, mla_w_ukv, w_branch_a, w_branch_b, w_branch_c, w_out, ffn_norm, dense_w1, dense_w3, dense_w2, router_w, moe_w1, moe_w3, moe_w2, final_norm):
    p = dict(w_in=w_in, conv_w=conv_w, gdn_a_log=gdn_a_log, gdn_dt_bias=gdn_dt_bias, gdn_norm=gdn_norm,
             mla_q_norm=mla_q_norm, mla_w_uq=mla_w_uq, mla_kv_norm=mla_kv_norm, mla_w_ukv=mla_w_ukv,
             w_branch_a=w_branch_a, w_branch_b=w_branch_b, w_branch_c=w_branch_c, w_out=w_out)
    Bn, S, D = x.shape
    assert Bn == 1 and D == D_MODEL
    cos_b, sin_b = _rope_tables(S, B_ROPE)
    cos_c, sin_c = _rope_tables(S, C_DH)
    tables = (jnp.concatenate([cos_b, cos_b, sin_b, sin_b], axis=1),
              jnp.concatenate([cos_c, cos_c], axis=1), jnp.concatenate([-sin_c, sin_c], axis=1))
    xr = x.reshape(S, D)
    h = rmsnorm(xr, attn_norm[0], BF16)
    out = None
    for layer in range(DEPTH):
        last = layer + 1 == DEPTH
        xr = hybrid_mixer(h, xr, layer, p, tables)
        i = layer // 2
        if layer % 2 == 0:
            h = rmsnorm(xr, ffn_norm[layer], BF16)
            xr = dense_ffn(h, dense_w1, dense_w3, dense_w2, i, xr)
            if last:
                out = rmsnorm(xr, final_norm, F32)
            else:
                h = rmsnorm(xr, attn_norm[layer + 1], BF16)
        else:
            h = rmsnorm(xr, ffn_norm[layer], F32)
            y = moe_layer(h, router_w[i], moe_w1, moe_w3, moe_w2, i)
            if last:
                out = add_rmsnorm_final(xr, y, final_norm)
            else:
                xr, h = add_rmsnorm(xr, y, attn_norm[layer + 1], BF16)
    return out.reshape(Bn, S, D)
```

```python
import functools
import math

import jax
import jax.numpy as jnp
from jax import lax
from jax.experimental import pallas as pl
from jax.experimental.pallas import tpu as pltpu

F32 = jnp.float32
BF16 = jnp.bfloat16

EPS = 1e-6
ROPE_THETA = 10000.0
NEG = -1e30

LANES = 128
VMEM_LIMIT_BYTES = 56 * 1024 * 1024

D_MODEL = 4096
DEPTH = 2
A_HEADS, A_DK, A_DV, A_CONV, A_CHUNK = 16, 128, 128, 4, 64
B_HEADS, B_Q_LORA, B_KV_LORA, B_NOPE, B_ROPE, B_VDIM = 16, 1024, 512, 128, 64, 128
C_HEADS, C_DH = 16, 128
C_PATTERNS = ((128, 1), (512, 4), (2048, 16))
D_FF = 11008
N_EXPERTS, TOP_K, D_EXPERT = 8, 2, 4096

A_QK_W = A_HEADS * A_DK
A_V_W = A_HEADS * A_DV
A_QKV_W = 2 * A_QK_W + A_V_W
C_W = C_HEADS * C_DH
OFF_A_IN = A_QKV_W + A_V_W
OFF_B_IN = OFF_A_IN + A_HEADS
OFF_CQ = OFF_B_IN + A_HEADS
OFF_CKV = OFF_CQ + B_Q_LORA
OFF_KROPE = OFF_CKV + B_KV_LORA
OFF_REST = OFF_KROPE + B_ROPE
REST_W = 3 * C_W + 3 * D_MODEL
PS_CQ, PS_CKV, PS_KR, PS_AB, PS_W = 0, 1024, 1536, 1664, 1792

B_QH = 384
B_KH = 256


def _cparams(*sem):
    return pltpu.CompilerParams(dimension_semantics=sem, vmem_limit_bytes=VMEM_LIMIT_BYTES)


def _dot(a, b):
    return jnp.dot(a, b, preferred_element_type=F32)


def _dot_nt(a, b):
    return lax.dot_general(a, b, (((1,), (1,)), ((), ())), preferred_element_type=F32)


def _sigmoid(x):
    return 1.0 / (1.0 + jnp.exp(-x))


def _silu(x):
    return x * _sigmoid(x)


def _softplus(x):
    return jnp.maximum(x, 0.0) + jnp.log(1.0 + jnp.exp(-jnp.abs(x)))


def _rmsnorm_body(x_ref, g_ref, o_ref):
    x = x_ref[...]
    y = x * lax.rsqrt(jnp.mean(x * x, axis=-1, keepdims=True) + EPS)
    o_ref[...] = (y * g_ref[...]).astype(o_ref.dtype)


def rmsnorm(x, g, out_dtype, bm=256):
    M, D = x.shape
    return pl.pallas_call(
        _rmsnorm_body,
        grid=(M // bm,),
        in_specs=[pl.BlockSpec((bm, D), lambda i: (i, 0)), pl.BlockSpec((1, D), lambda i: (0, 0))],
        out_specs=pl.BlockSpec((bm, D), lambda i: (i, 0)),
        out_shape=jax.ShapeDtypeStruct((M, D), out_dtype),
        compiler_params=_cparams("parallel"),
        name="rmsnorm",
    )(x, g.reshape(1, D))


def _add_rmsnorm_body(x_ref, y_ref, g_ref, xo_ref, ho_ref):
    x = x_ref[...] + y_ref[...].astype(F32)
    xo_ref[...] = x
    y = x * lax.rsqrt(jnp.mean(x * x, axis=-1, keepdims=True) + EPS)
    ho_ref[...] = (y * g_ref[...]).astype(ho_ref.dtype)


def add_rmsnorm(x, y, g, out_dtype, bm=256):
    M, D = x.shape
    row = pl.BlockSpec((bm, D), lambda i: (i, 0))
    return pl.pallas_call(
        _add_rmsnorm_body,
        grid=(M // bm,),
        in_specs=[row, row, pl.BlockSpec((1, D), lambda i: (0, 0))],
        out_specs=[row, row],
        out_shape=[jax.ShapeDtypeStruct((M, D), F32), jax.ShapeDtypeStruct((M, D), out_dtype)],
        compiler_params=_cparams("parallel"),
        name="add_rmsnorm",
    )(x, y, g.reshape(1, D))


def _add_rmsnorm_final_body(x_ref, y_ref, g_ref, o_ref):
    x = x_ref[...] + y_ref[...].astype(F32)
    y = x * lax.rsqrt(jnp.mean(x * x, axis=-1, keepdims=True) + EPS)
    o_ref[...] = y * g_ref[...]


def add_rmsnorm_final(x, y, g, bm=256):
    M, D = x.shape
    row = pl.BlockSpec((bm, D), lambda i: (i, 0))
    return pl.pallas_call(
        _add_rmsnorm_final_body,
        grid=(M // bm,),
        in_specs=[row, row, pl.BlockSpec((1, D), lambda i: (0, 0))],
        out_specs=row,
        out_shape=jax.ShapeDtypeStruct((M, D), F32),
        compiler_params=_cparams("parallel"),
        name="add_rmsnorm_final",
    )(x, y, g.reshape(1, D))


def _mm_acc_body(x_ref, w_ref, acc_ref, o_ref, wbf_ref):
    @pl.when(pl.program_id(1) == 0)
    def _():
        wbf_ref[...] = w_ref[...].astype(BF16)

    o_ref[...] = acc_ref[...] + _dot(x_ref[...], wbf_ref[...])


def matmul_acc(x, w, layer, acc, bm, bn):
    M, K = x.shape
    N = acc.shape[1]
    tile = pl.BlockSpec((bm, bn), lambda j, i: (i, j))
    return pl.pallas_call(
        _mm_acc_body,
        grid=(N // bn, M // bm),
        in_specs=[pl.BlockSpec((bm, K), lambda j, i: (i, 0)),
                  pl.BlockSpec((None, K, bn), lambda j, i: (layer, 0, j)), tile],
        out_specs=tile,
        out_shape=jax.ShapeDtypeStruct((M, N), F32),
        scratch_shapes=[pltpu.VMEM((K, bn), BF16)],
        compiler_params=_cparams("parallel", "arbitrary"),
        name="matmul_acc",
    )(x, w, acc)


def _mm_nt_body(x_ref, wt_ref, o_ref, wbf_ref):
    @pl.when(pl.program_id(1) == 0)
    def _():
        wbf_ref[...] = wt_ref[...].astype(BF16)

    o_ref[...] = _dot_nt(x_ref[...], wbf_ref[...]).astype(o_ref.dtype)


def matmul_nt(x, wt, layer, n_cols, out_dtype, bm, bn):
    M, K = x.shape
    return pl.pallas_call(
        _mm_nt_body,
        grid=(n_cols // bn, M // bm),
        in_specs=[pl.BlockSpec((bm, K), lambda j, i: (i, 0)),
                  pl.BlockSpec((None, bn, K), lambda j, i: (layer, j, 0))],
        out_specs=pl.BlockSpec((bm, bn), lambda j, i: (i, j)),
        out_shape=jax.ShapeDtypeStruct((M, n_cols), out_dtype),
        scratch_shapes=[pltpu.VMEM((bn, K), BF16)],
        compiler_params=_cparams("parallel", "arbitrary"),
        name="matmul_nt",
    )(x, wt)


MM_TAIL = 128


def _mm_nt_shifted_body(x_ref, wa_ref, wb_ref, o_ref, wbf_ref, *, shift):
    @pl.when(pl.program_id(1) == 0)
    def _():
        keep = wa_ref.shape[0] - shift
        wbf_ref[:keep, :] = wa_ref[shift:, :].astype(BF16)
        wbf_ref[keep:, :] = wb_ref[:shift, :].astype(BF16)

    o_ref[...] = _dot_nt(x_ref[...], wbf_ref[...]).astype(o_ref.dtype)


def matmul_nt_shifted(x, wt, layer, row0, n_cols, out_dtype, bm, bn):
    M, K = x.shape
    base, shift = divmod(row0, MM_TAIL)
    assert (base * MM_TAIL) % bn == 0 and 0 < shift and shift % 16 == 0
    ja = base * MM_TAIL // bn
    per = bn // MM_TAIL
    return pl.pallas_call(
        functools.partial(_mm_nt_shifted_body, shift=shift),
        grid=(n_cols // bn, M // bm),
        in_specs=[pl.BlockSpec((bm, K), lambda j, i: (i, 0)),
                  pl.BlockSpec((None, bn, K), lambda j, i: (layer, ja + j, 0)),
                  pl.BlockSpec((None, MM_TAIL, K), lambda j, i: (layer, base + per * (j + 1), 0))],
        out_specs=pl.BlockSpec((bm, bn), lambda j, i: (i, j)),
        out_shape=jax.ShapeDtypeStruct((M, n_cols), out_dtype),
        scratch_shapes=[pltpu.VMEM((bn, K), BF16)],
        compiler_params=_cparams("parallel", "arbitrary"),
        name="matmul_nt_shifted",
    )(x, wt, wt)


GDN_ROWS = 256
GDN_HB = 8
GDN_PREV = 16


def _gdn_gates_body(ab_ref, alog_ref, dtb_ref, o_ref):
    ab = ab_ref[...]
    lane = lax.broadcasted_iota(jnp.int32, ab.shape, 1)
    g = -jnp.exp(alog_ref[...]) * _softplus(ab + dtb_ref[...])
    o_ref[...] = jnp.where(lane < A_HEADS, g, _sigmoid(ab))


def gdn_gates(p_s, a_log, dt_bias, bm=512):
    S = p_s.shape[0]
    pad = lambda a: jnp.pad(a.astype(F32), (0, LANES - A_HEADS)).reshape(1, LANES)
    vec = pl.BlockSpec((1, LANES), lambda i: (0, 0))
    return pl.pallas_call(
        _gdn_gates_body,
        grid=(S // bm,),
        in_specs=[pl.BlockSpec((bm, LANES), lambda i: (i, PS_AB // LANES)), vec, vec],
        out_specs=pl.BlockSpec((bm, LANES), lambda i: (i, 0)),
        out_shape=jax.ShapeDtypeStruct((S, LANES), F32),
        compiler_params=_cparams("parallel"),
        name="gdn_gates",
    )(p_s, pad(a_log), pad(dt_bias))


def _gdn_body(q_ref, k_ref, v_ref, qp_ref, kp_ref, vp_ref, z_ref, gb_ref, cwq_ref, cwk_ref, cwv_ref,
              nw_ref, o_ref, state_ref):
    hb = pl.program_id(0)
    i = pl.program_id(1)
    R = GDN_ROWS
    C = A_CHUNK
    NC = R // C

    @pl.when(i == 0)
    def _():
        state_ref[...] = jnp.zeros_like(state_ref)

    has_prev = (i > 0).astype(F32)
    lane = lax.broadcasted_iota(jnp.int32, (R, LANES), 1)
    row = lax.broadcasted_iota(jnp.int32, (R, LANES), 0)
    rin = row & (C - 1)
    ri = lax.broadcasted_iota(jnp.int32, (R, R), 0)
    ci = lax.broadcasted_iota(jnp.int32, (R, R), 1)
    same = (ri // C) == (ci // C)
    incl = same & (ri >= ci)
    strict = same & (ri > ci)

    g_all = beta_all = gb_ref[...]
    nw = nw_ref[...]

    def conv_silu(cur_ref, prev_ref, cw_ref, sl):
        cur = cur_ref[:, sl].astype(F32)
        prev = prev_ref[:, sl].astype(F32) * has_prev
        full = jnp.concatenate([prev, cur], axis=0)
        cw = cw_ref[:, sl]
        y = cw[A_CONV - 1:A_CONV, :] * cur
        for s in range(1, A_CONV):
            y = y + cw[A_CONV - 1 - s:A_CONV - s, :] * pltpu.roll(full, s, axis=0)[GDN_PREV:, :]
        return _silu(y)

    heads = range(GDN_HB)
    lanes_of = [slice(hh * LANES, (hh + 1) * LANES) for hh in heads]
    st = [dict() for _ in heads]
    for hh, t in enumerate(st):
        h = hb * GDN_HB + hh
        g = jnp.broadcast_to(jnp.sum(jnp.where(lane == h, g_all, 0.0), axis=1, keepdims=True), (R, LANES))
        t["beta"] = jnp.broadcast_to(
            jnp.sum(jnp.where(lane == h + A_HEADS, beta_all, 0.0), axis=1, keepdims=True), (R, LANES))
        gc = g
        s = 1
        while s < C:
            gc = gc + jnp.where(rin >= s, pltpu.roll(gc, s, axis=0), 0.0)
            s *= 2
        t["g_last"] = jnp.broadcast_to(jnp.sum(g.reshape(NC, C, LANES), axis=1, keepdims=True),
                                       (NC, C, LANES)).reshape(R, LANES)
        t["gc"] = gc
        t["eg"] = jnp.exp(gc)
        gc_t = gc.T
        diff = jnp.concatenate([gc, gc], axis=1) - jnp.concatenate([gc_t, gc_t], axis=0)
        t["decay"] = jnp.exp(jnp.where(incl, diff, NEG))
    for hh, t in enumerate(st):
        sl = lanes_of[hh]
        q = conv_silu(q_ref, qp_ref, cwq_ref, sl)
        k = conv_silu(k_ref, kp_ref, cwk_ref, sl)
        t["v"] = conv_silu(v_ref, vp_ref, cwv_ref, sl)
        t["qn"] = q * lax.rsqrt(jnp.sum(q * q, axis=-1, keepdims=True) + EPS) * (A_DK ** -0.5)
        t["kn"] = k * lax.rsqrt(jnp.sum(k * k, axis=-1, keepdims=True) + EPS)
    for t in st:
        kn, qn, beta, eg, decay = t["kn"], t["qn"], t["beta"], t["eg"], t["decay"]
        kb = kn * beta
        kn_b = kn.astype(BF16)
        lower = jnp.where(strict, _dot_nt(kb.astype(BF16), kn_b) * decay, 0.0)
        t["xp"] = t["qmat"] = -lower
        t["qk"] = (_dot_nt(qn.astype(BF16), kn_b) * decay).astype(BF16)
        t["rhs"] = jnp.concatenate([t["v"] * beta, kb * eg], axis=1)
        t["q_dec"] = (qn * eg).astype(BF16)
        t["k_dec"] = kn * jnp.exp(t["g_last"] - t["gc"])

    for _ in range(int(math.log2(C)) - 1):
        for t in st:
            xb = t["xp"].astype(BF16)
            t["xp"] = _dot(xb, xb)
            t["qmat"] = t["qmat"] + t["xp"] + _dot(t["qmat"].astype(BF16), t["xp"].astype(BF16))
    for t in st:
        sol = t["rhs"] + _dot(t["qmat"].astype(BF16), t["rhs"].astype(BF16))
        t["u"] = sol[:, :A_DV]
        t["w_b"] = sol[:, A_DV:].astype(BF16)
        t["outs"] = []

    states = [state_ref[hh] for hh in heads]
    for c in range(NC):
        rs = slice(c * C, (c + 1) * C)
        for hh, t in enumerate(st):
            sb = states[hh].astype(BF16)
            v_new_b = (t["u"][rs] - _dot(t["w_b"][rs], sb)).astype(BF16)
            t["outs"].append(_dot(t["q_dec"][rs], sb) + _dot(t["qk"][rs, :][:, rs], v_new_b))
            states[hh] = (states[hh] * jnp.exp(t["g_last"][c * C:c * C + 1, :])
                          + _dot(t["k_dec"][rs].T.astype(BF16), v_new_b))
    for hh, t in enumerate(st):
        sl = lanes_of[hh]
        state_ref[hh] = states[hh]
        out = jnp.concatenate(t["outs"], axis=0)
        out = out * lax.rsqrt(jnp.mean(out * out, axis=-1, keepdims=True) + EPS) * nw
        o_ref[:, sl] = (out * _silu(z_ref[:, sl].astype(F32))).astype(o_ref.dtype)


def gdn_branch(p_a, p_s, conv_w, a_log, dt_bias, norm_w):
    S = p_a.shape[0]
    R, W = GDN_ROWS, GDN_HB * LANES
    nq = A_QK_W // W
    gb = gdn_gates(p_s, a_log, dt_bias)

    def cur(off):
        return pl.BlockSpec((R, W), lambda hb, i: (i, off + hb))

    def prev(off):
        return pl.BlockSpec((GDN_PREV, W), lambda hb, i: (jnp.maximum(i * (R // GDN_PREV) - 1, 0), off + hb))

    def cw(off):
        return pl.BlockSpec((A_CONV, W), lambda hb, i: (0, off + hb))

    vec = pl.BlockSpec((1, LANES), lambda hb, i: (0, 0))
    return pl.pallas_call(
        _gdn_body,
        grid=(A_HEADS // GDN_HB, S // R),
        in_specs=[cur(0), cur(nq), cur(2 * nq), prev(0), prev(nq), prev(2 * nq), cur(3 * nq),
                  pl.BlockSpec((R, LANES), lambda hb, i: (i, 0)),
                  cw(0), cw(nq), cw(2 * nq), vec],
        out_specs=pl.BlockSpec((R, W), lambda hb, i: (i, hb)),
        out_shape=jax.ShapeDtypeStruct((S, A_V_W), BF16),
        scratch_shapes=[pltpu.VMEM((GDN_HB, A_DK, A_DV), F32)],
        compiler_params=_cparams("parallel", "arbitrary"),
        name="gdn",
    )(p_a, p_a, p_a, p_a, p_a, p_a, p_a, gb, conv_w, conv_w, conv_w, norm_w.astype(F32).reshape(1, LANES))


MLA_HG = 4


def _mla_prep_body(cq_ref, ckv_ref, kr_ref, qn_ref, kvn_ref, wq_ref, wkv_ref, cs_ref, qo_ref, ko_ref, vo_ref):
    scale = (B_NOPE + B_ROPE) ** -0.5 * math.log2(math.e)
    cq = cq_ref[...]
    cq = (cq * lax.rsqrt(jnp.mean(cq * cq, axis=-1, keepdims=True) + EPS) * qn_ref[...]).astype(BF16)
    ckv = ckv_ref[...]
    ckv = (ckv * lax.rsqrt(jnp.mean(ckv * ckv, axis=-1, keepdims=True) + EPS) * kvn_ref[...]).astype(BF16)
    rq = _dot(cq, wq_ref[...])
    rkv = _dot(ckv, wkv_ref[...])
    cs = cs_ref[...]
    sc = pltpu.roll(cs, B_ROPE, axis=1)
    prod = kr_ref[...] * cs
    lane = lax.broadcasted_iota(jnp.int32, prod.shape, 1)
    k_pe = jnp.where(lane < B_ROPE, prod + pltpu.roll(prod, B_ROPE, axis=1), 0.0).astype(BF16)
    for hh in range(MLA_HG):
        qb = hh * B_QH
        roped = rq[:, qb + LANES:qb + 2 * LANES] * cs + rq[:, qb + 2 * LANES:qb + 3 * LANES] * sc
        qo_ref[:, hh * B_KH:hh * B_KH + LANES] = (rq[:, qb:qb + LANES] * scale).astype(BF16)
        qo_ref[:, hh * B_KH + LANES:(hh + 1) * B_KH] = (roped * scale).astype(BF16)
        ko_ref[:, hh * B_KH:hh * B_KH + LANES] = rkv[:, hh * 2 * LANES:hh * 2 * LANES + LANES].astype(BF16)
        ko_ref[:, hh * B_KH + LANES:(hh + 1) * B_KH] = k_pe
        vo_ref[:, hh * 2 * LANES:hh * 2 * LANES + LANES] = (
            rkv[:, hh * 2 * LANES + LANES:(hh + 1) * 2 * LANES].astype(BF16))
        vo_ref[:, hh * 2 * LANES + LANES:(hh + 1) * 2 * LANES] = jnp.ones((rkv.shape[0], LANES), BF16)


def mla_prep(p_s, q_norm, kv_norm, wq_ext, wkv, cs_b, bm=512):
    S = p_s.shape[0]
    HG = MLA_HG
    return pl.pallas_call(
        _mla_prep_body,
        grid=(B_HEADS // HG, S // bm),
        in_specs=[
            pl.BlockSpec((bm, B_Q_LORA), lambda g, i: (i, PS_CQ // B_Q_LORA)),
            pl.BlockSpec((bm, B_KV_LORA), lambda g, i: (i, PS_CKV // B_KV_LORA)),
            pl.BlockSpec((bm, LANES), lambda g, i: (i, PS_KR // LANES)),
            pl.BlockSpec((1, B_Q_LORA), lambda g, i: (0, 0)),
            pl.BlockSpec((1, B_KV_LORA), lambda g, i: (0, 0)),
            pl.BlockSpec((B_Q_LORA, HG * B_QH), lambda g, i: (0, g)),
            pl.BlockSpec((B_KV_LORA, HG * 2 * LANES), lambda g, i: (0, g)),
            pl.BlockSpec((bm, LANES), lambda g, i: (i, 0)),
        ],
        out_specs=[
            pl.BlockSpec((bm, HG * B_KH), lambda g, i: (i, g)),
            pl.BlockSpec((bm, HG * B_KH), lambda g, i: (i, g)),
            pl.BlockSpec((bm, HG * 2 * LANES), lambda g, i: (i, g)),
        ],
        out_shape=[
            jax.ShapeDtypeStruct((S, B_HEADS * B_KH), BF16),
            jax.ShapeDtypeStruct((S, B_HEADS * B_KH), BF16),
            jax.ShapeDtypeStruct((S, B_HEADS * 2 * B_VDIM), BF16),
        ],
        compiler_params=_cparams("parallel", "parallel"),
        name="mla_prep",
    )(p_s, p_s, p_s, q_norm.reshape(1, -1), kv_norm.reshape(1, -1), wq_ext, wkv, cs_b)


FLASH_BLK = 2048
FLASH_RC = 256


def _flash_body(qi_ref, kj_ref, q_ref, k_ref, v_ref, o_ref, m_ref, acc_ref):
    t = pl.program_id(1)
    qi = qi_ref[t]
    kj = kj_ref[t]

    @pl.when(kj == 0)
    def _():
        m_ref[...] = jnp.full_like(m_ref, NEG)
        acc_ref[...] = jnp.zeros_like(acc_ref)

    def step(diag):
        nc = FLASH_BLK // FLASH_RC
        rows_of = [slice(c * FLASH_RC, (c + 1) * FLASH_RC) for c in range(nc)]
        ncol_of = [(c + 1) * FLASH_RC if diag else FLASH_BLK for c in range(nc)]

        def scores(c):
            s = _dot_nt(q_ref[rows_of[c], :], k_ref[:ncol_of[c], :])
            if diag:
                r = lax.broadcasted_iota(jnp.int32, s.shape, 0) + c * FLASH_RC
                col = lax.broadcasted_iota(jnp.int32, s.shape, 1)
                s = jnp.where(col <= r, s, NEG)
            return s

        s_next = scores(0)
        for c in range(nc):
            s = s_next
            if c + 1 < nc:
                s_next = scores(c + 1)
            rows = rows_of[c]
            m_prev = m_ref[rows, :]
            m_new = jnp.maximum(m_prev, jnp.max(s, axis=1, keepdims=True))
            p = jnp.exp2(s - m_new).astype(BF16)
            acc_ref[rows, :] = jnp.exp2(m_prev - m_new) * acc_ref[rows, :] + _dot(p, v_ref[:ncol_of[c], :])
            m_ref[rows, :] = m_new

    @pl.when(kj < qi)
    def _():
        step(False)

    @pl.when(kj == qi)
    def _():
        step(True)
        acc = acc_ref[...]
        o_ref[...] = (acc[:, :B_VDIM] / acc[:, B_VDIM:]).astype(o_ref.dtype)


def mla_attention(q_cat, k_cat, v_ext):
    S = q_cat.shape[0]
    blk = FLASH_BLK
    n = S // blk
    pairs = [(qi, kj) for qi in range(n) for kj in range(qi + 1)]
    qi_tab = jnp.asarray([p[0] for p in pairs], jnp.int32)
    kj_tab = jnp.asarray([p[1] for p in pairs], jnp.int32)
    return pl.pallas_call(
        _flash_body,
        grid_spec=pltpu.PrefetchScalarGridSpec(
            num_scalar_prefetch=2,
            grid=(B_HEADS, len(pairs)),
            in_specs=[
                pl.BlockSpec((blk, B_KH), lambda h, t, qi, kj: (qi[t], h)),
                pl.BlockSpec((blk, B_KH), lambda h, t, qi, kj: (kj[t], h)),
                pl.BlockSpec((blk, 2 * B_VDIM), lambda h, t, qi, kj: (kj[t], h)),
            ],
            out_specs=pl.BlockSpec((blk, B_VDIM), lambda h, t, qi, kj: (qi[t], h)),
            scratch_shapes=[pltpu.VMEM((blk, 1), F32), pltpu.VMEM((blk, 2 * B_VDIM), F32)],
        ),
        out_shape=jax.ShapeDtypeStruct((S, B_HEADS * B_VDIM), BF16),
        compiler_params=_cparams("parallel", "arbitrary"),
        name="mla_flash",
    )(qi_tab, kj_tab, q_cat, k_cat, v_ext)


DIL_BLK = 128
DIL_ROWS = 256
DILATIONS = tuple(d for _, d in C_PATTERNS)


def _rope_split_body(q_ref, k_ref, v_ref, c_ref, s_ref, *refs):
    nd = len(DILATIONS)
    outs, (sq, sk, sv) = refs[:3 * nd], refs[3 * nd:]
    c = c_ref[...]
    s = s_ref[...]
    scale = C_DH ** -0.5
    for h in range(C_HEADS):
        sl = slice(h * C_DH, (h + 1) * C_DH)
        x = q_ref[:, sl].astype(F32)
        sq[h] = (x * c + pltpu.roll(x, C_DH // 2, axis=1) * s) * scale
        x = k_ref[:, sl].astype(F32)
        sk[h] = x * c + pltpu.roll(x, C_DH // 2, axis=1) * s
        sv[h] = v_ref[:, sl].astype(F32)
        for di, d in enumerate(DILATIONS):
            for src, dst in zip((sq, sk, sv), outs[3 * di:3 * di + 3]):
                for r in range(d):
                    rows = src[h] if d == 1 else src[h, pl.ds(r, DIL_ROWS // d, stride=d), :]
                    dst[r, :, sl] = rows.astype(BF16)


def rope_split(p_r, cos_c, sin_c):
    S = p_r.shape[0]
    bm = DIL_ROWS
    tab = pl.BlockSpec((bm, C_DH), lambda i: (i, 0))
    col = lambda j: pl.BlockSpec((bm, C_W), lambda i: (i, j))
    out_specs, out_shape = [], []
    for d in DILATIONS:
        out_specs += [pl.BlockSpec((d, bm // d, C_W), lambda i: (0, i, 0))] * 3
        out_shape += [jax.ShapeDtypeStruct((d, S // d, C_W), BF16)] * 3
    outs = pl.pallas_call(
        _rope_split_body,
        grid=(S // bm,),
        in_specs=[col(0), col(1), col(2), tab, tab],
        out_specs=out_specs,
        out_shape=out_shape,
        scratch_shapes=[pltpu.VMEM((C_HEADS, bm, C_DH), F32)] * 3,
        compiler_params=_cparams("parallel"),
        name="rope_split",
    )(p_r, p_r, p_r, cos_c, sin_c)
    return [outs[3 * i:3 * i + 3] for i in range(len(DILATIONS))]


def _dilated_body(q_ref, kc_ref, kp_ref, vc_ref, vp_ref, o_ref, lse_ref):
    B = DIL_BLK
    n = pl.program_id(1)
    qi = lax.broadcasted_iota(jnp.int32, (B, 2 * B), 0)
    kj = lax.broadcasted_iota(jnp.int32, (B, 2 * B), 1)
    dist = qi + B - kj
    valid = (dist >= 0) & (dist <= B) & ((kj >= B) | (n > 0))
    lanes_of = [slice(h * C_DH, (h + 1) * C_DH) for h in range(C_HEADS)]
    scores = [jnp.where(valid, _dot_nt(q_ref[:, sl], jnp.concatenate([kp_ref[:, sl], kc_ref[:, sl]], axis=0)), NEG)
              for sl in lanes_of]
    probs = []
    lane = lax.broadcasted_iota(jnp.int32, (B, LANES), 1)
    lse = jnp.zeros((B, LANES), F32)
    for h, s in enumerate(scores):
        m = jnp.max(s, axis=1, keepdims=True)
        p = jnp.exp(s - m)
        l = jnp.sum(p, axis=1, keepdims=True)
        lse = jnp.where(lane == h, m + jnp.log(l), lse)
        probs.append((p.astype(BF16), l))
    lse_ref[...] = lse
    for sl, (p, l) in zip(lanes_of, probs):
        vv = jnp.concatenate([vp_ref[:, sl], vc_ref[:, sl]], axis=0)
        o_ref[:, sl] = (_dot(p, vv) / l).astype(o_ref.dtype)


def dilated_pattern(q, k, v):
    d, L, W = q.shape
    B = DIL_BLK
    cur = pl.BlockSpec((None, B, W), lambda r, n: (r, n, 0))
    prv = pl.BlockSpec((None, B, W), lambda r, n: (r, jnp.maximum(n - 1, 0), 0))
    return pl.pallas_call(
        _dilated_body,
        grid=(d, L // B),
        in_specs=[cur, cur, prv, cur, prv],
        out_specs=[cur, pl.BlockSpec((None, B, LANES), lambda r, n: (r, n, 0))],
        out_shape=[jax.ShapeDtypeStruct((d, L, W), BF16), jax.ShapeDtypeStruct((d, L, LANES), F32)],
        compiler_params=_cparams("parallel", "parallel"),
        name=f"dilated_d{d}",
    )(q, k, k, v, v)


def _dilated_merge_body(*refs):
    nd = len(DILATIONS)
    o_refs, lse_refs, out_ref = refs[:nd], refs[nd:2 * nd], refs[2 * nd]
    so, sl = refs[2 * nd + 1:]
    rows_of = lambda d, r: slice(None) if d == 1 else pl.ds(r, DIL_ROWS // d, stride=d)
    for di, d in enumerate(DILATIONS):
        for r in range(d):
            sl[di, rows_of(d, r), :] = lse_refs[di][r]
    top = sl[0]
    for di in range(1, nd):
        top = jnp.maximum(top, sl[di])
    weights = [jnp.exp(sl[di] - top) for di in range(nd)]
    lane = lax.broadcasted_iota(jnp.int32, top.shape, 1)
    for h in range(C_HEADS):
        cols = slice(h * C_DH, (h + 1) * C_DH)
        for di, d in enumerate(DILATIONS):
            for r in range(d):
                so[di, rows_of(d, r), :] = o_refs[di][r, :, cols].astype(F32)
        num = 0.0
        den = 0.0
        for di in range(nd):
            w = jnp.sum(jnp.where(lane == h, weights[di], 0.0), axis=1, keepdims=True)
            num = num + w * so[di]
            den = den + w
        out_ref[:, cols] = (num / den).astype(out_ref.dtype)


def dilated_merge(outs, lses):
    S = outs[0].shape[0] * outs[0].shape[1]
    bm = DIL_ROWS
    nd = len(DILATIONS)
    specs = [pl.BlockSpec((d, bm // d, C_W), lambda i: (0, i, 0)) for d in DILATIONS]
    lse_specs = [pl.BlockSpec((d, bm // d, LANES), lambda i: (0, i, 0)) for d in DILATIONS]
    return pl.pallas_call(
        _dilated_merge_body,
        grid=(S // bm,),
        in_specs=specs + lse_specs,
        out_specs=pl.BlockSpec((bm, C_W), lambda i: (i, 0)),
        out_shape=jax.ShapeDtypeStruct((S, C_W), BF16),
        scratch_shapes=[pltpu.VMEM((nd, bm, C_DH), F32)] * 2,
        compiler_params=_cparams("parallel"),
        name="dilated_merge",
    )(*outs, *lses)


def dilated_branch(p_r, cos_c, sin_c):
    assert all(w // d == DIL_BLK for w, d in C_PATTERNS)
    results = [dilated_pattern(q, k, v) for q, k, v in rope_split(p_r, cos_c, sin_c)]
    return dilated_merge([r[0] for r in results], [r[1] for r in results])


def _merge_body(oa_ref, ob_ref, oc_ref, ga_ref, gb_ref, gc_ref, wa_ref, wb_ref, wc_ref, o_ref, wa_s, wb_s, wc_s):
    @pl.when(pl.program_id(1) == 0)
    def _():
        wa_s[...] = wa_ref[...].astype(BF16)
        wb_s[...] = wb_ref[...].astype(BF16)
        wc_s[...] = wc_ref[...].astype(BF16)

    acc = _sigmoid(ga_ref[...].astype(F32)) * _dot(oa_ref[...], wa_s[...])
    acc = acc + _sigmoid(gb_ref[...].astype(F32)) * _dot(ob_ref[...], wb_s[...])
    acc = acc + _sigmoid(gc_ref[...].astype(F32)) * _dot(oc_ref[...], wc_s[...])
    o_ref[...] = acc.astype(o_ref.dtype)


def merge_branches(o_a, o_b, o_c, p_r, w_a, w_b, w_c, layer, bm=512, bn=512):
    S, K = o_a.shape
    g0 = 3 * C_W // bn
    gstep = D_MODEL // bn
    act = pl.BlockSpec((bm, K), lambda j, i: (i, 0))
    wsp = pl.BlockSpec((None, K, bn), lambda j, i: (layer, 0, j))
    gate = lambda b: pl.BlockSpec((bm, bn), lambda j, i: (i, g0 + b * gstep + j))
    return pl.pallas_call(
        _merge_body,
        grid=(D_MODEL // bn, S // bm),
        in_specs=[act, act, act, gate(0), gate(1), gate(2), wsp, wsp, wsp],
        out_specs=pl.BlockSpec((bm, bn), lambda j, i: (i, j)),
        out_shape=jax.ShapeDtypeStruct((S, D_MODEL), BF16),
        scratch_shapes=[pltpu.VMEM((K, bn), BF16)] * 3,
        compiler_params=_cparams("parallel", "arbitrary"),
        name="merge_branches",
    )(o_a, o_b, o_c, p_r, p_r, p_r, w_a, w_b, w_c)


def _new_group(tg_ref):
    i = pl.program_id(1)
    return (i == 0) | (tg_ref[i] != tg_ref[jnp.maximum(i - 1, 0)])


def _swiglu_up_body(tg_ref, nt_ref, x_ref, w1_ref, w3_ref, o_ref, w1_s, w3_s):
    @pl.when(_new_group(tg_ref))
    def _():
        w1_s[...] = w1_ref[...].astype(BF16)
        w3_s[...] = w3_ref[...].astype(BF16)

    @pl.when(pl.program_id(1) < nt_ref[0])
    def _():
        x = x_ref[...]
        o_ref[...] = (_silu(_dot(x, w1_s[...])) * _dot(x, w3_s[...])).astype(o_ref.dtype)

    @pl.when(pl.program_id(1) >= nt_ref[0])
    def _():
        o_ref[...] = jnp.zeros_like(o_ref)


def swiglu_up(x, w1, w3, tile_group, n_tiles, bm, bf):
    R, D = x.shape
    F = w1.shape[2]
    wspec = pl.BlockSpec((None, D, bf), lambda f, i, tg, nt: (tg[i], 0, f))
    return pl.pallas_call(
        _swiglu_up_body,
        grid_spec=pltpu.PrefetchScalarGridSpec(
            num_scalar_prefetch=2,
            grid=(F // bf, R // bm),
            in_specs=[pl.BlockSpec((bm, D), lambda f, i, tg, nt: (jnp.minimum(i, nt[0] - 1), 0)), wspec, wspec],
            out_specs=pl.BlockSpec((bm, bf), lambda f, i, tg, nt: (i, f)),
            scratch_shapes=[pltpu.VMEM((D, bf), BF16)] * 2,
        ),
        out_shape=jax.ShapeDtypeStruct((R, F), BF16),
        compiler_params=_cparams("arbitrary", "arbitrary"),
        name="swiglu_up",
    )(tile_group, n_tiles, x, w1, w3)


def _grouped_mm_body(tg_ref, nt_ref, x_ref, w_ref, *rest, has_acc):
    if has_acc:
        acc_ref, o_ref, w_s = rest
    else:
        o_ref, w_s = rest

    @pl.when(_new_group(tg_ref))
    def _():
        w_s[...] = w_ref[...].astype(BF16)

    @pl.when(pl.program_id(1) < nt_ref[0])
    def _():
        y = _dot(x_ref[...], w_s[...])
        o_ref[...] = acc_ref[...] + y if has_acc else y

    @pl.when(pl.program_id(1) >= nt_ref[0])
    def _():
        o_ref[...] = jnp.zeros_like(o_ref)


def grouped_matmul(x, w, tile_group, n_tiles, bm, bn, k_blocks=1, k_index=0, acc=None):
    R, K = x.shape
    N = w.shape[2]
    kb = K // k_blocks
    in_specs = [pl.BlockSpec((bm, kb), lambda j, i, tg, nt: (jnp.minimum(i, nt[0] - 1), k_index)),
                pl.BlockSpec((None, kb, bn), lambda j, i, tg, nt: (tg[i], k_index, j))]
    out_spec = pl.BlockSpec((bm, bn), lambda j, i, tg, nt: (i, j))
    args = [tile_group, n_tiles, x, w]
    if acc is not None:
        in_specs.append(out_spec)
        args.append(acc)
    return pl.pallas_call(
        functools.partial(_grouped_mm_body, has_acc=acc is not None),
        grid_spec=pltpu.PrefetchScalarGridSpec(
            num_scalar_prefetch=2,
            grid=(N // bn, R // bm),
            in_specs=in_specs,
            out_specs=out_spec,
            scratch_shapes=[pltpu.VMEM((kb, bn), BF16)],
        ),
        out_shape=jax.ShapeDtypeStruct((R, N), F32),
        input_output_aliases={4: 0} if acc is not None else {},
        compiler_params=_cparams("arbitrary", "arbitrary"),
        name="grouped_matmul",
    )(*args)


DENSE_UP_BM = 1024
DENSE_DOWN_BM = 512
DENSE_KSPLIT = 2


def _one_group(n_rows, bm, group):
    return jnp.full((n_rows // bm,), group, jnp.int32), jnp.full((1,), n_rows // bm, jnp.int32)


def dense_ffn(h, w1, w3, w2, index, x_res):
    S = h.shape[0]
    g = swiglu_up(h, w1, w3, *_one_group(S, DENSE_UP_BM, index), bm=DENSE_UP_BM, bf=256)
    y = x_res
    for k in range(DENSE_KSPLIT):
        y = grouped_matmul(g, w2, *_one_group(S, DENSE_DOWN_BM, index), bm=DENSE_DOWN_BM, bn=512,
                           k_blocks=DENSE_KSPLIT, k_index=k, acc=y)
    return y


def _router_body(h_ref, rw_ref, sel_ref, gate_ref):
    logits = _dot(h_ref[...].astype(BF16), rw_ref[...].astype(BF16))
    lane = lax.broadcasted_iota(jnp.int32, logits.shape, 1)
    logits = jnp.where(lane < N_EXPERTS, logits, NEG)
    t1 = jnp.max(logits, axis=1, keepdims=True)
    i1 = jnp.min(jnp.where(logits == t1, lane, LANES), axis=1, keepdims=True)
    rest = jnp.where(lane == i1, NEG, logits)
    t2 = jnp.max(rest, axis=1, keepdims=True)
    i2 = jnp.min(jnp.where(rest == t2, lane, LANES), axis=1, keepdims=True)
    e2 = jnp.exp(t2 - t1)
    g1 = 1.0 / (1.0 + e2)
    sel_ref[...] = jnp.where(lane == 0, i1, jnp.where(lane == 1, i2, 0))
    gate_ref[...] = jnp.where(lane == 0, g1, jnp.where(lane == 1, e2 * g1, 0.0))


def router(h, router_w, bm=512):
    S, D = h.shape
    rw = jnp.pad(router_w, ((0, 0), (0, LANES - N_EXPERTS)))
    out = pl.BlockSpec((bm, LANES), lambda i: (i, 0))
    return pl.pallas_call(
        _router_body,
        grid=(S // bm,),
        in_specs=[pl.BlockSpec((bm, D), lambda i: (i, 0)), pl.BlockSpec((D, LANES), lambda i: (0, 0))],
        out_specs=[out, out],
        out_shape=[jax.ShapeDtypeStruct((S, LANES), jnp.int32), jax.ShapeDtypeStruct((S, LANES), F32)],
        compiler_params=_cparams("parallel"),
        name="router",
    )(h, rw)


MOE_BM = 512
MOE_BG = 256
DMA_QUEUES = 2


def _moe_plan(sel, bm):
    S = sel.shape[0]
    E = N_EXPERTS
    e = sel[:, :TOP_K].reshape(-1)
    onehot = (e[:, None] == jnp.arange(E, dtype=jnp.int32)[None, :]).astype(jnp.int32)
    csum = jnp.cumsum(onehot, axis=0)
    rank = jnp.sum((csum - 1) * onehot, axis=1)
    tiles_per = (csum[-1] + bm - 1) // bm
    tile_end = jnp.cumsum(tiles_per)
    pos = ((tile_end - tiles_per)[e] * bm + rank).astype(jnp.int32)
    n_tiles = tile_end[-1:].astype(jnp.int32)
    T = (TOP_K * S) // bm + E
    tid = jnp.arange(T, dtype=jnp.int32)
    te = jnp.minimum(jnp.sum((tid[:, None] >= tile_end[None, :]).astype(jnp.int32), axis=1), E - 1)
    te = jnp.where(tid < n_tiles[0], te, te[n_tiles[0] - 1]).astype(jnp.int32)
    src = jnp.zeros((T * bm,), jnp.int32).at[pos].set(jnp.arange(TOP_K * S, dtype=jnp.int32) // TOP_K)
    return pos, src, te, n_tiles


def _moe_gather_body(src_ref, h_hbm, o_ref, buf_ref, sem):
    base = pl.program_id(0) * MOE_BG

    def row_copy(j):
        return pltpu.make_async_copy(h_hbm.at[pl.ds(src_ref[base + j], 1), :], buf_ref.at[pl.ds(j, 1), :], sem)

    def start(jj, c):
        for par in range(DMA_QUEUES):
            row_copy(jj * DMA_QUEUES + par).start(priority=par)
        return c

    def wait(j, c):
        row_copy(j).wait()
        return c

    lax.fori_loop(0, MOE_BG // DMA_QUEUES, start, 0)
    lax.fori_loop(0, MOE_BG, wait, 0)
    o_ref[...] = buf_ref[...].astype(o_ref.dtype)


def moe_gather(h, src):
    S, D = h.shape
    R = src.shape[0]
    return pl.pallas_call(
        _moe_gather_body,
        grid_spec=pltpu.PrefetchScalarGridSpec(
            num_scalar_prefetch=1,
            grid=(R // MOE_BG,),
            in_specs=[pl.BlockSpec(memory_space=pl.ANY)],
            out_specs=pl.BlockSpec((MOE_BG, D), lambda i, src: (i, 0)),
            scratch_shapes=[pltpu.VMEM((MOE_BG, D), F32), pltpu.SemaphoreType.DMA(())],
        ),
        out_shape=jax.ShapeDtypeStruct((R, D), BF16),
        compiler_params=_cparams("arbitrary"),
        name="moe_gather",
    )(src, h)


def _moe_combine_body(pos_ref, gate_ref, ys_hbm, o_ref, buf_ref, sem):
    base = pl.program_id(0) * MOE_BG

    def row_copy(j, k):
        row = pos_ref[(base + j) * TOP_K + k]
        return pltpu.make_async_copy(ys_hbm.at[pl.ds(row, 1), :], buf_ref.at[k, pl.ds(j, 1), :], sem)

    def start(j, c):
        for k in range(TOP_K):
            row_copy(j, k).start(priority=k % DMA_QUEUES)
        return c

    def wait(j, c):
        for k in range(TOP_K):
            row_copy(j, k).wait()
        return c

    lax.fori_loop(0, MOE_BG, start, 0)
    lax.fori_loop(0, MOE_BG, wait, 0)
    gate = gate_ref[...]
    acc = gate[:, 0:1] * buf_ref[0]
    for k in range(1, TOP_K):
        acc = acc + gate[:, k:k + 1] * buf_ref[k]
    o_ref[...] = acc


def moe_combine(ys, pos, gate):
    S = gate.shape[0]
    D = ys.shape[1]
    return pl.pallas_call(
        _moe_combine_body,
        grid_spec=pltpu.PrefetchScalarGridSpec(
            num_scalar_prefetch=1,
            grid=(S // MOE_BG,),
            in_specs=[pl.BlockSpec((MOE_BG, LANES), lambda i, pos: (i, 0)), pl.BlockSpec(memory_space=pl.ANY)],
            out_specs=pl.BlockSpec((MOE_BG, D), lambda i, pos: (i, 0)),
            scratch_shapes=[pltpu.VMEM((TOP_K, MOE_BG, D), F32), pltpu.SemaphoreType.DMA(())],
        ),
        out_shape=jax.ShapeDtypeStruct((S, D), F32),
        compiler_params=_cparams("arbitrary"),
        name="moe_combine",
    )(pos, gate, ys)


def moe_layer(h, router_w, w1, w3, w2, index):
    sel, gate = router(h, router_w)
    pos, src, tile_expert, n_tiles = _moe_plan(sel, MOE_BM)
    tile_expert = tile_expert + index * N_EXPERTS
    w1, w3, w2 = (w.reshape((-1,) + w.shape[2:]) for w in (w1, w3, w2))
    xs = moe_gather(h, src)
    g = swiglu_up(xs, w1, w3, tile_expert, n_tiles, bm=MOE_BM, bf=512)
    ys = grouped_matmul(g, w2, tile_expert, n_tiles, bm=MOE_BM, bn=1024)
    return moe_combine(ys, pos, gate)


def _rot_cols(w):
    half = w.shape[-1] // 2
    return jnp.concatenate([-w[..., half:], w[..., :half]], axis=-1)


def _rope_tables(seq, dim):
    inv = ROPE_THETA ** (-jnp.arange(0, dim, 2, dtype=F32) / dim)
    ang = jnp.arange(seq, dtype=F32)[:, None] * inv[None, :]
    return jnp.cos(ang), jnp.sin(ang)


def _small_weight_t(wt_mid):
    row = lambda off, n: wt_mid[off - OFF_A_IN:off - OFF_A_IN + n]
    kr = row(OFF_KROPE, B_ROPE)
    parts = [row(OFF_CQ, B_Q_LORA), row(OFF_CKV, B_KV_LORA), kr, _rot_cols(kr.T).T,
             row(OFF_A_IN, A_HEADS), row(OFF_B_IN, A_HEADS)]
    w = jnp.concatenate(parts, axis=0)
    return jnp.pad(w, ((0, PS_W - w.shape[0]), (0, 0)))


def _mla_weights(w_uq, w_ukv):
    wq = w_uq.reshape(B_Q_LORA, B_HEADS, B_NOPE + B_ROPE)
    pe = wq[..., B_NOPE:]
    zeros = jnp.zeros_like(pe)
    wq_ext = jnp.concatenate([wq[..., :B_NOPE], pe, zeros, _rot_cols(pe), zeros], axis=-1)
    return wq_ext.reshape(B_Q_LORA, B_HEADS * B_QH).astype(BF16), w_ukv.astype(BF16)


def hybrid_mixer(h, x_res, layer, p, tables):
    cs_b, cos_c, sin_c = tables
    wt_in = jnp.swapaxes(p["w_in"], 1, 2)
    p_a = matmul_nt(h, wt_in, layer, OFF_A_IN, BF16, bm=1024, bn=512)
    p_s = matmul_nt(h, _small_weight_t(wt_in[layer, OFF_A_IN:OFF_REST])[None], 0, PS_W, F32, bm=1024, bn=256)
    p_r = matmul_nt_shifted(h, wt_in, layer, OFF_REST, REST_W, BF16, bm=1024, bn=512)
    o_a = gdn_branch(p_a, p_s, p["conv_w"][layer], p["gdn_a_log"][layer], p["gdn_dt_bias"][layer],
                     p["gdn_norm"][layer])
    wq_ext, wkv = _mla_weights(p["mla_w_uq"][layer], p["mla_w_ukv"][layer])
    q_cat, k_cat, v_b = mla_prep(p_s, p["mla_q_norm"][layer], p["mla_kv_norm"][layer], wq_ext, wkv, cs_b)
    o_b = mla_attention(q_cat, k_cat, v_b)
    o_c = dilated_branch(p_r, cos_c, sin_c)
    merged = merge_branches(o_a, o_b, o_c, p_r, p["w_branch_a"], p["w_branch_b"], p["w_branch_c"], layer)
    return matmul_acc(merged, p["w_out"], layer, x_res, bm=1024, bn=512)


def kernel(x, attn_norm, w_in, conv_w, gdn_a_log, gdn_dt_bias, gdn_norm, mla_q_norm, mla_w_uq, mla_kv_norm, mla_w_ukv, w_branch_a, w_branch_b, w_branch_c, w_out, ffn_norm, dense_w1, dense_w3, dense_w2, router_w, moe_w1, moe_w3, moe_w2, final_norm):
    p = dict(w_in=w_in, conv_w=conv_w, gdn_a_log=gdn_a_log, gdn_dt_bias=gdn_dt_bias, gdn_norm=gdn_norm,
             mla_q_norm=mla_q_norm, mla_w_uq=mla_w_uq, mla_kv_norm=mla_kv_norm, mla_w_ukv=mla_w_ukv,
             w_branch_a=w_branch_a, w_branch_b=w_branch_b, w_branch_c=w_branch_c, w_out=w_out)
    Bn, S, D = x.shape
    assert Bn == 1 and D == D_MODEL
    cos_b, sin_b = _rope_tables(S, B_ROPE)
    cos_c, sin_c = _rope_tables(S, C_DH)
    tables = (jnp.concatenate([cos_b, cos_b, sin_b, sin_b], axis=1),
              jnp.concatenate([cos_c, cos_c], axis=1), jnp.concatenate([-sin_c, sin_c], axis=1))
    xr = x.reshape(S, D)
    h = rmsnorm(xr, attn_norm[0], BF16)
    out = None
    for layer in range(DEPTH):
        last = layer + 1 == DEPTH
        xr = hybrid_mixer(h, xr, layer, p, tables)
        i = layer // 2
        if layer % 2 == 0:
            h = rmsnorm(xr, ffn_norm[layer], BF16)
            xr = dense_ffn(h, dense_w1, dense_w3, dense_w2, i, xr)
            if last:
                out = rmsnorm(xr, final_norm, F32)
            else:
                h = rmsnorm(xr, attn_norm[layer + 1], BF16)
        else:
            h = rmsnorm(xr, ffn_norm[layer], F32)
            y = moe_layer(h, router_w[i], moe_w1, moe_w3, moe_w2, i)
            if last:
                out = add_rmsnorm_final(xr, y, final_norm)
            else:
                xr, h = add_rmsnorm(xr, y, attn_norm[layer + 1], BF16)
    return out.reshape(Bn, S, D)
```

```python
import functools
import math

import jax
import jax.numpy as jnp
from jax import lax
from jax.experimental import pallas as pl
from jax.experimental.pallas import tpu as pltpu

F32 = jnp.float32
BF16 = jnp.bfloat16

EPS = 1e-6
ROPE_THETA = 10000.0
NEG = -1e30

LANES = 128
VMEM_LIMIT_BYTES = 56 * 1024 * 1024

D_MODEL = 4096
DEPTH = 2
A_HEADS, A_DK, A_DV, A_CONV, A_CHUNK = 16, 128, 128, 4, 64
B_HEADS, B_Q_LORA, B_KV_LORA, B_NOPE, B_ROPE, B_VDIM = 16, 1024, 512, 128, 64, 128
C_HEADS, C_DH = 16, 128
C_PATTERNS = ((128, 1), (512, 4), (2048, 16))
D_FF = 11008
N_EXPERTS, TOP_K, D_EXPERT = 8, 2, 4096

A_QK_W = A_HEADS * A_DK
A_V_W = A_HEADS * A_DV
A_QKV_W = 2 * A_QK_W + A_V_W
C_W = C_HEADS * C_DH
OFF_A_IN = A_QKV_W + A_V_W
OFF_B_IN = OFF_A_IN + A_HEADS
OFF_CQ = OFF_B_IN + A_HEADS
OFF_CKV = OFF_CQ + B_Q_LORA
OFF_KROPE = OFF_CKV + B_KV_LORA
OFF_REST = OFF_KROPE + B_ROPE
REST_W = 3 * C_W + 3 * D_MODEL
PS_CQ, PS_CKV, PS_KR, PS_AB, PS_W = 0, 1024, 1536, 1664, 1792

B_QH = 384
B_KH = 256


def _cparams(*sem):
    return pltpu.CompilerParams(dimension_semantics=sem, vmem_limit_bytes=VMEM_LIMIT_BYTES)


def _dot(a, b):
    return jnp.dot(a, b, preferred_element_type=F32)


def _dot_nt(a, b):
    return lax.dot_general(a, b, (((1,), (1,)), ((), ())), preferred_element_type=F32)


def _sigmoid(x):
    return 1.0 / (1.0 + jnp.exp(-x))


def _silu(x):
    return x * _sigmoid(x)


def _softplus(x):
    return jnp.maximum(x, 0.0) + jnp.log(1.0 + jnp.exp(-jnp.abs(x)))


def _rmsnorm_body(x_ref, g_ref, o_ref):
    x = x_ref[...]
    y = x * lax.rsqrt(jnp.mean(x * x, axis=-1, keepdims=True) + EPS)
    o_ref[...] = (y * g_ref[...]).astype(o_ref.dtype)


def rmsnorm(x, g, out_dtype, bm=256):
    M, D = x.shape
    return pl.pallas_call(
        _rmsnorm_body,
        grid=(M // bm,),
        in_specs=[pl.BlockSpec((bm, D), lambda i: (i, 0)), pl.BlockSpec((1, D), lambda i: (0, 0))],
        out_specs=pl.BlockSpec((bm, D), lambda i: (i, 0)),
        out_shape=jax.ShapeDtypeStruct((M, D), out_dtype),
        compiler_params=_cparams("parallel"),
        name="rmsnorm",
    )(x, g.reshape(1, D))


def _rmsnorm_slab_body(x_ref, g_ref, ho_ref, slab_ref):
    x = x_ref[...]
    y = x * lax.rsqrt(jnp.mean(x * x, axis=-1, keepdims=True) + EPS) * g_ref[...]
    ho_ref[...] = y.astype(ho_ref.dtype)
    bm, D = x.shape
    pitch = _slab_pitch(D)
    slab_ref[...] = jnp.zeros_like(slab_ref)
    for c in range(D // LANES):
        slab_ref[pl.ds(c, bm, stride=pitch), :] = y[:, c * LANES:(c + 1) * LANES]


def _slab_pitch(d_model):
    return d_model // LANES + 1


def rmsnorm_slab(x, g, bm=256):
    M, D = x.shape
    pitch = _slab_pitch(D)
    return pl.pallas_call(
        _rmsnorm_slab_body,
        grid=(M // bm,),
        in_specs=[pl.BlockSpec((bm, D), lambda i: (i, 0)), pl.BlockSpec((1, D), lambda i: (0, 0))],
        out_specs=[pl.BlockSpec((bm, D), lambda i: (i, 0)), pl.BlockSpec((bm * pitch, LANES), lambda i: (i, 0))],
        out_shape=[jax.ShapeDtypeStruct((M, D), BF16), jax.ShapeDtypeStruct((M * pitch, LANES), F32)],
        compiler_params=_cparams("parallel"),
        name="rmsnorm_slab",
    )(x, g.reshape(1, D))


def _add_rmsnorm_body(x_ref, y_ref, g_ref, xo_ref, ho_ref):
    x = x_ref[...] + y_ref[...].astype(F32)
    xo_ref[...] = x
    y = x * lax.rsqrt(jnp.mean(x * x, axis=-1, keepdims=True) + EPS)
    ho_ref[...] = (y * g_ref[...]).astype(ho_ref.dtype)


def add_rmsnorm(x, y, g, out_dtype, bm=256):
    M, D = x.shape
    row = pl.BlockSpec((bm, D), lambda i: (i, 0))
    return pl.pallas_call(
        _add_rmsnorm_body,
        grid=(M // bm,),
        in_specs=[row, row, pl.BlockSpec((1, D), lambda i: (0, 0))],
        out_specs=[row, row],
        out_shape=[jax.ShapeDtypeStruct((M, D), F32), jax.ShapeDtypeStruct((M, D), out_dtype)],
        compiler_params=_cparams("parallel"),
        name="add_rmsnorm",
    )(x, y, g.reshape(1, D))


def _add_rmsnorm_final_body(x_ref, y_ref, g_ref, o_ref):
    x = x_ref[...] + y_ref[...].astype(F32)
    y = x * lax.rsqrt(jnp.mean(x * x, axis=-1, keepdims=True) + EPS)
    o_ref[...] = y * g_ref[...]


def add_rmsnorm_final(x, y, g, bm=256):
    M, D = x.shape
    row = pl.BlockSpec((bm, D), lambda i: (i, 0))
    return pl.pallas_call(
        _add_rmsnorm_final_body,
        grid=(M // bm,),
        in_specs=[row, row, pl.BlockSpec((1, D), lambda i: (0, 0))],
        out_specs=row,
        out_shape=jax.ShapeDtypeStruct((M, D), F32),
        compiler_params=_cparams("parallel"),
        name="add_rmsnorm_final",
    )(x, y, g.reshape(1, D))


def _mm_acc_body(x_ref, w_ref, acc_ref, o_ref, wbf_ref):
    @pl.when(pl.program_id(1) == 0)
    def _():
        wbf_ref[...] = w_ref[...].astype(BF16)

    o_ref[...] = acc_ref[...] + _dot(x_ref[...], wbf_ref[...])


def matmul_acc(x, w, layer, acc, bm, bn):
    M, K = x.shape
    N = acc.shape[1]
    tile = pl.BlockSpec((bm, bn), lambda j, i: (i, j))
    return pl.pallas_call(
        _mm_acc_body,
        grid=(N // bn, M // bm),
        in_specs=[pl.BlockSpec((bm, K), lambda j, i: (i, 0)),
                  pl.BlockSpec((None, K, bn), lambda j, i: (layer, 0, j)), tile],
        out_specs=tile,
        out_shape=jax.ShapeDtypeStruct((M, N), F32),
        scratch_shapes=[pltpu.VMEM((K, bn), BF16)],
        compiler_params=_cparams("parallel", "arbitrary"),
        name="matmul_acc",
    )(x, w, acc)


def _mm_nt_body(x_ref, wt_ref, o_ref, wbf_ref):
    @pl.when(pl.program_id(1) == 0)
    def _():
        wbf_ref[...] = wt_ref[...].astype(BF16)

    o_ref[...] = _dot_nt(x_ref[...], wbf_ref[...]).astype(o_ref.dtype)


def matmul_nt(x, wt, layer, n_cols, out_dtype, bm, bn):
    M, K = x.shape
    return pl.pallas_call(
        _mm_nt_body,
        grid=(n_cols // bn, M // bm),
        in_specs=[pl.BlockSpec((bm, K), lambda j, i: (i, 0)),
                  pl.BlockSpec((None, bn, K), lambda j, i: (layer, j, 0))],
        out_specs=pl.BlockSpec((bm, bn), lambda j, i: (i, j)),
        out_shape=jax.ShapeDtypeStruct((M, n_cols), out_dtype),
        scratch_shapes=[pltpu.VMEM((bn, K), BF16)],
        compiler_params=_cparams("parallel", "arbitrary"),
        name="matmul_nt",
    )(x, wt)


MM_TAIL = 128


def _mm_nt_shifted_body(x_ref, wa_ref, wb_ref, o_ref, wbf_ref, *, shift):
    @pl.when(pl.program_id(1) == 0)
    def _():
        keep = wa_ref.shape[0] - shift
        wbf_ref[:keep, :] = wa_ref[shift:, :].astype(BF16)
        wbf_ref[keep:, :] = wb_ref[:shift, :].astype(BF16)

    o_ref[...] = _dot_nt(x_ref[...], wbf_ref[...]).astype(o_ref.dtype)


def matmul_nt_shifted(x, wt, layer, row0, n_cols, out_dtype, bm, bn):
    M, K = x.shape
    base, shift = divmod(row0, MM_TAIL)
    assert (base * MM_TAIL) % bn == 0 and 0 < shift and shift % 16 == 0
    ja = base * MM_TAIL // bn
    per = bn // MM_TAIL
    return pl.pallas_call(
        functools.partial(_mm_nt_shifted_body, shift=shift),
        grid=(n_cols // bn, M // bm),
        in_specs=[pl.BlockSpec((bm, K), lambda j, i: (i, 0)),
                  pl.BlockSpec((None, bn, K), lambda j, i: (layer, ja + j, 0)),
                  pl.BlockSpec((None, MM_TAIL, K), lambda j, i: (layer, base + per * (j + 1), 0))],
        out_specs=pl.BlockSpec((bm, bn), lambda j, i: (i, j)),
        out_shape=jax.ShapeDtypeStruct((M, n_cols), out_dtype),
        scratch_shapes=[pltpu.VMEM((bn, K), BF16)],
        compiler_params=_cparams("parallel", "arbitrary"),
        name="matmul_nt_shifted",
    )(x, wt, wt)


GDN_ROWS = 256
GDN_HB = 8
GDN_PREV = 16


def _gdn_gates_body(ab_ref, alog_ref, dtb_ref, o_ref):
    ab = ab_ref[...]
    lane = lax.broadcasted_iota(jnp.int32, ab.shape, 1)
    g = -jnp.exp(alog_ref[...]) * _softplus(ab + dtb_ref[...])
    o_ref[...] = jnp.where(lane < A_HEADS, g, _sigmoid(ab))


def gdn_gates(p_s, a_log, dt_bias, bm=512):
    S = p_s.shape[0]
    pad = lambda a: jnp.pad(a.astype(F32), (0, LANES - A_HEADS)).reshape(1, LANES)
    vec = pl.BlockSpec((1, LANES), lambda i: (0, 0))
    return pl.pallas_call(
        _gdn_gates_body,
        grid=(S // bm,),
        in_specs=[pl.BlockSpec((bm, LANES), lambda i: (i, PS_AB // LANES)), vec, vec],
        out_specs=pl.BlockSpec((bm, LANES), lambda i: (i, 0)),
        out_shape=jax.ShapeDtypeStruct((S, LANES), F32),
        compiler_params=_cparams("parallel"),
        name="gdn_gates",
    )(p_s, pad(a_log), pad(dt_bias))


def _gdn_body(q_ref, k_ref, v_ref, qp_ref, kp_ref, vp_ref, z_ref, gb_ref, cwq_ref, cwk_ref, cwv_ref,
              nw_ref, o_ref, state_ref):
    hb = pl.program_id(0)
    i = pl.program_id(1)
    R = GDN_ROWS
    C = A_CHUNK
    NC = R // C

    @pl.when(i == 0)
    def _():
        state_ref[...] = jnp.zeros_like(state_ref)

    has_prev = (i > 0).astype(F32)
    lane = lax.broadcasted_iota(jnp.int32, (R, LANES), 1)
    row = lax.broadcasted_iota(jnp.int32, (R, LANES), 0)
    rin = row & (C - 1)
    ri = lax.broadcasted_iota(jnp.int32, (R, R), 0)
    ci = lax.broadcasted_iota(jnp.int32, (R, R), 1)
    same = (ri // C) == (ci // C)
    incl = same & (ri >= ci)
    strict = same & (ri > ci)

    g_all = beta_all = gb_ref[...]
    nw = nw_ref[...]

    def conv_silu(cur_ref, prev_ref, cw_ref, sl):
        cur = cur_ref[:, sl].astype(F32)
        prev = prev_ref[:, sl].astype(F32) * has_prev
        full = jnp.concatenate([prev, cur], axis=0)
        cw = cw_ref[:, sl]
        y = cw[A_CONV - 1:A_CONV, :] * cur
        for s in range(1, A_CONV):
            y = y + cw[A_CONV - 1 - s:A_CONV - s, :] * pltpu.roll(full, s, axis=0)[GDN_PREV:, :]
        return _silu(y)

    heads = range(GDN_HB)
    lanes_of = [slice(hh * LANES, (hh + 1) * LANES) for hh in heads]
    st = [dict() for _ in heads]
    for hh, t in enumerate(st):
        h = hb * GDN_HB + hh
        g = jnp.broadcast_to(jnp.sum(jnp.where(lane == h, g_all, 0.0), axis=1, keepdims=True), (R, LANES))
        t["beta"] = jnp.broadcast_to(
            jnp.sum(jnp.where(lane == h + A_HEADS, beta_all, 0.0), axis=1, keepdims=True), (R, LANES))
        gc = g
        s = 1
        while s < C:
            gc = gc + jnp.where(rin >= s, pltpu.roll(gc, s, axis=0), 0.0)
            s *= 2
        t["g_last"] = jnp.broadcast_to(jnp.sum(g.reshape(NC, C, LANES), axis=1, keepdims=True),
                                       (NC, C, LANES)).reshape(R, LANES)
        t["gc"] = gc
        t["eg"] = jnp.exp(gc)
        gc_t = gc.T
        diff = jnp.concatenate([gc, gc], axis=1) - jnp.concatenate([gc_t, gc_t], axis=0)
        t["decay"] = jnp.exp(jnp.where(incl, diff, NEG))
    for hh, t in enumerate(st):
        sl = lanes_of[hh]
        q = conv_silu(q_ref, qp_ref, cwq_ref, sl)
        k = conv_silu(k_ref, kp_ref, cwk_ref, sl)
        t["v"] = conv_silu(v_ref, vp_ref, cwv_ref, sl)
        t["qn"] = q * lax.rsqrt(jnp.sum(q * q, axis=-1, keepdims=True) + EPS) * (A_DK ** -0.5)
        t["kn"] = k * lax.rsqrt(jnp.sum(k * k, axis=-1, keepdims=True) + EPS)
    for t in st:
        kn, qn, beta, eg, decay = t["kn"], t["qn"], t["beta"], t["eg"], t["decay"]
        kb = kn * beta
        kn_b = kn.astype(BF16)
        lower = jnp.where(strict, _dot_nt(kb.astype(BF16), kn_b) * decay, 0.0)
        t["xp"] = t["qmat"] = -lower
        t["qk"] = (_dot_nt(qn.astype(BF16), kn_b) * decay).astype(BF16)
        t["rhs"] = jnp.concatenate([t["v"] * beta, kb * eg], axis=1)
        t["q_dec"] = (qn * eg).astype(BF16)
        t["k_dec"] = kn * jnp.exp(t["g_last"] - t["gc"])

    for _ in range(int(math.log2(C)) - 1):
        for t in st:
            xb = t["xp"].astype(BF16)
            t["xp"] = _dot(xb, xb)
            t["qmat"] = t["qmat"] + t["xp"] + _dot(t["qmat"].astype(BF16), t["xp"].astype(BF16))
    for t in st:
        sol = t["rhs"] + _dot(t["qmat"].astype(BF16), t["rhs"].astype(BF16))
        t["u"] = sol[:, :A_DV]
        t["w_b"] = sol[:, A_DV:].astype(BF16)
        t["outs"] = []

    states = [state_ref[hh] for hh in heads]
    for c in range(NC):
        rs = slice(c * C, (c + 1) * C)
        for hh, t in enumerate(st):
            sb = states[hh].astype(BF16)
            v_new_b = (t["u"][rs] - _dot(t["w_b"][rs], sb)).astype(BF16)
            t["outs"].append(_dot(t["q_dec"][rs], sb) + _dot(t["qk"][rs, :][:, rs], v_new_b))
            states[hh] = (states[hh] * jnp.exp(t["g_last"][c * C:c * C + 1, :])
                          + _dot(t["k_dec"][rs].T.astype(BF16), v_new_b))
    for hh, t in enumerate(st):
        sl = lanes_of[hh]
        state_ref[hh] = states[hh]
        out = jnp.concatenate(t["outs"], axis=0)
        out = out * lax.rsqrt(jnp.mean(out * out, axis=-1, keepdims=True) + EPS) * nw
        o_ref[:, sl] = (out * _silu(z_ref[:, sl].astype(F32))).astype(o_ref.dtype)


def gdn_branch(p_a, p_s, conv_w, a_log, dt_bias, norm_w):
    S = p_a.shape[0]
    R, W = GDN_ROWS, GDN_HB * LANES
    nq = A_QK_W // W
    gb = gdn_gates(p_s, a_log, dt_bias)

    def cur(off):
        return pl.BlockSpec((R, W), lambda hb, i: (i, off + hb))

    def prev(off):
        return pl.BlockSpec((GDN_PREV, W), lambda hb, i: (jnp.maximum(i * (R // GDN_PREV) - 1, 0), off + hb))

    def cw(off):
        return pl.BlockSpec((A_CONV, W), lambda hb, i: (0, off + hb))

    vec = pl.BlockSpec((1, LANES), lambda hb, i: (0, 0))
    return pl.pallas_call(
        _gdn_body,
        grid=(A_HEADS // GDN_HB, S // R),
        in_specs=[cur(0), cur(nq), cur(2 * nq), prev(0), prev(nq), prev(2 * nq), cur(3 * nq),
                  pl.BlockSpec((R, LANES), lambda hb, i: (i, 0)),
                  cw(0), cw(nq), cw(2 * nq), vec],
        out_specs=pl.BlockSpec((R, W), lambda hb, i: (i, hb)),
        out_shape=jax.ShapeDtypeStruct((S, A_V_W), BF16),
        scratch_shapes=[pltpu.VMEM((GDN_HB, A_DK, A_DV), F32)],
        compiler_params=_cparams("parallel", "arbitrary"),
        name="gdn",
    )(p_a, p_a, p_a, p_a, p_a, p_a, p_a, gb, conv_w, conv_w, conv_w, norm_w.astype(F32).reshape(1, LANES))


MLA_HG = 4


def _mla_prep_body(cq_ref, ckv_ref, kr_ref, qn_ref, kvn_ref, wq_ref, wkv_ref, cs_ref, qo_ref, ko_ref, vo_ref):
    scale = (B_NOPE + B_ROPE) ** -0.5 * math.log2(math.e)
    cq = cq_ref[...]
    cq = (cq * lax.rsqrt(jnp.mean(cq * cq, axis=-1, keepdims=True) + EPS) * qn_ref[...]).astype(BF16)
    ckv = ckv_ref[...]
    ckv = (ckv * lax.rsqrt(jnp.mean(ckv * ckv, axis=-1, keepdims=True) + EPS) * kvn_ref[...]).astype(BF16)
    rq = _dot(cq, wq_ref[...])
    rkv = _dot(ckv, wkv_ref[...])
    cs = cs_ref[...]
    sc = pltpu.roll(cs, B_ROPE, axis=1)
    prod = kr_ref[...] * cs
    lane = lax.broadcasted_iota(jnp.int32, prod.shape, 1)
    k_pe = jnp.where(lane < B_ROPE, prod + pltpu.roll(prod, B_ROPE, axis=1), 0.0).astype(BF16)
    for hh in range(MLA_HG):
        qb = hh * B_QH
        roped = rq[:, qb + LANES:qb + 2 * LANES] * cs + rq[:, qb + 2 * LANES:qb + 3 * LANES] * sc
        qo_ref[:, hh * B_KH:hh * B_KH + LANES] = (rq[:, qb:qb + LANES] * scale).astype(BF16)
        qo_ref[:, hh * B_KH + LANES:(hh + 1) * B_KH] = (roped * scale).astype(BF16)
        ko_ref[:, hh * B_KH:hh * B_KH + LANES] = rkv[:, hh * 2 * LANES:hh * 2 * LANES + LANES].astype(BF16)
        ko_ref[:, hh * B_KH + LANES:(hh + 1) * B_KH] = k_pe
        vo_ref[:, hh * 2 * LANES:hh * 2 * LANES + LANES] = (
            rkv[:, hh * 2 * LANES + LANES:(hh + 1) * 2 * LANES].astype(BF16))
        vo_ref[:, hh * 2 * LANES + LANES:(hh + 1) * 2 * LANES] = jnp.ones((rkv.shape[0], LANES), BF16)


def mla_prep(p_s, q_norm, kv_norm, wq_ext, wkv, cs_b, bm=512):
    S = p_s.shape[0]
    HG = MLA_HG
    return pl.pallas_call(
        _mla_prep_body,
        grid=(B_HEADS // HG, S // bm),
        in_specs=[
            pl.BlockSpec((bm, B_Q_LORA), lambda g, i: (i, PS_CQ // B_Q_LORA)),
            pl.BlockSpec((bm, B_KV_LORA), lambda g, i: (i, PS_CKV // B_KV_LORA)),
            pl.BlockSpec((bm, LANES), lambda g, i: (i, PS_KR // LANES)),
            pl.BlockSpec((1, B_Q_LORA), lambda g, i: (0, 0)),
            pl.BlockSpec((1, B_KV_LORA), lambda g, i: (0, 0)),
            pl.BlockSpec((B_Q_LORA, HG * B_QH), lambda g, i: (0, g)),
            pl.BlockSpec((B_KV_LORA, HG * 2 * LANES), lambda g, i: (0, g)),
            pl.BlockSpec((bm, LANES), lambda g, i: (i, 0)),
        ],
        out_specs=[
            pl.BlockSpec((bm, HG * B_KH), lambda g, i: (i, g)),
            pl.BlockSpec((bm, HG * B_KH), lambda g, i: (i, g)),
            pl.BlockSpec((bm, HG * 2 * LANES), lambda g, i: (i, g)),
        ],
        out_shape=[
            jax.ShapeDtypeStruct((S, B_HEADS * B_KH), BF16),
            jax.ShapeDtypeStruct((S, B_HEADS * B_KH), BF16),
            jax.ShapeDtypeStruct((S, B_HEADS * 2 * B_VDIM), BF16),
        ],
        compiler_params=_cparams("parallel", "parallel"),
        name="mla_prep",
    )(p_s, p_s, p_s, q_norm.reshape(1, -1), kv_norm.reshape(1, -1), wq_ext, wkv, cs_b)


FLASH_BLK = 2048
FLASH_RC = 256


def _flash_body(qi_ref, kj_ref, q_ref, k_ref, v_ref, o_ref, m_ref, acc_ref):
    t = pl.program_id(1)
    qi = qi_ref[t]
    kj = kj_ref[t]

    @pl.when(kj == 0)
    def _():
        m_ref[...] = jnp.full_like(m_ref, NEG)
        acc_ref[...] = jnp.zeros_like(acc_ref)

    def step(diag):
        nc = FLASH_BLK // FLASH_RC
        rows_of = [slice(c * FLASH_RC, (c + 1) * FLASH_RC) for c in range(nc)]
        ncol_of = [(c + 1) * FLASH_RC if diag else FLASH_BLK for c in range(nc)]

        def scores(c):
            s = _dot_nt(q_ref[rows_of[c], :], k_ref[:ncol_of[c], :])
            if diag:
                r = lax.broadcasted_iota(jnp.int32, s.shape, 0) + c * FLASH_RC
                col = lax.broadcasted_iota(jnp.int32, s.shape, 1)
                s = jnp.where(col <= r, s, NEG)
            return s

        s_next = scores(0)
        for c in range(nc):
            s = s_next
            if c + 1 < nc:
                s_next = scores(c + 1)
            rows = rows_of[c]
            m_prev = m_ref[rows, :]
            m_new = jnp.maximum(m_prev, jnp.max(s, axis=1, keepdims=True))
            p = jnp.exp2(s - m_new).astype(BF16)
            acc_ref[rows, :] = jnp.exp2(m_prev - m_new) * acc_ref[rows, :] + _dot(p, v_ref[:ncol_of[c], :])
            m_ref[rows, :] = m_new

    @pl.when(kj < qi)
    def _():
        step(False)

    @pl.when(kj == qi)
    def _():
        step(True)
        acc = acc_ref[...]
        o_ref[...] = (acc[:, :B_VDIM] / acc[:, B_VDIM:]).astype(o_ref.dtype)


def mla_attention(q_cat, k_cat, v_ext):
    S = q_cat.shape[0]
    blk = FLASH_BLK
    n = S // blk
    pairs = [(qi, kj) for qi in range(n) for kj in range(qi + 1)]
    qi_tab = jnp.asarray([p[0] for p in pairs], jnp.int32)
    kj_tab = jnp.asarray([p[1] for p in pairs], jnp.int32)
    return pl.pallas_call(
        _flash_body,
        grid_spec=pltpu.PrefetchScalarGridSpec(
            num_scalar_prefetch=2,
            grid=(B_HEADS, len(pairs)),
            in_specs=[
                pl.BlockSpec((blk, B_KH), lambda h, t, qi, kj: (qi[t], h)),
                pl.BlockSpec((blk, B_KH), lambda h, t, qi, kj: (kj[t], h)),
                pl.BlockSpec((blk, 2 * B_VDIM), lambda h, t, qi, kj: (kj[t], h)),
            ],
            out_specs=pl.BlockSpec((blk, B_VDIM), lambda h, t, qi, kj: (qi[t], h)),
            scratch_shapes=[pltpu.VMEM((blk, 1), F32), pltpu.VMEM((blk, 2 * B_VDIM), F32)],
        ),
        out_shape=jax.ShapeDtypeStruct((S, B_HEADS * B_VDIM), BF16),
        compiler_params=_cparams("parallel", "arbitrary"),
        name="mla_flash",
    )(qi_tab, kj_tab, q_cat, k_cat, v_ext)


DIL_BLK = 128
DIL_ROWS = 256
DILATIONS = tuple(d for _, d in C_PATTERNS)


def _rope_split_body(q_ref, k_ref, v_ref, c_ref, s_ref, *refs):
    nd = len(DILATIONS)
    outs, (sq, sk, sv) = refs[:3 * nd], refs[3 * nd:]
    c = c_ref[...]
    s = s_ref[...]
    scale = C_DH ** -0.5
    for h in range(C_HEADS):
        sl = slice(h * C_DH, (h + 1) * C_DH)
        x = q_ref[:, sl].astype(F32)
        sq[h] = (x * c + pltpu.roll(x, C_DH // 2, axis=1) * s) * scale
        x = k_ref[:, sl].astype(F32)
        sk[h] = x * c + pltpu.roll(x, C_DH // 2, axis=1) * s
        sv[h] = v_ref[:, sl].astype(F32)
        for di, d in enumerate(DILATIONS):
            for src, dst in zip((sq, sk, sv), outs[3 * di:3 * di + 3]):
                for r in range(d):
                    rows = src[h] if d == 1 else src[h, pl.ds(r, DIL_ROWS // d, stride=d), :]
                    dst[r, :, sl] = rows.astype(BF16)


def rope_split(p_r, cos_c, sin_c):
    S = p_r.shape[0]
    bm = DIL_ROWS
    tab = pl.BlockSpec((bm, C_DH), lambda i: (i, 0))
    col = lambda j: pl.BlockSpec((bm, C_W), lambda i: (i, j))
    out_specs, out_shape = [], []
    for d in DILATIONS:
        out_specs += [pl.BlockSpec((d, bm // d, C_W), lambda i: (0, i, 0))] * 3
        out_shape += [jax.ShapeDtypeStruct((d, S // d, C_W), BF16)] * 3
    outs = pl.pallas_call(
        _rope_split_body,
        grid=(S // bm,),
        in_specs=[col(0), col(1), col(2), tab, tab],
        out_specs=out_specs,
        out_shape=out_shape,
        scratch_shapes=[pltpu.VMEM((C_HEADS, bm, C_DH), F32)] * 3,
        compiler_params=_cparams("parallel"),
        name="rope_split",
    )(p_r, p_r, p_r, cos_c, sin_c)
    return [outs[3 * i:3 * i + 3] for i in range(len(DILATIONS))]


def _dilated_body(q_ref, kc_ref, kp_ref, vc_ref, vp_ref, o_ref, lse_ref):
    B = DIL_BLK
    n = pl.program_id(1)
    qi = lax.broadcasted_iota(jnp.int32, (B, 2 * B), 0)
    kj = lax.broadcasted_iota(jnp.int32, (B, 2 * B), 1)
    dist = qi + B - kj
    valid = (dist >= 0) & (dist <= B) & ((kj >= B) | (n > 0))
    lanes_of = [slice(h * C_DH, (h + 1) * C_DH) for h in range(C_HEADS)]
    scores = [jnp.where(valid, _dot_nt(q_ref[:, sl], jnp.concatenate([kp_ref[:, sl], kc_ref[:, sl]], axis=0)), NEG)
              for sl in lanes_of]
    probs = []
    lane = lax.broadcasted_iota(jnp.int32, (B, LANES), 1)
    lse = jnp.zeros((B, LANES), F32)
    for h, s in enumerate(scores):
        m = jnp.max(s, axis=1, keepdims=True)
        p = jnp.exp(s - m)
        l = jnp.sum(p, axis=1, keepdims=True)
        lse = jnp.where(lane == h, m + jnp.log(l), lse)
        probs.append((p.astype(BF16), l))
    lse_ref[...] = lse
    for sl, (p, l) in zip(lanes_of, probs):
        vv = jnp.concatenate([vp_ref[:, sl], vc_ref[:, sl]], axis=0)
        o_ref[:, sl] = (_dot(p, vv) / l).astype(o_ref.dtype)


def dilated_pattern(q, k, v):
    d, L, W = q.shape
    B = DIL_BLK
    cur = pl.BlockSpec((None, B, W), lambda r, n: (r, n, 0))
    prv = pl.BlockSpec((None, B, W), lambda r, n: (r, jnp.maximum(n - 1, 0), 0))
    return pl.pallas_call(
        _dilated_body,
        grid=(d, L // B),
        in_specs=[cur, cur, prv, cur, prv],
        out_specs=[cur, pl.BlockSpec((None, B, LANES), lambda r, n: (r, n, 0))],
        out_shape=[jax.ShapeDtypeStruct((d, L, W), BF16), jax.ShapeDtypeStruct((d, L, LANES), F32)],
        compiler_params=_cparams("parallel", "parallel"),
        name=f"dilated_d{d}",
    )(q, k, k, v, v)


def _dilated_merge_body(*refs):
    nd = len(DILATIONS)
    o_refs, lse_refs, out_ref = refs[:nd], refs[nd:2 * nd], refs[2 * nd]
    so, sl = refs[2 * nd + 1:]
    rows_of = lambda d, r: slice(None) if d == 1 else pl.ds(r, DIL_ROWS // d, stride=d)
    for di, d in enumerate(DILATIONS):
        for r in range(d):
            sl[di, rows_of(d, r), :] = lse_refs[di][r]
    top = sl[0]
    for di in range(1, nd):
        top = jnp.maximum(top, sl[di])
    weights = [jnp.exp(sl[di] - top) for di in range(nd)]
    lane = lax.broadcasted_iota(jnp.int32, top.shape, 1)
    for h in range(C_HEADS):
        cols = slice(h * C_DH, (h + 1) * C_DH)
        for di, d in enumerate(DILATIONS):
            for r in range(d):
                so[di, rows_of(d, r), :] = o_refs[di][r, :, cols].astype(F32)
        num = 0.0
        den = 0.0
        for di in range(nd):
            w = jnp.sum(jnp.where(lane == h, weights[di], 0.0), axis=1, keepdims=True)
            num = num + w * so[di]
            den = den + w
        out_ref[:, cols] = (num / den).astype(out_ref.dtype)


def dilated_merge(outs, lses):
    S = outs[0].shape[0] * outs[0].shape[1]
    bm = DIL_ROWS
    nd = len(DILATIONS)
    specs = [pl.BlockSpec((d, bm // d, C_W), lambda i: (0, i, 0)) for d in DILATIONS]
    lse_specs = [pl.BlockSpec((d, bm // d, LANES), lambda i: (0, i, 0)) for d in DILATIONS]
    return pl.pallas_call(
        _dilated_merge_body,
        grid=(S // bm,),
        in_specs=specs + lse_specs,
        out_specs=pl.BlockSpec((bm, C_W), lambda i: (i, 0)),
        out_shape=jax.ShapeDtypeStruct((S, C_W), BF16),
        scratch_shapes=[pltpu.VMEM((nd, bm, C_DH), F32)] * 2,
        compiler_params=_cparams("parallel"),
        name="dilated_merge",
    )(*outs, *lses)


def dilated_branch(p_r, cos_c, sin_c):
    assert all(w // d == DIL_BLK for w, d in C_PATTERNS)
    results = [dilated_pattern(q, k, v) for q, k, v in rope_split(p_r, cos_c, sin_c)]
    return dilated_merge([r[0] for r in results], [r[1] for r in results])


def _merge_body(oa_ref, ob_ref, oc_ref, ga_ref, gb_ref, gc_ref, wa_ref, wb_ref, wc_ref, o_ref, wa_s, wb_s, wc_s):
    @pl.when(pl.program_id(1) == 0)
    def _():
        wa_s[...] = wa_ref[...].astype(BF16)
        wb_s[...] = wb_ref[...].astype(BF16)
        wc_s[...] = wc_ref[...].astype(BF16)

    gates = [_sigmoid(g[...].astype(F32)) for g in (ga_ref, gb_ref, gc_ref)]
    ya = _dot(oa_ref[...], wa_s[...])
    yb = _dot(ob_ref[...], wb_s[...])
    yc = _dot(oc_ref[...], wc_s[...])
    o_ref[...] = (gates[0] * ya + gates[1] * yb + gates[2] * yc).astype(o_ref.dtype)


def merge_branches(o_a, o_b, o_c, p_r, w_a, w_b, w_c, layer, bm=512, bn=512):
    S, K = o_a.shape
    g0 = 3 * C_W // bn
    gstep = D_MODEL // bn
    act = pl.BlockSpec((bm, K), lambda j, i: (i, 0))
    wsp = pl.BlockSpec((None, K, bn), lambda j, i: (layer, 0, j))
    gate = lambda b: pl.BlockSpec((bm, bn), lambda j, i: (i, g0 + b * gstep + j))
    return pl.pallas_call(
        _merge_body,
        grid=(D_MODEL // bn, S // bm),
        in_specs=[act, act, act, gate(0), gate(1), gate(2), wsp, wsp, wsp],
        out_specs=pl.BlockSpec((bm, bn), lambda j, i: (i, j)),
        out_shape=jax.ShapeDtypeStruct((S, D_MODEL), BF16),
        scratch_shapes=[pltpu.VMEM((K, bn), BF16)] * 3,
        compiler_params=_cparams("parallel", "arbitrary"),
        name="merge_branches",
    )(o_a, o_b, o_c, p_r, p_r, p_r, w_a, w_b, w_c)


def _new_group(tg_ref):
    i = pl.program_id(1)
    return (i == 0) | (tg_ref[i] != tg_ref[jnp.maximum(i - 1, 0)])


def _swiglu_up_body(tg_ref, nt_ref, x_ref, w1_ref, w3_ref, o_ref, w1_s, w3_s):
    @pl.when(_new_group(tg_ref))
    def _():
        w1_s[...] = w1_ref[...].astype(BF16)
        w3_s[...] = w3_ref[...].astype(BF16)

    @pl.when(pl.program_id(1) < nt_ref[0])
    def _():
        x = x_ref[...]
        o_ref[...] = (_silu(_dot(x, w1_s[...])) * _dot(x, w3_s[...])).astype(o_ref.dtype)

    @pl.when(pl.program_id(1) >= nt_ref[0])
    def _():
        o_ref[...] = jnp.zeros_like(o_ref)


def swiglu_up(x, w1, w3, tile_group, n_tiles, bm, bf):
    R, D = x.shape
    F = w1.shape[2]
    wspec = pl.BlockSpec((None, D, bf), lambda f, i, tg, nt: (tg[i], 0, f))
    return pl.pallas_call(
        _swiglu_up_body,
        grid_spec=pltpu.PrefetchScalarGridSpec(
            num_scalar_prefetch=2,
            grid=(F // bf, R // bm),
            in_specs=[pl.BlockSpec((bm, D), lambda f, i, tg, nt: (jnp.minimum(i, nt[0] - 1), 0)), wspec, wspec],
            out_specs=pl.BlockSpec((bm, bf), lambda f, i, tg, nt: (i, f)),
            scratch_shapes=[pltpu.VMEM((D, bf), BF16)] * 2,
        ),
        out_shape=jax.ShapeDtypeStruct((R, F), BF16),
        compiler_params=_cparams("arbitrary", "arbitrary"),
        name="swiglu_up",
    )(tile_group, n_tiles, x, w1, w3)


def _grouped_mm_body(tg_ref, nt_ref, x_ref, w_ref, *rest, has_acc):
    if has_acc:
        acc_ref, o_ref, w_s = rest
    else:
        o_ref, w_s = rest

    @pl.when(_new_group(tg_ref))
    def _():
        w_s[...] = w_ref[...].astype(BF16)

    @pl.when(pl.program_id(1) < nt_ref[0])
    def _():
        y = _dot(x_ref[...], w_s[...])
        o_ref[...] = acc_ref[...] + y if has_acc else y

    @pl.when(pl.program_id(1) >= nt_ref[0])
    def _():
        o_ref[...] = jnp.zeros_like(o_ref)


def grouped_matmul(x, w, tile_group, n_tiles, bm, bn, k_blocks=1, k_index=0, acc=None):
    R, K = x.shape
    N = w.shape[2]
    kb = K // k_blocks
    in_specs = [pl.BlockSpec((bm, kb), lambda j, i, tg, nt: (jnp.minimum(i, nt[0] - 1), k_index)),
                pl.BlockSpec((None, kb, bn), lambda j, i, tg, nt: (tg[i], k_index, j))]
    out_spec = pl.BlockSpec((bm, bn), lambda j, i, tg, nt: (i, j))
    args = [tile_group, n_tiles, x, w]
    if acc is not None:
        in_specs.append(out_spec)
        args.append(acc)
    return pl.pallas_call(
        functools.partial(_grouped_mm_body, has_acc=acc is not None),
        grid_spec=pltpu.PrefetchScalarGridSpec(
            num_scalar_prefetch=2,
            grid=(N // bn, R // bm),
            in_specs=in_specs,
            out_specs=out_spec,
            scratch_shapes=[pltpu.VMEM((kb, bn), BF16)],
        ),
        out_shape=jax.ShapeDtypeStruct((R, N), F32),
        input_output_aliases={4: 0} if acc is not None else {},
        compiler_params=_cparams("arbitrary", "arbitrary"),
        name="grouped_matmul",
    )(*args)


DENSE_UP_BM = 1024
DENSE_DOWN_BM = 512
DENSE_KSPLIT = 2


def _one_group(n_rows, bm, group):
    return jnp.full((n_rows // bm,), group, jnp.int32), jnp.full((1,), n_rows // bm, jnp.int32)


def dense_ffn(h, w1, w3, w2, index, x_res):
    S = h.shape[0]
    g = swiglu_up(h, w1, w3, *_one_group(S, DENSE_UP_BM, index), bm=DENSE_UP_BM, bf=256)
    y = x_res
    for k in range(DENSE_KSPLIT):
        y = grouped_matmul(g, w2, *_one_group(S, DENSE_DOWN_BM, index), bm=DENSE_DOWN_BM, bn=512,
                           k_blocks=DENSE_KSPLIT, k_index=k, acc=y)
    return y


def _router_body(h_ref, rw_ref, sel_ref, gate_ref):
    logits = _dot(h_ref[...].astype(BF16), rw_ref[...].astype(BF16))
    lane = lax.broadcasted_iota(jnp.int32, logits.shape, 1)
    logits = jnp.where(lane < N_EXPERTS, logits, NEG)
    t1 = jnp.max(logits, axis=1, keepdims=True)
    i1 = jnp.min(jnp.where(logits == t1, lane, LANES), axis=1, keepdims=True)
    rest = jnp.where(lane == i1, NEG, logits)
    t2 = jnp.max(rest, axis=1, keepdims=True)
    i2 = jnp.min(jnp.where(rest == t2, lane, LANES), axis=1, keepdims=True)
    e2 = jnp.exp(t2 - t1)
    g1 = 1.0 / (1.0 + e2)
    sel_ref[...] = jnp.where(lane == 0, i1, jnp.where(lane == 1, i2, 0))
    gate_ref[...] = jnp.where(lane == 0, g1, jnp.where(lane == 1, e2 * g1, 0.0))


def router(h, router_w, bm=512):
    S, D = h.shape
    rw = jnp.pad(router_w, ((0, 0), (0, LANES - N_EXPERTS)))
    out = pl.BlockSpec((bm, LANES), lambda i: (i, 0))
    return pl.pallas_call(
        _router_body,
        grid=(S // bm,),
        in_specs=[pl.BlockSpec((bm, D), lambda i: (i, 0)), pl.BlockSpec((D, LANES), lambda i: (0, 0))],
        out_specs=[out, out],
        out_shape=[jax.ShapeDtypeStruct((S, LANES), jnp.int32), jax.ShapeDtypeStruct((S, LANES), F32)],
        compiler_params=_cparams("parallel"),
        name="router",
    )(h, rw)


MOE_BM = 512
MOE_BG = 256
DMA_QUEUES = 2


def _moe_plan(sel, bm):
    S = sel.shape[0]
    E = N_EXPERTS
    e = sel[:, :TOP_K].reshape(-1)
    onehot = (e[:, None] == jnp.arange(E, dtype=jnp.int32)[None, :]).astype(jnp.int32)
    csum = jnp.cumsum(onehot, axis=0)
    rank = jnp.sum((csum - 1) * onehot, axis=1)
    tiles_per = (csum[-1] + bm - 1) // bm
    tile_end = jnp.cumsum(tiles_per)
    pos = ((tile_end - tiles_per)[e] * bm + rank).astype(jnp.int32)
    n_tiles = tile_end[-1:].astype(jnp.int32)
    T = (TOP_K * S) // bm + E
    tid = jnp.arange(T, dtype=jnp.int32)
    te = jnp.minimum(jnp.sum((tid[:, None] >= tile_end[None, :]).astype(jnp.int32), axis=1), E - 1)
    te = jnp.where(tid < n_tiles[0], te, te[n_tiles[0] - 1]).astype(jnp.int32)
    src = jnp.zeros((T * bm,), jnp.int32).at[pos].set(jnp.arange(TOP_K * S, dtype=jnp.int32) // TOP_K)
    return pos, src, te, n_tiles


def _moe_gather_body(src_ref, slab_hbm, o_ref, buf_ref, sem):
    base = pl.program_id(0) * MOE_BG
    tiles = o_ref.shape[1] // LANES
    pitch = _slab_pitch(o_ref.shape[1])

    def row_copy(j):
        src_rows = pl.ds(src_ref[base + j] * pitch, tiles)
        return pltpu.make_async_copy(slab_hbm.at[src_rows, :], buf_ref.at[pl.ds(j * pitch, tiles), :], sem)

    def start(jj, c):
        for par in range(DMA_QUEUES):
            row_copy(jj * DMA_QUEUES + par).start(priority=par)
        return c

    def wait(j, c):
        row_copy(j).wait()
        return c

    lax.fori_loop(0, MOE_BG // DMA_QUEUES, start, 0)
    lax.fori_loop(0, MOE_BG, wait, 0)
    for c in range(tiles):
        o_ref[:, c * LANES:(c + 1) * LANES] = buf_ref[pl.ds(c, MOE_BG, stride=pitch), :].astype(o_ref.dtype)


def moe_gather(slab, src, d_model):
    R = src.shape[0]
    return pl.pallas_call(
        _moe_gather_body,
        grid_spec=pltpu.PrefetchScalarGridSpec(
            num_scalar_prefetch=1,
            grid=(R // MOE_BG,),
            in_specs=[pl.BlockSpec(memory_space=pl.ANY)],
            out_specs=pl.BlockSpec((MOE_BG, d_model), lambda i, src: (i, 0)),
            scratch_shapes=[pltpu.VMEM((MOE_BG * _slab_pitch(d_model), LANES), F32), pltpu.SemaphoreType.DMA(())],
        ),
        out_shape=jax.ShapeDtypeStruct((R, d_model), BF16),
        compiler_params=_cparams("arbitrary"),
        name="moe_gather",
    )(src, slab)


def _moe_combine_body(pos_ref, gate_ref, ys_hbm, o_ref, buf_ref, sem):
    base = pl.program_id(0) * MOE_BG

    def row_copy(j, k):
        row = pos_ref[(base + j) * TOP_K + k]
        return pltpu.make_async_copy(ys_hbm.at[pl.ds(row, 1), :], buf_ref.at[k, pl.ds(j, 1), :], sem)

    def start(j, c):
        for k in range(TOP_K):
            row_copy(j, k).start(priority=k % DMA_QUEUES)
        return c

    def wait(j, c):
        for k in range(TOP_K):
            row_copy(j, k).wait()
        return c

    lax.fori_loop(0, MOE_BG, start, 0)
    lax.fori_loop(0, MOE_BG, wait, 0)
    gate = gate_ref[...]
    acc = gate[:, 0:1] * buf_ref[0]
    for k in range(1, TOP_K):
        acc = acc + gate[:, k:k + 1] * buf_ref[k]
    o_ref[...] = acc


def moe_combine(ys, pos, gate):
    S = gate.shape[0]
    D = ys.shape[1]
    return pl.pallas_call(
        _moe_combine_body,
        grid_spec=pltpu.PrefetchScalarGridSpec(
            num_scalar_prefetch=1,
            grid=(S // MOE_BG,),
            in_specs=[pl.BlockSpec((MOE_BG, LANES), lambda i, pos: (i, 0)), pl.BlockSpec(memory_space=pl.ANY)],
            out_specs=pl.BlockSpec((MOE_BG, D), lambda i, pos: (i, 0)),
            scratch_shapes=[pltpu.VMEM((TOP_K, MOE_BG, D), F32), pltpu.SemaphoreType.DMA(())],
        ),
        out_shape=jax.ShapeDtypeStruct((S, D), F32),
        compiler_params=_cparams("arbitrary"),
        name="moe_combine",
    )(pos, gate, ys)


def moe_layer(h, h_slab, router_w, w1, w3, w2, index):
    sel, gate = router(h, router_w)
    pos, src, tile_expert, n_tiles = _moe_plan(sel, MOE_BM)
    tile_expert = tile_expert + index * N_EXPERTS
    w1, w3, w2 = (w.reshape((-1,) + w.shape[2:]) for w in (w1, w3, w2))
    xs = moe_gather(h_slab, src, h.shape[1])
    g = swiglu_up(xs, w1, w3, tile_expert, n_tiles, bm=MOE_BM, bf=512)
    ys = grouped_matmul(g, w2, tile_expert, n_tiles, bm=MOE_BM, bn=1024)
    return moe_combine(ys, pos, gate)


def _rot_cols(w):
    half = w.shape[-1] // 2
    return jnp.concatenate([-w[..., half:], w[..., :half]], axis=-1)


def _rope_tables(seq, dim):
    inv = ROPE_THETA ** (-jnp.arange(0, dim, 2, dtype=F32) / dim)
    ang = jnp.arange(seq, dtype=F32)[:, None] * inv[None, :]
    return jnp.cos(ang), jnp.sin(ang)


def _small_weight_t(wt_mid):
    row = lambda off, n: wt_mid[off - OFF_A_IN:off - OFF_A_IN + n]
    kr = row(OFF_KROPE, B_ROPE)
    parts = [row(OFF_CQ, B_Q_LORA), row(OFF_CKV, B_KV_LORA), kr, _rot_cols(kr.T).T,
             row(OFF_A_IN, A_HEADS), row(OFF_B_IN, A_HEADS)]
    w = jnp.concatenate(parts, axis=0)
    return jnp.pad(w, ((0, PS_W - w.shape[0]), (0, 0)))


def _mla_weights(w_uq, w_ukv):
    wq = w_uq.reshape(B_Q_LORA, B_HEADS, B_NOPE + B_ROPE)
    pe = wq[..., B_NOPE:]
    zeros = jnp.zeros_like(pe)
    wq_ext = jnp.concatenate([wq[..., :B_NOPE], pe, zeros, _rot_cols(pe), zeros], axis=-1)
    return wq_ext.reshape(B_Q_LORA, B_HEADS * B_QH).astype(BF16), w_ukv.astype(BF16)


def hybrid_mixer(h, x_res, layer, p, tables):
    cs_b, cos_c, sin_c = tables
    wt_in = jnp.swapaxes(p["w_in"], 1, 2)
    p_a = matmul_nt(h, wt_in, layer, OFF_A_IN, BF16, bm=1024, bn=512)
    p_s = matmul_nt(h, _small_weight_t(wt_in[layer, OFF_A_IN:OFF_REST])[None], 0, PS_W, F32, bm=1024, bn=256)
    p_r = matmul_nt_shifted(h, wt_in, layer, OFF_REST, REST_W, BF16, bm=1024, bn=512)
    o_a = gdn_branch(p_a, p_s, p["conv_w"][layer], p["gdn_a_log"][layer], p["gdn_dt_bias"][layer],
                     p["gdn_norm"][layer])
    wq_ext, wkv = _mla_weights(p["mla_w_uq"][layer], p["mla_w_ukv"][layer])
    q_cat, k_cat, v_b = mla_prep(p_s, p["mla_q_norm"][layer], p["mla_kv_norm"][layer], wq_ext, wkv, cs_b)
    o_b = mla_attention(q_cat, k_cat, v_b)
    o_c = dilated_branch(p_r, cos_c, sin_c)
    merged = merge_branches(o_a, o_b, o_c, p_r, p["w_branch_a"], p["w_branch_b"], p["w_branch_c"], layer)
    return matmul_acc(merged, p["w_out"], layer, x_res, bm=1024, bn=512)


def kernel(x, attn_norm, w_in, conv_w, gdn_a_log, gdn_dt_bias, gdn_norm, mla_q_norm, mla_w_uq, mla_kv_norm, mla_w_ukv, w_branch_a, w_branch_b, w_branch_c, w_out, ffn_norm, dense_w1, dense_w3, dense_w2, router_w, moe_w1, moe_w3, moe_w2, final_norm):
    p = dict(w_in=w_in, conv_w=conv_w, gdn_a_log=gdn_a_log, gdn_dt_bias=gdn_dt_bias, gdn_norm=gdn_norm,
             mla_q_norm=mla_q_norm, mla_w_uq=mla_w_uq, mla_kv_norm=mla_kv_norm, mla_w_ukv=mla_w_ukv,
             w_branch_a=w_branch_a, w_branch_b=w_branch_b, w_branch_c=w_branch_c, w_out=w_out)
    Bn, S, D = x.shape
    assert Bn == 1 and D == D_MODEL
    cos_b, sin_b = _rope_tables(S, B_ROPE)
    cos_c, sin_c = _rope_tables(S, C_DH)
    tables = (jnp.concatenate([cos_b, cos_b, sin_b, sin_b], axis=1),
              jnp.concatenate([cos_c, cos_c], axis=1), jnp.concatenate([-sin_c, sin_c], axis=1))
    xr = x.reshape(S, D)
    h = rmsnorm(xr, attn_norm[0], BF16)
    out = None
    for layer in range(DEPTH):
        last = layer + 1 == DEPTH
        xr = hybrid_mixer(h, xr, layer, p, tables)
        i = layer // 2
        if layer % 2 == 0:
            h = rmsnorm(xr, ffn_norm[layer], BF16)
            xr = dense_ffn(h, dense_w1, dense_w3, dense_w2, i, xr)
            if last:
                out = rmsnorm(xr, final_norm, F32)
            else:
                h = rmsnorm(xr, attn_norm[layer + 1], BF16)
        else:
            h, h_slab = rmsnorm_slab(xr, ffn_norm[layer])
            y = moe_layer(h, h_slab, router_w[i], moe_w1, moe_w3, moe_w2, i)
            if last:
                out = add_rmsnorm_final(xr, y, final_norm)
            else:
                xr, h = add_rmsnorm(xr, y, attn_norm[layer + 1], BF16)
    return out.reshape(Bn, S, D)
```

```python
import functools
import math

import jax
import jax.numpy as jnp
from jax import lax
from jax.experimental import pallas as pl
from jax.experimental.pallas import tpu as pltpu

F32 = jnp.float32
BF16 = jnp.bfloat16

EPS = 1e-6
ROPE_THETA = 10000.0
NEG = -1e30

LANES = 128
VMEM_LIMIT_BYTES = 56 * 1024 * 1024

D_MODEL = 4096
DEPTH = 2
A_HEADS, A_DK, A_DV, A_CONV, A_CHUNK = 16, 128, 128, 4, 64
B_HEADS, B_Q_LORA, B_KV_LORA, B_NOPE, B_ROPE, B_VDIM = 16, 1024, 512, 128, 64, 128
C_HEADS, C_DH = 16, 128
C_PATTERNS = ((128, 1), (512, 4), (2048, 16))
D_FF = 11008
N_EXPERTS, TOP_K, D_EXPERT = 8, 2, 4096

A_QK_W = A_HEADS * A_DK
A_V_W = A_HEADS * A_DV
A_QKV_W = 2 * A_QK_W + A_V_W
C_W = C_HEADS * C_DH
OFF_A_IN = A_QKV_W + A_V_W
OFF_B_IN = OFF_A_IN + A_HEADS
OFF_CQ = OFF_B_IN + A_HEADS
OFF_CKV = OFF_CQ + B_Q_LORA
OFF_KROPE = OFF_CKV + B_KV_LORA
OFF_REST = OFF_KROPE + B_ROPE
REST_W = 3 * C_W + 3 * D_MODEL
PS_CQ, PS_CKV, PS_KR, PS_AB, PS_W = 0, 1024, 1536, 1664, 1792

B_QH = 256
B_KH = 256


def _cparams(*sem):
    return pltpu.CompilerParams(dimension_semantics=sem, vmem_limit_bytes=VMEM_LIMIT_BYTES)


def _dot(a, b):
    return jnp.dot(a, b, preferred_element_type=F32)


def _dot_nt(a, b):
    return lax.dot_general(a, b, (((1,), (1,)), ((), ())), preferred_element_type=F32)


def _sigmoid(x):
    return 1.0 / (1.0 + jnp.exp(-x))


def _silu(x):
    return x * _sigmoid(x)


def _softplus(x):
    return jnp.maximum(x, 0.0) + jnp.log(1.0 + jnp.exp(-jnp.abs(x)))


def _rmsnorm_body(x_ref, g_ref, o_ref):
    x = x_ref[...]
    y = x * lax.rsqrt(jnp.mean(x * x, axis=-1, keepdims=True) + EPS)
    o_ref[...] = (y * g_ref[...]).astype(o_ref.dtype)


def rmsnorm(x, g, out_dtype, bm=256):
    M, D = x.shape
    return pl.pallas_call(
        _rmsnorm_body,
        grid=(M // bm,),
        in_specs=[pl.BlockSpec((bm, D), lambda i: (i, 0)), pl.BlockSpec((1, D), lambda i: (0, 0))],
        out_specs=pl.BlockSpec((bm, D), lambda i: (i, 0)),
        out_shape=jax.ShapeDtypeStruct((M, D), out_dtype),
        compiler_params=_cparams("parallel"),
        name="rmsnorm",
    )(x, g.reshape(1, D))


def _rmsnorm_slab_body(x_ref, g_ref, ho_ref, slab_ref):
    x = x_ref[...]
    y = x * lax.rsqrt(jnp.mean(x * x, axis=-1, keepdims=True) + EPS) * g_ref[...]
    ho_ref[...] = y.astype(ho_ref.dtype)
    bm, D = x.shape
    pitch = _slab_pitch(D)
    slab_ref[...] = jnp.zeros_like(slab_ref)
    for c in range(D // LANES):
        slab_ref[pl.ds(c, bm, stride=pitch), :] = y[:, c * LANES:(c + 1) * LANES]


def _slab_pitch(d_model):
    return d_model // LANES + 1


def rmsnorm_slab(x, g, bm=256):
    M, D = x.shape
    pitch = _slab_pitch(D)
    return pl.pallas_call(
        _rmsnorm_slab_body,
        grid=(M // bm,),
        in_specs=[pl.BlockSpec((bm, D), lambda i: (i, 0)), pl.BlockSpec((1, D), lambda i: (0, 0))],
        out_specs=[pl.BlockSpec((bm, D), lambda i: (i, 0)), pl.BlockSpec((bm * pitch, LANES), lambda i: (i, 0))],
        out_shape=[jax.ShapeDtypeStruct((M, D), BF16), jax.ShapeDtypeStruct((M * pitch, LANES), F32)],
        compiler_params=_cparams("parallel"),
        name="rmsnorm_slab",
    )(x, g.reshape(1, D))


def _add_rmsnorm_body(x_ref, y_ref, g_ref, xo_ref, ho_ref):
    x = x_ref[...] + y_ref[...].astype(F32)
    xo_ref[...] = x
    y = x * lax.rsqrt(jnp.mean(x * x, axis=-1, keepdims=True) + EPS)
    ho_ref[...] = (y * g_ref[...]).astype(ho_ref.dtype)


def add_rmsnorm(x, y, g, out_dtype, bm=256):
    M, D = x.shape
    row = pl.BlockSpec((bm, D), lambda i: (i, 0))
    return pl.pallas_call(
        _add_rmsnorm_body,
        grid=(M // bm,),
        in_specs=[row, row, pl.BlockSpec((1, D), lambda i: (0, 0))],
        out_specs=[row, row],
        out_shape=[jax.ShapeDtypeStruct((M, D), F32), jax.ShapeDtypeStruct((M, D), out_dtype)],
        compiler_params=_cparams("parallel"),
        name="add_rmsnorm",
    )(x, y, g.reshape(1, D))


def _add_rmsnorm_final_body(x_ref, y_ref, g_ref, o_ref):
    x = x_ref[...] + y_ref[...].astype(F32)
    y = x * lax.rsqrt(jnp.mean(x * x, axis=-1, keepdims=True) + EPS)
    o_ref[...] = y * g_ref[...]


def add_rmsnorm_final(x, y, g, bm=256):
    M, D = x.shape
    row = pl.BlockSpec((bm, D), lambda i: (i, 0))
    return pl.pallas_call(
        _add_rmsnorm_final_body,
        grid=(M // bm,),
        in_specs=[row, row, pl.BlockSpec((1, D), lambda i: (0, 0))],
        out_specs=row,
        out_shape=jax.ShapeDtypeStruct((M, D), F32),
        compiler_params=_cparams("parallel"),
        name="add_rmsnorm_final",
    )(x, y, g.reshape(1, D))


def _mm_acc_body(x_ref, w_ref, acc_ref, o_ref, wbf_ref):
    @pl.when(pl.program_id(1) == 0)
    def _():
        wbf_ref[...] = w_ref[...].astype(BF16)

    o_ref[...] = acc_ref[...] + _dot(x_ref[...], wbf_ref[...])


def matmul_acc(x, w, layer, acc, bm, bn):
    M, K = x.shape
    N = acc.shape[1]
    tile = pl.BlockSpec((bm, bn), lambda j, i: (i, j))
    return pl.pallas_call(
        _mm_acc_body,
        grid=(N // bn, M // bm),
        in_specs=[pl.BlockSpec((bm, K), lambda j, i: (i, 0)),
                  pl.BlockSpec((None, K, bn), lambda j, i: (layer, 0, j)), tile],
        out_specs=tile,
        out_shape=jax.ShapeDtypeStruct((M, N), F32),
        scratch_shapes=[pltpu.VMEM((K, bn), BF16)],
        compiler_params=_cparams("parallel", "arbitrary"),
        name="matmul_acc",
    )(x, w, acc)


def _mm_nt_body(x_ref, wt_ref, o_ref, wbf_ref):
    @pl.when(pl.program_id(1) == 0)
    def _():
        wbf_ref[...] = wt_ref[...].astype(BF16)

    o_ref[...] = _dot_nt(x_ref[...], wbf_ref[...]).astype(o_ref.dtype)


def matmul_nt(x, wt, layer, n_cols, out_dtype, bm, bn):
    M, K = x.shape
    return pl.pallas_call(
        _mm_nt_body,
        grid=(n_cols // bn, M // bm),
        in_specs=[pl.BlockSpec((bm, K), lambda j, i: (i, 0)),
                  pl.BlockSpec((None, bn, K), lambda j, i: (layer, j, 0))],
        out_specs=pl.BlockSpec((bm, bn), lambda j, i: (i, j)),
        out_shape=jax.ShapeDtypeStruct((M, n_cols), out_dtype),
        scratch_shapes=[pltpu.VMEM((bn, K), BF16)],
        compiler_params=_cparams("parallel", "arbitrary"),
        name="matmul_nt",
    )(x, wt)


MM_TAIL = 128


def _mm_nt_shifted_body(x_ref, wa_ref, wb_ref, o_ref, wbf_ref, *, shift):
    @pl.when(pl.program_id(1) == 0)
    def _():
        keep = wa_ref.shape[0] - shift
        wbf_ref[:keep, :] = wa_ref[shift:, :].astype(BF16)
        wbf_ref[keep:, :] = wb_ref[:shift, :].astype(BF16)

    o_ref[...] = _dot_nt(x_ref[...], wbf_ref[...]).astype(o_ref.dtype)


def matmul_nt_shifted(x, wt, layer, row0, n_cols, out_dtype, bm, bn):
    M, K = x.shape
    base, shift = divmod(row0, MM_TAIL)
    assert (base * MM_TAIL) % bn == 0 and 0 < shift and shift % 16 == 0
    ja = base * MM_TAIL // bn
    per = bn // MM_TAIL
    return pl.pallas_call(
        functools.partial(_mm_nt_shifted_body, shift=shift),
        grid=(n_cols // bn, M // bm),
        in_specs=[pl.BlockSpec((bm, K), lambda j, i: (i, 0)),
                  pl.BlockSpec((None, bn, K), lambda j, i: (layer, ja + j, 0)),
                  pl.BlockSpec((None, MM_TAIL, K), lambda j, i: (layer, base + per * (j + 1), 0))],
        out_specs=pl.BlockSpec((bm, bn), lambda j, i: (i, j)),
        out_shape=jax.ShapeDtypeStruct((M, n_cols), out_dtype),
        scratch_shapes=[pltpu.VMEM((bn, K), BF16)],
        compiler_params=_cparams("parallel", "arbitrary"),
        name="matmul_nt_shifted",
    )(x, wt, wt)


GDN_ROWS = 256
GDN_HB = 8
GDN_PREV = 16


def _gdn_gates_body(ab_ref, alog_ref, dtb_ref, o_ref):
    ab = ab_ref[...]
    lane = lax.broadcasted_iota(jnp.int32, ab.shape, 1)
    g = -jnp.exp(alog_ref[...]) * _softplus(ab + dtb_ref[...])
    o_ref[...] = jnp.where(lane < A_HEADS, g, _sigmoid(ab))


def gdn_gates(p_s, a_log, dt_bias, bm=512):
    S = p_s.shape[0]
    pad = lambda a: jnp.pad(a.astype(F32), (0, LANES - A_HEADS)).reshape(1, LANES)
    vec = pl.BlockSpec((1, LANES), lambda i: (0, 0))
    return pl.pallas_call(
        _gdn_gates_body,
        grid=(S // bm,),
        in_specs=[pl.BlockSpec((bm, LANES), lambda i: (i, PS_AB // LANES)), vec, vec],
        out_specs=pl.BlockSpec((bm, LANES), lambda i: (i, 0)),
        out_shape=jax.ShapeDtypeStruct((S, LANES), F32),
        compiler_params=_cparams("parallel"),
        name="gdn_gates",
    )(p_s, pad(a_log), pad(dt_bias))


def _gdn_body(q_ref, k_ref, v_ref, qp_ref, kp_ref, vp_ref, z_ref, gb_ref, cwq_ref, cwk_ref, cwv_ref,
              nw_ref, o_ref, state_ref):
    hb = pl.program_id(0)
    i = pl.program_id(1)
    R = GDN_ROWS
    C = A_CHUNK
    NC = R // C

    @pl.when(i == 0)
    def _():
        state_ref[...] = jnp.zeros_like(state_ref)

    has_prev = (i > 0).astype(F32)
    lane = lax.broadcasted_iota(jnp.int32, (R, LANES), 1)
    row = lax.broadcasted_iota(jnp.int32, (R, LANES), 0)
    rin = row & (C - 1)
    ri = lax.broadcasted_iota(jnp.int32, (R, R), 0)
    ci = lax.broadcasted_iota(jnp.int32, (R, R), 1)
    same = (ri // C) == (ci // C)
    incl = same & (ri >= ci)
    strict = same & (ri > ci)

    g_all = beta_all = gb_ref[...]
    nw = nw_ref[...]

    def conv_silu(cur_ref, prev_ref, cw_ref, sl):
        cur = cur_ref[:, sl].astype(F32)
        prev = prev_ref[:, sl].astype(F32) * has_prev
        full = jnp.concatenate([prev, cur], axis=0)
        cw = cw_ref[:, sl]
        y = cw[A_CONV - 1:A_CONV, :] * cur
        for s in range(1, A_CONV):
            y = y + cw[A_CONV - 1 - s:A_CONV - s, :] * pltpu.roll(full, s, axis=0)[GDN_PREV:, :]
        return _silu(y)

    heads = range(GDN_HB)
    lanes_of = [slice(hh * LANES, (hh + 1) * LANES) for hh in heads]
    st = [dict() for _ in heads]
    for hh, t in enumerate(st):
        h = hb * GDN_HB + hh
        g = jnp.broadcast_to(jnp.sum(jnp.where(lane == h, g_all, 0.0), axis=1, keepdims=True), (R, LANES))
        t["beta"] = jnp.broadcast_to(
            jnp.sum(jnp.where(lane == h + A_HEADS, beta_all, 0.0), axis=1, keepdims=True), (R, LANES))
        gc = g
        s = 1
        while s < C:
            gc = gc + jnp.where(rin >= s, pltpu.roll(gc, s, axis=0), 0.0)
            s *= 2
        t["g_last"] = jnp.broadcast_to(jnp.sum(g.reshape(NC, C, LANES), axis=1, keepdims=True),
                                       (NC, C, LANES)).reshape(R, LANES)
        t["gc"] = gc
        t["eg"] = jnp.exp(gc)
        gc_t = gc.T
        diff = jnp.concatenate([gc, gc], axis=1) - jnp.concatenate([gc_t, gc_t], axis=0)
        t["decay"] = jnp.exp(jnp.where(incl, diff, NEG))
    for hh, t in enumerate(st):
        sl = lanes_of[hh]
        q = conv_silu(q_ref, qp_ref, cwq_ref, sl)
        k = conv_silu(k_ref, kp_ref, cwk_ref, sl)
        t["v"] = conv_silu(v_ref, vp_ref, cwv_ref, sl)
        t["qn"] = q * lax.rsqrt(jnp.sum(q * q, axis=-1, keepdims=True) + EPS) * (A_DK ** -0.5)
        t["kn"] = k * lax.rsqrt(jnp.sum(k * k, axis=-1, keepdims=True) + EPS)
    for t in st:
        kn, qn, beta, eg, decay = t["kn"], t["qn"], t["beta"], t["eg"], t["decay"]
        kb = kn * beta
        kn_b = kn.astype(BF16)
        lower = jnp.where(strict, _dot_nt(kb.astype(BF16), kn_b) * decay, 0.0)
        t["xp"] = t["qmat"] = -lower
        t["qk"] = (_dot_nt(qn.astype(BF16), kn_b) * decay).astype(BF16)
        t["rhs"] = jnp.concatenate([t["v"] * beta, kb * eg], axis=1)
        t["q_dec"] = (qn * eg).astype(BF16)
        t["k_dec"] = kn * jnp.exp(t["g_last"] - t["gc"])

    for _ in range(int(math.log2(C)) - 1):
        for t in st:
            xb = t["xp"].astype(BF16)
            t["xp"] = _dot(xb, xb)
            t["qmat"] = t["qmat"] + t["xp"] + _dot(t["qmat"].astype(BF16), t["xp"].astype(BF16))
    for t in st:
        sol = t["rhs"] + _dot(t["qmat"].astype(BF16), t["rhs"].astype(BF16))
        t["u"] = sol[:, :A_DV]
        t["w_b"] = sol[:, A_DV:].astype(BF16)
        t["outs"] = []

    states = [state_ref[hh] for hh in heads]
    for c in range(NC):
        rs = slice(c * C, (c + 1) * C)
        for hh, t in enumerate(st):
            sb = states[hh].astype(BF16)
            v_new_b = (t["u"][rs] - _dot(t["w_b"][rs], sb)).astype(BF16)
            t["outs"].append(_dot(t["q_dec"][rs], sb) + _dot(t["qk"][rs, :][:, rs], v_new_b))
            states[hh] = (states[hh] * jnp.exp(t["g_last"][c * C:c * C + 1, :])
                          + _dot(t["k_dec"][rs].T.astype(BF16), v_new_b))
    for hh, t in enumerate(st):
        sl = lanes_of[hh]
        state_ref[hh] = states[hh]
        out = jnp.concatenate(t["outs"], axis=0)
        out = out * lax.rsqrt(jnp.mean(out * out, axis=-1, keepdims=True) + EPS) * nw
        o_ref[:, sl] = (out * _silu(z_ref[:, sl].astype(F32))).astype(o_ref.dtype)


def gdn_branch(p_a, p_s, conv_w, a_log, dt_bias, norm_w):
    S = p_a.shape[0]
    R, W = GDN_ROWS, GDN_HB * LANES
    nq = A_QK_W // W
    gb = gdn_gates(p_s, a_log, dt_bias)

    def cur(off):
        return pl.BlockSpec((R, W), lambda hb, i: (i, off + hb))

    def prev(off):
        return pl.BlockSpec((GDN_PREV, W), lambda hb, i: (jnp.maximum(i * (R // GDN_PREV) - 1, 0), off + hb))

    def cw(off):
        return pl.BlockSpec((A_CONV, W), lambda hb, i: (0, off + hb))

    vec = pl.BlockSpec((1, LANES), lambda hb, i: (0, 0))
    return pl.pallas_call(
        _gdn_body,
        grid=(A_HEADS // GDN_HB, S // R),
        in_specs=[cur(0), cur(nq), cur(2 * nq), prev(0), prev(nq), prev(2 * nq), cur(3 * nq),
                  pl.BlockSpec((R, LANES), lambda hb, i: (i, 0)),
                  cw(0), cw(nq), cw(2 * nq), vec],
        out_specs=pl.BlockSpec((R, W), lambda hb, i: (i, hb)),
        out_shape=jax.ShapeDtypeStruct((S, A_V_W), BF16),
        scratch_shapes=[pltpu.VMEM((GDN_HB, A_DK, A_DV), F32)],
        compiler_params=_cparams("parallel", "arbitrary"),
        name="gdn",
    )(p_a, p_a, p_a, p_a, p_a, p_a, p_a, gb, conv_w, conv_w, conv_w, norm_w.astype(F32).reshape(1, LANES))


MLA_HG = 4


def _mla_prep_body(cq_ref, ckv_ref, kr_ref, qn_ref, kvn_ref, wq_ref, wkv_ref, cs_ref, qo_ref, ko_ref, vo_ref):
    scale = (B_NOPE + B_ROPE) ** -0.5 * math.log2(math.e)
    cq = cq_ref[...]
    cq = (cq * lax.rsqrt(jnp.mean(cq * cq, axis=-1, keepdims=True) + EPS) * qn_ref[...]).astype(BF16)
    ckv = ckv_ref[...]
    ckv = (ckv * lax.rsqrt(jnp.mean(ckv * ckv, axis=-1, keepdims=True) + EPS) * kvn_ref[...]).astype(BF16)
    rq = _dot(cq, wq_ref[...])
    rkv = _dot(ckv, wkv_ref[...])
    cs = cs_ref[...]
    lane = lax.broadcasted_iota(jnp.int32, cs.shape, 1)

    def rope(pair):
        prod = pair * cs
        return jnp.where(lane < B_ROPE, prod + pltpu.roll(prod, B_ROPE, axis=1), 0.0)

    k_pe = rope(kr_ref[...]).astype(BF16)
    for hh in range(MLA_HG):
        qb = hh * B_QH
        qo_ref[:, hh * B_KH:hh * B_KH + LANES] = (rq[:, qb:qb + LANES] * scale).astype(BF16)
        qo_ref[:, hh * B_KH + LANES:(hh + 1) * B_KH] = (rope(rq[:, qb + LANES:qb + 2 * LANES]) * scale).astype(BF16)
        ko_ref[:, hh * B_KH:hh * B_KH + LANES] = rkv[:, hh * 2 * LANES:hh * 2 * LANES + LANES].astype(BF16)
        ko_ref[:, hh * B_KH + LANES:(hh + 1) * B_KH] = k_pe
        vo_ref[:, hh * 2 * LANES:hh * 2 * LANES + LANES] = (
            rkv[:, hh * 2 * LANES + LANES:(hh + 1) * 2 * LANES].astype(BF16))
        vo_ref[:, hh * 2 * LANES + LANES:(hh + 1) * 2 * LANES] = jnp.ones((rkv.shape[0], LANES), BF16)


def mla_prep(p_s, q_norm, kv_norm, wq_ext, wkv, cs_b, bm=512):
    S = p_s.shape[0]
    HG = MLA_HG
    return pl.pallas_call(
        _mla_prep_body,
        grid=(B_HEADS // HG, S // bm),
        in_specs=[
            pl.BlockSpec((bm, B_Q_LORA), lambda g, i: (i, PS_CQ // B_Q_LORA)),
            pl.BlockSpec((bm, B_KV_LORA), lambda g, i: (i, PS_CKV // B_KV_LORA)),
            pl.BlockSpec((bm, LANES), lambda g, i: (i, PS_KR // LANES)),
            pl.BlockSpec((1, B_Q_LORA), lambda g, i: (0, 0)),
            pl.BlockSpec((1, B_KV_LORA), lambda g, i: (0, 0)),
            pl.BlockSpec((B_Q_LORA, HG * B_QH), lambda g, i: (0, g)),
            pl.BlockSpec((B_KV_LORA, HG * 2 * LANES), lambda g, i: (0, g)),
            pl.BlockSpec((bm, LANES), lambda g, i: (i, 0)),
        ],
        out_specs=[
            pl.BlockSpec((bm, HG * B_KH), lambda g, i: (i, g)),
            pl.BlockSpec((bm, HG * B_KH), lambda g, i: (i, g)),
            pl.BlockSpec((bm, HG * 2 * LANES), lambda g, i: (i, g)),
        ],
        out_shape=[
            jax.ShapeDtypeStruct((S, B_HEADS * B_KH), BF16),
            jax.ShapeDtypeStruct((S, B_HEADS * B_KH), BF16),
            jax.ShapeDtypeStruct((S, B_HEADS * 2 * B_VDIM), BF16),
        ],
        compiler_params=_cparams("parallel", "parallel"),
        name="mla_prep",
    )(p_s, p_s, p_s, q_norm.reshape(1, -1), kv_norm.reshape(1, -1), wq_ext, wkv, cs_b)


FLASH_BLK = 2048
FLASH_RC = 256


def _flash_body(qi_ref, kj_ref, q_ref, k_ref, v_ref, o_ref, m_ref, acc_ref):
    t = pl.program_id(1)
    qi = qi_ref[t]
    kj = kj_ref[t]

    @pl.when(kj == 0)
    def _():
        m_ref[...] = jnp.full_like(m_ref, NEG)
        acc_ref[...] = jnp.zeros_like(acc_ref)

    def step(diag):
        nc = FLASH_BLK // FLASH_RC
        rows_of = [slice(c * FLASH_RC, (c + 1) * FLASH_RC) for c in range(nc)]
        ncol_of = [(c + 1) * FLASH_RC if diag else FLASH_BLK for c in range(nc)]

        def scores(c):
            s = _dot_nt(q_ref[rows_of[c], :], k_ref[:ncol_of[c], :])
            if diag:
                r = lax.broadcasted_iota(jnp.int32, s.shape, 0) + c * FLASH_RC
                col = lax.broadcasted_iota(jnp.int32, s.shape, 1)
                s = jnp.where(col <= r, s, NEG)
            return s

        s_next = scores(0)
        for c in range(nc):
            s = s_next
            if c + 1 < nc:
                s_next = scores(c + 1)
            rows = rows_of[c]
            m_prev = m_ref[rows, :]
            m_new = jnp.maximum(m_prev, jnp.max(s, axis=1, keepdims=True))
            p = jnp.exp2(s - m_new).astype(BF16)
            acc_ref[rows, :] = jnp.exp2(m_prev - m_new) * acc_ref[rows, :] + _dot(p, v_ref[:ncol_of[c], :])
            m_ref[rows, :] = m_new

    @pl.when(kj < qi)
    def _():
        step(False)

    @pl.when(kj == qi)
    def _():
        step(True)
        acc = acc_ref[...]
        o_ref[...] = (acc[:, :B_VDIM] / acc[:, B_VDIM:]).astype(o_ref.dtype)


def mla_attention(q_cat, k_cat, v_ext):
    S = q_cat.shape[0]
    blk = FLASH_BLK
    n = S // blk
    pairs = [(qi, kj) for qi in range(n) for kj in range(qi + 1)]
    qi_tab = jnp.asarray([p[0] for p in pairs], jnp.int32)
    kj_tab = jnp.asarray([p[1] for p in pairs], jnp.int32)
    return pl.pallas_call(
        _flash_body,
        grid_spec=pltpu.PrefetchScalarGridSpec(
            num_scalar_prefetch=2,
            grid=(B_HEADS, len(pairs)),
            in_specs=[
                pl.BlockSpec((blk, B_KH), lambda h, t, qi, kj: (qi[t], h)),
                pl.BlockSpec((blk, B_KH), lambda h, t, qi, kj: (kj[t], h)),
                pl.BlockSpec((blk, 2 * B_VDIM), lambda h, t, qi, kj: (kj[t], h)),
            ],
            out_specs=pl.BlockSpec((blk, B_VDIM), lambda h, t, qi, kj: (qi[t], h)),
            scratch_shapes=[pltpu.VMEM((blk, 1), F32), pltpu.VMEM((blk, 2 * B_VDIM), F32)],
        ),
        out_shape=jax.ShapeDtypeStruct((S, B_HEADS * B_VDIM), BF16),
        compiler_params=_cparams("parallel", "arbitrary"),
        name="mla_flash",
    )(qi_tab, kj_tab, q_cat, k_cat, v_ext)


DIL_BLK = 128
DIL_ROWS = 256
DILATIONS = tuple(d for _, d in C_PATTERNS)


def _rope_split_body(q_ref, k_ref, v_ref, c_ref, s_ref, *refs):
    nd = len(DILATIONS)
    outs, (sq, sk, sv) = refs[:3 * nd], refs[3 * nd:]
    c = c_ref[...]
    s = s_ref[...]
    scale = C_DH ** -0.5
    for h in range(C_HEADS):
        sl = slice(h * C_DH, (h + 1) * C_DH)
        x = q_ref[:, sl].astype(F32)
        sq[h] = (x * c + pltpu.roll(x, C_DH // 2, axis=1) * s) * scale
        x = k_ref[:, sl].astype(F32)
        sk[h] = x * c + pltpu.roll(x, C_DH // 2, axis=1) * s
        sv[h] = v_ref[:, sl].astype(F32)
        for di, d in enumerate(DILATIONS):
            for src, dst in zip((sq, sk, sv), outs[3 * di:3 * di + 3]):
                for r in range(d):
                    rows = src[h] if d == 1 else src[h, pl.ds(r, DIL_ROWS // d, stride=d), :]
                    dst[r, :, sl] = rows.astype(BF16)


def rope_split(p_r, cos_c, sin_c):
    S = p_r.shape[0]
    bm = DIL_ROWS
    tab = pl.BlockSpec((bm, C_DH), lambda i: (i, 0))
    col = lambda j: pl.BlockSpec((bm, C_W), lambda i: (i, j))
    out_specs, out_shape = [], []
    for d in DILATIONS:
        out_specs += [pl.BlockSpec((d, bm // d, C_W), lambda i: (0, i, 0))] * 3
        out_shape += [jax.ShapeDtypeStruct((d, S // d, C_W), BF16)] * 3
    outs = pl.pallas_call(
        _rope_split_body,
        grid=(S // bm,),
        in_specs=[col(0), col(1), col(2), tab, tab],
        out_specs=out_specs,
        out_shape=out_shape,
        scratch_shapes=[pltpu.VMEM((C_HEADS, bm, C_DH), F32)] * 3,
        compiler_params=_cparams("parallel"),
        name="rope_split",
    )(p_r, p_r, p_r, cos_c, sin_c)
    return [outs[3 * i:3 * i + 3] for i in range(len(DILATIONS))]


def _dilated_body(q_ref, kc_ref, vc_ref, o_ref, lse_ref, kp_ref, vp_ref):
    B = DIL_BLK
    n = pl.program_id(1)

    @pl.when(n == 0)
    def _():
        kp_ref[...] = jnp.zeros_like(kp_ref)
        vp_ref[...] = jnp.zeros_like(vp_ref)

    qi = lax.broadcasted_iota(jnp.int32, (B, 2 * B), 0)
    kj = lax.broadcasted_iota(jnp.int32, (B, 2 * B), 1)
    dist = qi + B - kj
    valid = (dist >= 0) & (dist <= B) & ((kj >= B) | (n > 0))
    lanes_of = [slice(h * C_DH, (h + 1) * C_DH) for h in range(C_HEADS)]
    scores = [jnp.where(valid, _dot_nt(q_ref[:, sl], jnp.concatenate([kp_ref[:, sl], kc_ref[:, sl]], axis=0)), NEG)
              for sl in lanes_of]
    probs = []
    lane = lax.broadcasted_iota(jnp.int32, (B, LANES), 1)
    lse = jnp.zeros((B, LANES), F32)
    for h, s in enumerate(scores):
        m = jnp.max(s, axis=1, keepdims=True)
        p = jnp.exp(s - m)
        l = jnp.sum(p, axis=1, keepdims=True)
        lse = jnp.where(lane == h, m + jnp.log(l), lse)
        probs.append((p.astype(BF16), l))
    lse_ref[...] = lse
    for sl, (p, l) in zip(lanes_of, probs):
        vv = jnp.concatenate([vp_ref[:, sl], vc_ref[:, sl]], axis=0)
        o_ref[:, sl] = (_dot(p, vv) / l).astype(o_ref.dtype)
    kp_ref[...] = kc_ref[...]
    vp_ref[...] = vc_ref[...]


def dilated_pattern(q, k, v):
    d, L, W = q.shape
    B = DIL_BLK
    cur = pl.BlockSpec((None, B, W), lambda r, n: (r, n, 0))
    return pl.pallas_call(
        _dilated_body,
        grid=(d, L // B),
        in_specs=[cur, cur, cur],
        out_specs=[cur, pl.BlockSpec((None, B, LANES), lambda r, n: (r, n, 0))],
        out_shape=[jax.ShapeDtypeStruct((d, L, W), BF16), jax.ShapeDtypeStruct((d, L, LANES), F32)],
        scratch_shapes=[pltpu.VMEM((B, W), BF16)] * 2,
        compiler_params=_cparams("arbitrary", "arbitrary"),
        name=f"dilated_d{d}",
    )(q, k, v)


def _dilated_merge_body(*refs):
    nd = len(DILATIONS)
    o_refs, lse_refs, out_ref = refs[:nd], refs[nd:2 * nd], refs[2 * nd]
    so, sl = refs[2 * nd + 1:]
    rows_of = lambda d, r: slice(None) if d == 1 else pl.ds(r, DIL_ROWS // d, stride=d)
    for di, d in enumerate(DILATIONS):
        for r in range(d):
            sl[di, rows_of(d, r), :] = lse_refs[di][r]
    top = sl[0]
    for di in range(1, nd):
        top = jnp.maximum(top, sl[di])
    weights = [jnp.exp(sl[di] - top) for di in range(nd)]
    lane = lax.broadcasted_iota(jnp.int32, top.shape, 1)
    for h in range(C_HEADS):
        cols = slice(h * C_DH, (h + 1) * C_DH)
        for di, d in enumerate(DILATIONS):
            for r in range(d):
                so[di, rows_of(d, r), :] = o_refs[di][r, :, cols].astype(F32)
        num = 0.0
        den = 0.0
        for di in range(nd):
            w = jnp.sum(jnp.where(lane == h, weights[di], 0.0), axis=1, keepdims=True)
            num = num + w * so[di]
            den = den + w
        out_ref[:, cols] = (num / den).astype(out_ref.dtype)


def dilated_merge(outs, lses):
    S = outs[0].shape[0] * outs[0].shape[1]
    bm = DIL_ROWS
    nd = len(DILATIONS)
    specs = [pl.BlockSpec((d, bm // d, C_W), lambda i: (0, i, 0)) for d in DILATIONS]
    lse_specs = [pl.BlockSpec((d, bm // d, LANES), lambda i: (0, i, 0)) for d in DILATIONS]
    return pl.pallas_call(
        _dilated_merge_body,
        grid=(S // bm,),
        in_specs=specs + lse_specs,
        out_specs=pl.BlockSpec((bm, C_W), lambda i: (i, 0)),
        out_shape=jax.ShapeDtypeStruct((S, C_W), BF16),
        scratch_shapes=[pltpu.VMEM((nd, bm, C_DH), F32)] * 2,
        compiler_params=_cparams("parallel"),
        name="dilated_merge",
    )(*outs, *lses)


def dilated_branch(p_r, cos_c, sin_c):
    assert all(w // d == DIL_BLK for w, d in C_PATTERNS)
    results = [dilated_pattern(q, k, v) for q, k, v in rope_split(p_r, cos_c, sin_c)]
    return dilated_merge([r[0] for r in results], [r[1] for r in results])


def _merge_body(oa_ref, ob_ref, oc_ref, ga_ref, gb_ref, gc_ref, wa_ref, wb_ref, wc_ref, o_ref, wa_s, wb_s, wc_s):
    @pl.when(pl.program_id(1) == 0)
    def _():
        wa_s[...] = wa_ref[...].astype(BF16)
        wb_s[...] = wb_ref[...].astype(BF16)
        wc_s[...] = wc_ref[...].astype(BF16)

    gates = [_sigmoid(g[...].astype(F32)) for g in (ga_ref, gb_ref, gc_ref)]
    ya = _dot(oa_ref[...], wa_s[...])
    yb = _dot(ob_ref[...], wb_s[...])
    yc = _dot(oc_ref[...], wc_s[...])
    o_ref[...] = (gates[0] * ya + gates[1] * yb + gates[2] * yc).astype(o_ref.dtype)


def merge_branches(o_a, o_b, o_c, p_r, w_a, w_b, w_c, layer, bm=512, bn=512):
    S, K = o_a.shape
    g0 = 3 * C_W // bn
    gstep = D_MODEL // bn
    act = pl.BlockSpec((bm, K), lambda j, i: (i, 0))
    wsp = pl.BlockSpec((None, K, bn), lambda j, i: (layer, 0, j))
    gate = lambda b: pl.BlockSpec((bm, bn), lambda j, i: (i, g0 + b * gstep + j))
    return pl.pallas_call(
        _merge_body,
        grid=(D_MODEL // bn, S // bm),
        in_specs=[act, act, act, gate(0), gate(1), gate(2), wsp, wsp, wsp],
        out_specs=pl.BlockSpec((bm, bn), lambda j, i: (i, j)),
        out_shape=jax.ShapeDtypeStruct((S, D_MODEL), BF16),
        scratch_shapes=[pltpu.VMEM((K, bn), BF16)] * 3,
        compiler_params=_cparams("parallel", "arbitrary"),
        name="merge_branches",
    )(o_a, o_b, o_c, p_r, p_r, p_r, w_a, w_b, w_c)


def _new_group(tg_ref):
    i = pl.program_id(1)
    return (i == 0) | (tg_ref[i] != tg_ref[jnp.maximum(i - 1, 0)])


def _swiglu_up_body(tg_ref, nt_ref, x_ref, w1_ref, w3_ref, o_ref, w1_s, w3_s):
    @pl.when(_new_group(tg_ref))
    def _():
        w1_s[...] = w1_ref[...].astype(BF16)
        w3_s[...] = w3_ref[...].astype(BF16)

    @pl.when(pl.program_id(1) < nt_ref[0])
    def _():
        x = x_ref[...]
        o_ref[...] = (_silu(_dot(x, w1_s[...])) * _dot(x, w3_s[...])).astype(o_ref.dtype)

    @pl.when(pl.program_id(1) >= nt_ref[0])
    def _():
        o_ref[...] = jnp.zeros_like(o_ref)


def swiglu_up(x, w1, w3, tile_group, n_tiles, bm, bf):
    R, D = x.shape
    F = w1.shape[2]
    wspec = pl.BlockSpec((None, D, bf), lambda f, i, tg, nt: (tg[i], 0, f))
    return pl.pallas_call(
        _swiglu_up_body,
        grid_spec=pltpu.PrefetchScalarGridSpec(
            num_scalar_prefetch=2,
            grid=(F // bf, R // bm),
            in_specs=[pl.BlockSpec((bm, D), lambda f, i, tg, nt: (jnp.minimum(i, nt[0] - 1), 0)), wspec, wspec],
            out_specs=pl.BlockSpec((bm, bf), lambda f, i, tg, nt: (i, f)),
            scratch_shapes=[pltpu.VMEM((D, bf), BF16)] * 2,
        ),
        out_shape=jax.ShapeDtypeStruct((R, F), BF16),
        compiler_params=_cparams("arbitrary", "arbitrary"),
        name="swiglu_up",
    )(tile_group, n_tiles, x, w1, w3)


def _grouped_mm_body(tg_ref, nt_ref, x_ref, w_ref, *rest, has_acc):
    if has_acc:
        acc_ref, o_ref, w_s = rest
    else:
        o_ref, w_s = rest

    @pl.when(_new_group(tg_ref))
    def _():
        w_s[...] = w_ref[...].astype(BF16)

    @pl.when(pl.program_id(1) < nt_ref[0])
    def _():
        y = _dot(x_ref[...], w_s[...])
        o_ref[...] = acc_ref[...] + y if has_acc else y

    @pl.when(pl.program_id(1) >= nt_ref[0])
    def _():
        o_ref[...] = jnp.zeros_like(o_ref)


def grouped_matmul(x, w, tile_group, n_tiles, bm, bn, k_blocks=1, k_index=0, acc=None):
    R, K = x.shape
    N = w.shape[2]
    kb = K // k_blocks
    in_specs = [pl.BlockSpec((bm, kb), lambda j, i, tg, nt: (jnp.minimum(i, nt[0] - 1), k_index)),
                pl.BlockSpec((None, kb, bn), lambda j, i, tg, nt: (tg[i], k_index, j))]
    out_spec = pl.BlockSpec((bm, bn), lambda j, i, tg, nt: (i, j))
    args = [tile_group, n_tiles, x, w]
    if acc is not None:
        in_specs.append(out_spec)
        args.append(acc)
    return pl.pallas_call(
        functools.partial(_grouped_mm_body, has_acc=acc is not None),
        grid_spec=pltpu.PrefetchScalarGridSpec(
            num_scalar_prefetch=2,
            grid=(N // bn, R // bm),
            in_specs=in_specs,
            out_specs=out_spec,
            scratch_shapes=[pltpu.VMEM((kb, bn), BF16)],
        ),
        out_shape=jax.ShapeDtypeStruct((R, N), F32),
        input_output_aliases={4: 0} if acc is not None else {},
        compiler_params=_cparams("arbitrary", "arbitrary"),
        name="grouped_matmul",
    )(*args)


DENSE_UP_BM = 1024
DENSE_DOWN_BM = 512
DENSE_KSPLIT = 2


def _one_group(n_rows, bm, group):
    return jnp.full((n_rows // bm,), group, jnp.int32), jnp.full((1,), n_rows // bm, jnp.int32)


def dense_ffn(h, w1, w3, w2, index, x_res):
    S = h.shape[0]
    g = swiglu_up(h, w1, w3, *_one_group(S, DENSE_UP_BM, index), bm=DENSE_UP_BM, bf=256)
    y = x_res
    for k in range(DENSE_KSPLIT):
        y = grouped_matmul(g, w2, *_one_group(S, DENSE_DOWN_BM, index), bm=DENSE_DOWN_BM, bn=512,
                           k_blocks=DENSE_KSPLIT, k_index=k, acc=y)
    return y


def _router_body(h_ref, rw_ref, sel_ref, gate_ref):
    logits = _dot(h_ref[...].astype(BF16), rw_ref[...].astype(BF16))
    lane = lax.broadcasted_iota(jnp.int32, logits.shape, 1)
    logits = jnp.where(lane < N_EXPERTS, logits, NEG)
    t1 = jnp.max(logits, axis=1, keepdims=True)
    i1 = jnp.min(jnp.where(logits == t1, lane, LANES), axis=1, keepdims=True)
    rest = jnp.where(lane == i1, NEG, logits)
    t2 = jnp.max(rest, axis=1, keepdims=True)
    i2 = jnp.min(jnp.where(rest == t2, lane, LANES), axis=1, keepdims=True)
    e2 = jnp.exp(t2 - t1)
    g1 = 1.0 / (1.0 + e2)
    sel_ref[...] = jnp.where(lane == 0, i1, jnp.where(lane == 1, i2, 0))
    gate_ref[...] = jnp.where(lane == 0, g1, jnp.where(lane == 1, e2 * g1, 0.0))


def router(h, router_w, bm=512):
    S, D = h.shape
    rw = jnp.pad(router_w, ((0, 0), (0, LANES - N_EXPERTS)))
    out = pl.BlockSpec((bm, LANES), lambda i: (i, 0))
    return pl.pallas_call(
        _router_body,
        grid=(S // bm,),
        in_specs=[pl.BlockSpec((bm, D), lambda i: (i, 0)), pl.BlockSpec((D, LANES), lambda i: (0, 0))],
        out_specs=[out, out],
        out_shape=[jax.ShapeDtypeStruct((S, LANES), jnp.int32), jax.ShapeDtypeStruct((S, LANES), F32)],
        compiler_params=_cparams("parallel"),
        name="router",
    )(h, rw)


MOE_BM = 512
MOE_BG = 256
DMA_QUEUES = 2


def _moe_plan(sel, bm):
    S = sel.shape[0]
    E = N_EXPERTS
    e = sel[:, :TOP_K].reshape(-1)
    onehot = (e[:, None] == jnp.arange(E, dtype=jnp.int32)[None, :]).astype(jnp.int32)
    csum = jnp.cumsum(onehot, axis=0)
    rank = jnp.sum((csum - 1) * onehot, axis=1)
    tiles_per = (csum[-1] + bm - 1) // bm
    tile_end = jnp.cumsum(tiles_per)
    pos = ((tile_end - tiles_per)[e] * bm + rank).astype(jnp.int32)
    n_tiles = tile_end[-1:].astype(jnp.int32)
    T = (TOP_K * S) // bm + E
    tid = jnp.arange(T, dtype=jnp.int32)
    te = jnp.minimum(jnp.sum((tid[:, None] >= tile_end[None, :]).astype(jnp.int32), axis=1), E - 1)
    te = jnp.where(tid < n_tiles[0], te, te[n_tiles[0] - 1]).astype(jnp.int32)
    src = jnp.zeros((T * bm,), jnp.int32).at[pos].set(jnp.arange(TOP_K * S, dtype=jnp.int32) // TOP_K)
    return pos, src, te, n_tiles


def _moe_gather_body(src_ref, slab_hbm, o_ref, buf_ref, sem):
    base = pl.program_id(0) * MOE_BG
    tiles = o_ref.shape[1] // LANES
    pitch = _slab_pitch(o_ref.shape[1])

    def row_copy(j):
        src_rows = pl.ds(src_ref[base + j] * pitch, tiles)
        return pltpu.make_async_copy(slab_hbm.at[src_rows, :], buf_ref.at[pl.ds(j * pitch, tiles), :], sem)

    def start(jj, c):
        for par in range(DMA_QUEUES):
            row_copy(jj * DMA_QUEUES + par).start(priority=par)
        return c

    def wait(j, c):
        row_copy(j).wait()
        return c

    lax.fori_loop(0, MOE_BG // DMA_QUEUES, start, 0)
    lax.fori_loop(0, MOE_BG, wait, 0)
    for c in range(tiles):
        o_ref[:, c * LANES:(c + 1) * LANES] = buf_ref[pl.ds(c, MOE_BG, stride=pitch), :].astype(o_ref.dtype)


def moe_gather(slab, src, d_model):
    R = src.shape[0]
    return pl.pallas_call(
        _moe_gather_body,
        grid_spec=pltpu.PrefetchScalarGridSpec(
            num_scalar_prefetch=1,
            grid=(R // MOE_BG,),
            in_specs=[pl.BlockSpec(memory_space=pl.ANY)],
            out_specs=pl.BlockSpec((MOE_BG, d_model), lambda i, src: (i, 0)),
            scratch_shapes=[pltpu.VMEM((MOE_BG * _slab_pitch(d_model), LANES), F32), pltpu.SemaphoreType.DMA(())],
        ),
        out_shape=jax.ShapeDtypeStruct((R, d_model), BF16),
        compiler_params=_cparams("arbitrary"),
        name="moe_gather",
    )(src, slab)


def _moe_combine_body(pos_ref, gate_ref, ys_hbm, o_ref, buf_ref, sem):
    base = pl.program_id(0) * MOE_BG

    def row_copy(j, k):
        row = pos_ref[(base + j) * TOP_K + k]
        return pltpu.make_async_copy(ys_hbm.at[pl.ds(row, 1), :], buf_ref.at[k, pl.ds(j, 1), :], sem)

    def start(j, c):
        for k in range(TOP_K):
            row_copy(j, k).start(priority=k % DMA_QUEUES)
        return c

    def wait(j, c):
        for k in range(TOP_K):
            row_copy(j, k).wait()
        return c

    lax.fori_loop(0, MOE_BG, start, 0)
    lax.fori_loop(0, MOE_BG, wait, 0)
    gate = gate_ref[...]
    acc = gate[:, 0:1] * buf_ref[0]
    for k in range(1, TOP_K):
        acc = acc + gate[:, k:k + 1] * buf_ref[k]
    o_ref[...] = acc


def moe_combine(ys, pos, gate):
    S = gate.shape[0]
    D = ys.shape[1]
    return pl.pallas_call(
        _moe_combine_body,
        grid_spec=pltpu.PrefetchScalarGridSpec(
            num_scalar_prefetch=1,
            grid=(S // MOE_BG,),
            in_specs=[pl.BlockSpec((MOE_BG, LANES), lambda i, pos: (i, 0)), pl.BlockSpec(memory_space=pl.ANY)],
            out_specs=pl.BlockSpec((MOE_BG, D), lambda i, pos: (i, 0)),
            scratch_shapes=[pltpu.VMEM((TOP_K, MOE_BG, D), F32), pltpu.SemaphoreType.DMA(())],
        ),
        out_shape=jax.ShapeDtypeStruct((S, D), F32),
        compiler_params=_cparams("arbitrary"),
        name="moe_combine",
    )(pos, gate, ys)


def moe_layer(h, h_slab, router_w, w1, w3, w2, index):
    sel, gate = router(h, router_w)
    pos, src, tile_expert, n_tiles = _moe_plan(sel, MOE_BM)
    tile_expert = tile_expert + index * N_EXPERTS
    w1, w3, w2 = (w.reshape((-1,) + w.shape[2:]) for w in (w1, w3, w2))
    xs = moe_gather(h_slab, src, h.shape[1])
    g = swiglu_up(xs, w1, w3, tile_expert, n_tiles, bm=MOE_BM, bf=512)
    ys = grouped_matmul(g, w2, tile_expert, n_tiles, bm=MOE_BM, bn=1024)
    return moe_combine(ys, pos, gate)


def _rot_cols(w):
    half = w.shape[-1] // 2
    return jnp.concatenate([-w[..., half:], w[..., :half]], axis=-1)


def _rope_tables(seq, dim):
    inv = ROPE_THETA ** (-jnp.arange(0, dim, 2, dtype=F32) / dim)
    ang = jnp.arange(seq, dtype=F32)[:, None] * inv[None, :]
    return jnp.cos(ang), jnp.sin(ang)


def _small_weight_t(wt_mid):
    row = lambda off, n: wt_mid[off - OFF_A_IN:off - OFF_A_IN + n]
    kr = row(OFF_KROPE, B_ROPE)
    parts = [row(OFF_CQ, B_Q_LORA), row(OFF_CKV, B_KV_LORA), kr, _rot_cols(kr.T).T,
             row(OFF_A_IN, A_HEADS), row(OFF_B_IN, A_HEADS)]
    w = jnp.concatenate(parts, axis=0)
    return jnp.pad(w, ((0, PS_W - w.shape[0]), (0, 0)))


def _mla_weights(w_uq, w_ukv):
    wq = w_uq.reshape(B_Q_LORA, B_HEADS, B_NOPE + B_ROPE)
    pe = wq[..., B_NOPE:]
    wq_ext = jnp.concatenate([wq[..., :B_NOPE], pe, _rot_cols(pe)], axis=-1)
    return wq_ext.reshape(B_Q_LORA, B_HEADS * B_QH).astype(BF16), w_ukv.astype(BF16)


def hybrid_mixer(h, x_res, layer, p, tables):
    cs_b, cos_c, sin_c = tables
    wt_in = jnp.swapaxes(p["w_in"], 1, 2)
    p_a = matmul_nt(h, wt_in, layer, OFF_A_IN, BF16, bm=1024, bn=512)
    p_s = matmul_nt(h, _small_weight_t(wt_in[layer, OFF_A_IN:OFF_REST])[None], 0, PS_W, F32, bm=1024, bn=256)
    p_r = matmul_nt_shifted(h, wt_in, layer, OFF_REST, REST_W, BF16, bm=1024, bn=512)
    o_a = gdn_branch(p_a, p_s, p["conv_w"][layer], p["gdn_a_log"][layer], p["gdn_dt_bias"][layer],
                     p["gdn_norm"][layer])
    wq_ext, wkv = _mla_weights(p["mla_w_uq"][layer], p["mla_w_ukv"][layer])
    q_cat, k_cat, v_b = mla_prep(p_s, p["mla_q_norm"][layer], p["mla_kv_norm"][layer], wq_ext, wkv, cs_b)
    o_b = mla_attention(q_cat, k_cat, v_b)
    o_c = dilated_branch(p_r, cos_c, sin_c)
    merged = merge_branches(o_a, o_b, o_c, p_r, p["w_branch_a"], p["w_branch_b"], p["w_branch_c"], layer)
    return matmul_acc(merged, p["w_out"], layer, x_res, bm=1024, bn=512)


def kernel(x, attn_norm, w_in, conv_w, gdn_a_log, gdn_dt_bias, gdn_norm, mla_q_norm, mla_w_uq, mla_kv_norm, mla_w_ukv, w_branch_a, w_branch_b, w_branch_c, w_out, ffn_norm, dense_w1, dense_w3, dense_w2, router_w, moe_w1, moe_w3, moe_w2, final_norm):
    p = dict(w_in=w_in, conv_w=conv_w, gdn_a_log=gdn_a_log, gdn_dt_bias=gdn_dt_bias, gdn_norm=gdn_norm,
             mla_q_norm=mla_q_norm, mla_w_uq=mla_w_uq, mla_kv_norm=mla_kv_norm, mla_w_ukv=mla_w_ukv,
             w_branch_a=w_branch_a, w_branch_b=w_branch_b, w_branch_c=w_branch_c, w_out=w_out)
    Bn, S, D = x.shape
    assert Bn == 1 and D == D_MODEL
    cos_b, sin_b = _rope_tables(S, B_ROPE)
    cos_c, sin_c = _rope_tables(S, C_DH)
    tables = (jnp.concatenate([cos_b, cos_b, sin_b, sin_b], axis=1),
              jnp.concatenate([cos_c, cos_c], axis=1), jnp.concatenate([-sin_c, sin_c], axis=1))
    xr = x.reshape(S, D)
    h = rmsnorm(xr, attn_norm[0], BF16)
    out = None
    for layer in range(DEPTH):
        last = layer + 1 == DEPTH
        xr = hybrid_mixer(h, xr, layer, p, tables)
        i = layer // 2
        if layer % 2 == 0:
            h = rmsnorm(xr, ffn_norm[layer], BF16)
            xr = dense_ffn(h, dense_w1, dense_w3, dense_w2, i, xr)
            if last:
                out = rmsnorm(xr, final_norm, F32)
            else:
                h = rmsnorm(xr, attn_norm[layer + 1], BF16)
        else:
            h, h_slab = rmsnorm_slab(xr, ffn_norm[layer])
            y = moe_layer(h, h_slab, router_w[i], moe_w1, moe_w3, moe_w2, i)
            if last:
                out = add_rmsnorm_final(xr, y, final_norm)
            else:
                xr, h = add_rmsnorm(xr, y, attn_norm[layer + 1], BF16)
    return out.reshape(Bn, S, D)
```

```python
import functools
import math

import jax
import jax.numpy as jnp
from jax import lax
from jax.experimental import pallas as pl
from jax.experimental.pallas import tpu as pltpu

F32 = jnp.float32
BF16 = jnp.bfloat16

EPS = 1e-6
ROPE_THETA = 10000.0
NEG = -1e30

LANES = 128
VMEM_LIMIT_BYTES = 56 * 1024 * 1024

D_MODEL = 4096
DEPTH = 2
A_HEADS, A_DK, A_DV, A_CONV, A_CHUNK = 16, 128, 128, 4, 64
B_HEADS, B_Q_LORA, B_KV_LORA, B_NOPE, B_ROPE, B_VDIM = 16, 1024, 512, 128, 64, 128
C_HEADS, C_DH = 16, 128
C_PATTERNS = ((128, 1), (512, 4), (2048, 16))
D_FF = 11008
N_EXPERTS, TOP_K, D_EXPERT = 8, 2, 4096

A_QK_W = A_HEADS * A_DK
A_V_W = A_HEADS * A_DV
A_QKV_W = 2 * A_QK_W + A_V_W
C_W = C_HEADS * C_DH
OFF_A_IN = A_QKV_W + A_V_W
OFF_B_IN = OFF_A_IN + A_HEADS
OFF_CQ = OFF_B_IN + A_HEADS
OFF_CKV = OFF_CQ + B_Q_LORA
OFF_KROPE = OFF_CKV + B_KV_LORA
OFF_REST = OFF_KROPE + B_ROPE
REST_W = 3 * C_W + 3 * D_MODEL
PS_CQ, PS_CKV, PS_KR, PS_AB, PS_W = 0, 1024, 1536, 1664, 1792

B_QH = 256
B_KH = 256


def _cparams(*sem):
    return pltpu.CompilerParams(dimension_semantics=sem, vmem_limit_bytes=VMEM_LIMIT_BYTES)


def _dot(a, b):
    return jnp.dot(a, b, preferred_element_type=F32)


def _dot_nt(a, b):
    return lax.dot_general(a, b, (((1,), (1,)), ((), ())), preferred_element_type=F32)


def _sigmoid(x):
    return 1.0 / (1.0 + jnp.exp(-x))


def _silu(x):
    return x * _sigmoid(x)


def _softplus(x):
    return jnp.maximum(x, 0.0) + jnp.log(1.0 + jnp.exp(-jnp.abs(x)))


def _rmsnorm_body(x_ref, g_ref, o_ref):
    x = x_ref[...]
    y = x * lax.rsqrt(jnp.mean(x * x, axis=-1, keepdims=True) + EPS)
    o_ref[...] = (y * g_ref[...]).astype(o_ref.dtype)


def rmsnorm(x, g, out_dtype, bm=256):
    M, D = x.shape
    return pl.pallas_call(
        _rmsnorm_body,
        grid=(M // bm,),
        in_specs=[pl.BlockSpec((bm, D), lambda i: (i, 0)), pl.BlockSpec((1, D), lambda i: (0, 0))],
        out_specs=pl.BlockSpec((bm, D), lambda i: (i, 0)),
        out_shape=jax.ShapeDtypeStruct((M, D), out_dtype),
        compiler_params=_cparams("parallel"),
        name="rmsnorm",
    )(x, g.reshape(1, D))


def _rmsnorm_slab_body(x_ref, g_ref, ho_ref, slab_ref):
    x = x_ref[...]
    y = x * lax.rsqrt(jnp.mean(x * x, axis=-1, keepdims=True) + EPS) * g_ref[...]
    ho_ref[...] = y.astype(ho_ref.dtype)
    bm, D = x.shape
    pitch = _slab_pitch(D)
    slab_ref[...] = jnp.zeros_like(slab_ref)
    for c in range(D // LANES):
        slab_ref[pl.ds(c, bm, stride=pitch), :] = y[:, c * LANES:(c + 1) * LANES]


def _slab_pitch(d_model):
    return d_model // LANES + 1


def rmsnorm_slab(x, g, bm=256):
    M, D = x.shape
    pitch = _slab_pitch(D)
    return pl.pallas_call(
        _rmsnorm_slab_body,
        grid=(M // bm,),
        in_specs=[pl.BlockSpec((bm, D), lambda i: (i, 0)), pl.BlockSpec((1, D), lambda i: (0, 0))],
        out_specs=[pl.BlockSpec((bm, D), lambda i: (i, 0)), pl.BlockSpec((bm * pitch, LANES), lambda i: (i, 0))],
        out_shape=[jax.ShapeDtypeStruct((M, D), BF16), jax.ShapeDtypeStruct((M * pitch, LANES), F32)],
        compiler_params=_cparams("parallel"),
        name="rmsnorm_slab",
    )(x, g.reshape(1, D))


def _mm_acc_body(x_ref, w_ref, acc_ref, o_ref, wbf_ref):
    @pl.when(pl.program_id(1) == 0)
    def _():
        wbf_ref[...] = w_ref[...].astype(BF16)

    o_ref[...] = acc_ref[...] + _dot(x_ref[...], wbf_ref[...])


def matmul_acc(x, w, layer, acc, bm, bn):
    M, K = x.shape
    N = acc.shape[1]
    tile = pl.BlockSpec((bm, bn), lambda j, i: (i, j))
    return pl.pallas_call(
        _mm_acc_body,
        grid=(N // bn, M // bm),
        in_specs=[pl.BlockSpec((bm, K), lambda j, i: (i, 0)),
                  pl.BlockSpec((None, K, bn), lambda j, i: (layer, 0, j)), tile],
        out_specs=tile,
        out_shape=jax.ShapeDtypeStruct((M, N), F32),
        scratch_shapes=[pltpu.VMEM((K, bn), BF16)],
        compiler_params=_cparams("parallel", "arbitrary"),
        name="matmul_acc",
    )(x, w, acc)


def _mm_nt_body(x_ref, wt_ref, o_ref, wbf_ref):
    @pl.when(pl.program_id(1) == 0)
    def _():
        wbf_ref[...] = wt_ref[...].astype(BF16)

    o_ref[...] = _dot_nt(x_ref[...], wbf_ref[...]).astype(o_ref.dtype)


def matmul_nt(x, wt, layer, n_cols, out_dtype, bm, bn):
    M, K = x.shape
    return pl.pallas_call(
        _mm_nt_body,
        grid=(n_cols // bn, M // bm),
        in_specs=[pl.BlockSpec((bm, K), lambda j, i: (i, 0)),
                  pl.BlockSpec((None, bn, K), lambda j, i: (layer, j, 0))],
        out_specs=pl.BlockSpec((bm, bn), lambda j, i: (i, j)),
        out_shape=jax.ShapeDtypeStruct((M, n_cols), out_dtype),
        scratch_shapes=[pltpu.VMEM((bn, K), BF16)],
        compiler_params=_cparams("parallel", "arbitrary"),
        name="matmul_nt",
    )(x, wt)


MM_TAIL = 128


def _mm_nt_shifted_body(x_ref, wa_ref, wb_ref, o_ref, wbf_ref, *, shift):
    @pl.when(pl.program_id(1) == 0)
    def _():
        keep = wa_ref.shape[0] - shift
        wbf_ref[:keep, :] = wa_ref[shift:, :].astype(BF16)
        wbf_ref[keep:, :] = wb_ref[:shift, :].astype(BF16)

    o_ref[...] = _dot_nt(x_ref[...], wbf_ref[...]).astype(o_ref.dtype)


def matmul_nt_shifted(x, wt, layer, row0, n_cols, out_dtype, bm, bn):
    M, K = x.shape
    base, shift = divmod(row0, MM_TAIL)
    assert (base * MM_TAIL) % bn == 0 and 0 < shift and shift % 16 == 0
    ja = base * MM_TAIL // bn
    per = bn // MM_TAIL
    return pl.pallas_call(
        functools.partial(_mm_nt_shifted_body, shift=shift),
        grid=(n_cols // bn, M // bm),
        in_specs=[pl.BlockSpec((bm, K), lambda j, i: (i, 0)),
                  pl.BlockSpec((None, bn, K), lambda j, i: (layer, ja + j, 0)),
                  pl.BlockSpec((None, MM_TAIL, K), lambda j, i: (layer, base + per * (j + 1), 0))],
        out_specs=pl.BlockSpec((bm, bn), lambda j, i: (i, j)),
        out_shape=jax.ShapeDtypeStruct((M, n_cols), out_dtype),
        scratch_shapes=[pltpu.VMEM((bn, K), BF16)],
        compiler_params=_cparams("parallel", "arbitrary"),
        name="matmul_nt_shifted",
    )(x, wt, wt)


GDN_ROWS = 256
GDN_HB = 8
GDN_PREV = 16


def _gdn_gates_body(ab_ref, alog_ref, dtb_ref, o_ref):
    ab = ab_ref[...]
    lane = lax.broadcasted_iota(jnp.int32, ab.shape, 1)
    g = -jnp.exp(alog_ref[...]) * _softplus(ab + dtb_ref[...])
    o_ref[...] = jnp.where(lane < A_HEADS, g, _sigmoid(ab))


def gdn_gates(p_s, a_log, dt_bias, bm=512):
    S = p_s.shape[0]
    pad = lambda a: jnp.pad(a.astype(F32), (0, LANES - A_HEADS)).reshape(1, LANES)
    vec = pl.BlockSpec((1, LANES), lambda i: (0, 0))
    return pl.pallas_call(
        _gdn_gates_body,
        grid=(S // bm,),
        in_specs=[pl.BlockSpec((bm, LANES), lambda i: (i, PS_AB // LANES)), vec, vec],
        out_specs=pl.BlockSpec((bm, LANES), lambda i: (i, 0)),
        out_shape=jax.ShapeDtypeStruct((S, LANES), F32),
        compiler_params=_cparams("parallel"),
        name="gdn_gates",
    )(p_s, pad(a_log), pad(dt_bias))


def _gdn_body(q_ref, k_ref, v_ref, qp_ref, kp_ref, vp_ref, z_ref, gb_ref, cwq_ref, cwk_ref, cwv_ref,
              nw_ref, o_ref, state_ref):
    hb = pl.program_id(0)
    i = pl.program_id(1)
    R = GDN_ROWS
    C = A_CHUNK
    NC = R // C

    @pl.when(i == 0)
    def _():
        state_ref[...] = jnp.zeros_like(state_ref)

    has_prev = (i > 0).astype(F32)
    lane = lax.broadcasted_iota(jnp.int32, (R, LANES), 1)
    row = lax.broadcasted_iota(jnp.int32, (R, LANES), 0)
    rin = row & (C - 1)
    ri = lax.broadcasted_iota(jnp.int32, (R, R), 0)
    ci = lax.broadcasted_iota(jnp.int32, (R, R), 1)
    same = (ri // C) == (ci // C)
    incl = same & (ri >= ci)
    strict = same & (ri > ci)

    g_all = beta_all = gb_ref[...]
    nw = nw_ref[...]

    def conv_silu(cur_ref, prev_ref, cw_ref, sl):
        cur = cur_ref[:, sl].astype(F32)
        prev = prev_ref[:, sl].astype(F32) * has_prev
        full = jnp.concatenate([prev, cur], axis=0)
        cw = cw_ref[:, sl]
        y = cw[A_CONV - 1:A_CONV, :] * cur
        for s in range(1, A_CONV):
            y = y + cw[A_CONV - 1 - s:A_CONV - s, :] * pltpu.roll(full, s, axis=0)[GDN_PREV:, :]
        return _silu(y)

    heads = range(GDN_HB)
    lanes_of = [slice(hh * LANES, (hh + 1) * LANES) for hh in heads]
    st = [dict() for _ in heads]
    for hh, t in enumerate(st):
        h = hb * GDN_HB + hh
        g = jnp.broadcast_to(jnp.sum(jnp.where(lane == h, g_all, 0.0), axis=1, keepdims=True), (R, LANES))
        t["beta"] = jnp.broadcast_to(
            jnp.sum(jnp.where(lane == h + A_HEADS, beta_all, 0.0), axis=1, keepdims=True), (R, LANES))
        gc = g
        s = 1
        while s < C:
            gc = gc + jnp.where(rin >= s, pltpu.roll(gc, s, axis=0), 0.0)
            s *= 2
        t["g_last"] = jnp.broadcast_to(jnp.sum(g.reshape(NC, C, LANES), axis=1, keepdims=True),
                                       (NC, C, LANES)).reshape(R, LANES)
        t["gc"] = gc
        t["eg"] = jnp.exp(gc)
        gc_t = gc.T
        diff = jnp.concatenate([gc, gc], axis=1) - jnp.concatenate([gc_t, gc_t], axis=0)
        t["decay"] = jnp.exp(jnp.where(incl, diff, NEG))
    for hh, t in enumerate(st):
        sl = lanes_of[hh]
        q = conv_silu(q_ref, qp_ref, cwq_ref, sl)
        k = conv_silu(k_ref, kp_ref, cwk_ref, sl)
        t["v"] = conv_silu(v_ref, vp_ref, cwv_ref, sl)
        t["qn"] = q * lax.rsqrt(jnp.sum(q * q, axis=-1, keepdims=True) + EPS) * (A_DK ** -0.5)
        t["kn"] = k * lax.rsqrt(jnp.sum(k * k, axis=-1, keepdims=True) + EPS)
    for t in st:
        kn, qn, beta, eg, decay = t["kn"], t["qn"], t["beta"], t["eg"], t["decay"]
        kb = kn * beta
        kn_b = kn.astype(BF16)
        lower = jnp.where(strict, _dot_nt(kb.astype(BF16), kn_b) * decay, 0.0)
        t["xp"] = t["qmat"] = -lower
        t["qk"] = (_dot_nt(qn.astype(BF16), kn_b) * decay).astype(BF16)
        t["rhs"] = jnp.concatenate([t["v"] * beta, kb * eg], axis=1)
        t["q_dec"] = (qn * eg).astype(BF16)
        t["k_dec"] = kn * jnp.exp(t["g_last"] - t["gc"])

    for _ in range(int(math.log2(C)) - 1):
        for t in st:
            xb = t["xp"].astype(BF16)
            t["xp"] = _dot(xb, xb)
            t["qmat"] = t["qmat"] + t["xp"] + _dot(t["qmat"].astype(BF16), t["xp"].astype(BF16))
    for t in st:
        sol = t["rhs"] + _dot(t["qmat"].astype(BF16), t["rhs"].astype(BF16))
        t["u"] = sol[:, :A_DV]
        t["w_b"] = sol[:, A_DV:].astype(BF16)
        t["outs"] = []

    states = [state_ref[hh] for hh in heads]
    for c in range(NC):
        rs = slice(c * C, (c + 1) * C)
        for hh, t in enumerate(st):
            sb = states[hh].astype(BF16)
            v_new_b = (t["u"][rs] - _dot(t["w_b"][rs], sb)).astype(BF16)
            t["outs"].append(_dot(t["q_dec"][rs], sb) + _dot(t["qk"][rs, :][:, rs], v_new_b))
            states[hh] = (states[hh] * jnp.exp(t["g_last"][c * C:c * C + 1, :])
                          + _dot(t["k_dec"][rs].T.astype(BF16), v_new_b))
    for hh, t in enumerate(st):
        sl = lanes_of[hh]
        state_ref[hh] = states[hh]
        out = jnp.concatenate(t["outs"], axis=0)
        out = out * lax.rsqrt(jnp.mean(out * out, axis=-1, keepdims=True) + EPS) * nw
        o_ref[:, sl] = (out * _silu(z_ref[:, sl].astype(F32))).astype(o_ref.dtype)


def gdn_branch(p_a, p_s, conv_w, a_log, dt_bias, norm_w):
    S = p_a.shape[0]
    R, W = GDN_ROWS, GDN_HB * LANES
    nq = A_QK_W // W
    gb = gdn_gates(p_s, a_log, dt_bias)

    def cur(off):
        return pl.BlockSpec((R, W), lambda hb, i: (i, off + hb))

    def prev(off):
        return pl.BlockSpec((GDN_PREV, W), lambda hb, i: (jnp.maximum(i * (R // GDN_PREV) - 1, 0), off + hb))

    def cw(off):
        return pl.BlockSpec((A_CONV, W), lambda hb, i: (0, off + hb))

    vec = pl.BlockSpec((1, LANES), lambda hb, i: (0, 0))
    return pl.pallas_call(
        _gdn_body,
        grid=(A_HEADS // GDN_HB, S // R),
        in_specs=[cur(0), cur(nq), cur(2 * nq), prev(0), prev(nq), prev(2 * nq), cur(3 * nq),
                  pl.BlockSpec((R, LANES), lambda hb, i: (i, 0)),
                  cw(0), cw(nq), cw(2 * nq), vec],
        out_specs=pl.BlockSpec((R, W), lambda hb, i: (i, hb)),
        out_shape=jax.ShapeDtypeStruct((S, A_V_W), BF16),
        scratch_shapes=[pltpu.VMEM((GDN_HB, A_DK, A_DV), F32)],
        compiler_params=_cparams("parallel", "arbitrary"),
        name="gdn",
    )(p_a, p_a, p_a, p_a, p_a, p_a, p_a, gb, conv_w, conv_w, conv_w, norm_w.astype(F32).reshape(1, LANES))


MLA_HG = 4


def _mla_prep_body(cq_ref, ckv_ref, kr_ref, qn_ref, kvn_ref, wq_ref, wkv_ref, cs_ref, qo_ref, ko_ref, vo_ref):
    scale = (B_NOPE + B_ROPE) ** -0.5 * math.log2(math.e)
    cq = cq_ref[...]
    cq = (cq * lax.rsqrt(jnp.mean(cq * cq, axis=-1, keepdims=True) + EPS) * qn_ref[...]).astype(BF16)
    ckv = ckv_ref[...]
    ckv = (ckv * lax.rsqrt(jnp.mean(ckv * ckv, axis=-1, keepdims=True) + EPS) * kvn_ref[...]).astype(BF16)
    rq = _dot(cq, wq_ref[...])
    rkv = _dot(ckv, wkv_ref[...])
    cs = cs_ref[...]
    lane = lax.broadcasted_iota(jnp.int32, cs.shape, 1)

    def rope(pair):
        prod = pair * cs
        return jnp.where(lane < B_ROPE, prod + pltpu.roll(prod, B_ROPE, axis=1), 0.0)

    k_pe = rope(kr_ref[...]).astype(BF16)
    for hh in range(MLA_HG):
        qb = hh * B_QH
        qo_ref[:, hh * B_KH:hh * B_KH + LANES] = (rq[:, qb:qb + LANES] * scale).astype(BF16)
        qo_ref[:, hh * B_KH + LANES:(hh + 1) * B_KH] = (rope(rq[:, qb + LANES:qb + 2 * LANES]) * scale).astype(BF16)
        ko_ref[:, hh * B_KH:hh * B_KH + LANES] = rkv[:, hh * 2 * LANES:hh * 2 * LANES + LANES].astype(BF16)
        ko_ref[:, hh * B_KH + LANES:(hh + 1) * B_KH] = k_pe
        vo_ref[:, hh * 2 * LANES:hh * 2 * LANES + LANES] = (
            rkv[:, hh * 2 * LANES + LANES:(hh + 1) * 2 * LANES].astype(BF16))
        vo_ref[:, hh * 2 * LANES + LANES:(hh + 1) * 2 * LANES] = jnp.ones((rkv.shape[0], LANES), BF16)


def mla_prep(p_s, q_norm, kv_norm, wq_ext, wkv, cs_b, bm=512):
    S = p_s.shape[0]
    HG = MLA_HG
    return pl.pallas_call(
        _mla_prep_body,
        grid=(B_HEADS // HG, S // bm),
        in_specs=[
            pl.BlockSpec((bm, B_Q_LORA), lambda g, i: (i, PS_CQ // B_Q_LORA)),
            pl.BlockSpec((bm, B_KV_LORA), lambda g, i: (i, PS_CKV // B_KV_LORA)),
            pl.BlockSpec((bm, LANES), lambda g, i: (i, PS_KR // LANES)),
            pl.BlockSpec((1, B_Q_LORA), lambda g, i: (0, 0)),
            pl.BlockSpec((1, B_KV_LORA), lambda g, i: (0, 0)),
            pl.BlockSpec((B_Q_LORA, HG * B_QH), lambda g, i: (0, g)),
            pl.BlockSpec((B_KV_LORA, HG * 2 * LANES), lambda g, i: (0, g)),
            pl.BlockSpec((bm, LANES), lambda g, i: (i, 0)),
        ],
        out_specs=[
            pl.BlockSpec((bm, HG * B_KH), lambda g, i: (i, g)),
            pl.BlockSpec((bm, HG * B_KH), lambda g, i: (i, g)),
            pl.BlockSpec((bm, HG * 2 * LANES), lambda g, i: (i, g)),
        ],
        out_shape=[
            jax.ShapeDtypeStruct((S, B_HEADS * B_KH), BF16),
            jax.ShapeDtypeStruct((S, B_HEADS * B_KH), BF16),
            jax.ShapeDtypeStruct((S, B_HEADS * 2 * B_VDIM), BF16),
        ],
        compiler_params=_cparams("parallel", "parallel"),
        name="mla_prep",
    )(p_s, p_s, p_s, q_norm.reshape(1, -1), kv_norm.reshape(1, -1), wq_ext, wkv, cs_b)


FLASH_BLK = 2048
FLASH_RC = 256


def _flash_body(qi_ref, kj_ref, q_ref, k_ref, v_ref, o_ref, m_ref, acc_ref):
    t = pl.program_id(1)
    qi = qi_ref[t]
    kj = kj_ref[t]

    @pl.when(kj == 0)
    def _():
        m_ref[...] = jnp.full_like(m_ref, NEG)
        acc_ref[...] = jnp.zeros_like(acc_ref)

    def step(diag):
        nc = FLASH_BLK // FLASH_RC
        rows_of = [slice(c * FLASH_RC, (c + 1) * FLASH_RC) for c in range(nc)]
        ncol_of = [(c + 1) * FLASH_RC if diag else FLASH_BLK for c in range(nc)]

        def scores(c):
            s = _dot_nt(q_ref[rows_of[c], :], k_ref[:ncol_of[c], :])
            if diag:
                r = lax.broadcasted_iota(jnp.int32, s.shape, 0) + c * FLASH_RC
                col = lax.broadcasted_iota(jnp.int32, s.shape, 1)
                s = jnp.where(col <= r, s, NEG)
            return s

        s_next = scores(0)
        for c in range(nc):
            s = s_next
            if c + 1 < nc:
                s_next = scores(c + 1)
            rows = rows_of[c]
            m_prev = m_ref[rows, :]
            m_new = jnp.maximum(m_prev, jnp.max(s, axis=1, keepdims=True))
            p = jnp.exp2(s - m_new).astype(BF16)
            acc_ref[rows, :] = jnp.exp2(m_prev - m_new) * acc_ref[rows, :] + _dot(p, v_ref[:ncol_of[c], :])
            m_ref[rows, :] = m_new

    @pl.when(kj < qi)
    def _():
        step(False)

    @pl.when(kj == qi)
    def _():
        step(True)
        acc = acc_ref[...]
        o_ref[...] = (acc[:, :B_VDIM] / acc[:, B_VDIM:]).astype(o_ref.dtype)


def mla_attention(q_cat, k_cat, v_ext):
    S = q_cat.shape[0]
    blk = FLASH_BLK
    n = S // blk
    pairs = [(qi, kj) for qi in range(n) for kj in range(qi + 1)]
    qi_tab = jnp.asarray([p[0] for p in pairs], jnp.int32)
    kj_tab = jnp.asarray([p[1] for p in pairs], jnp.int32)
    return pl.pallas_call(
        _flash_body,
        grid_spec=pltpu.PrefetchScalarGridSpec(
            num_scalar_prefetch=2,
            grid=(B_HEADS, len(pairs)),
            in_specs=[
                pl.BlockSpec((blk, B_KH), lambda h, t, qi, kj: (qi[t], h)),
                pl.BlockSpec((blk, B_KH), lambda h, t, qi, kj: (kj[t], h)),
                pl.BlockSpec((blk, 2 * B_VDIM), lambda h, t, qi, kj: (kj[t], h)),
            ],
            out_specs=pl.BlockSpec((blk, B_VDIM), lambda h, t, qi, kj: (qi[t], h)),
            scratch_shapes=[pltpu.VMEM((blk, 1), F32), pltpu.VMEM((blk, 2 * B_VDIM), F32)],
        ),
        out_shape=jax.ShapeDtypeStruct((S, B_HEADS * B_VDIM), BF16),
        compiler_params=_cparams("parallel", "arbitrary"),
        name="mla_flash",
    )(qi_tab, kj_tab, q_cat, k_cat, v_ext)


DIL_BLK = 128
DIL_ROWS = 256
DILATIONS = tuple(d for _, d in C_PATTERNS)


def _rope_split_body(q_ref, k_ref, v_ref, c_ref, s_ref, *refs):
    nd = len(DILATIONS)
    outs, (sq, sk, sv) = refs[:3 * nd], refs[3 * nd:]
    c = c_ref[...]
    s = s_ref[...]
    scale = C_DH ** -0.5
    for h in range(C_HEADS):
        sl = slice(h * C_DH, (h + 1) * C_DH)
        x = q_ref[:, sl].astype(F32)
        sq[h] = (x * c + pltpu.roll(x, C_DH // 2, axis=1) * s) * scale
        x = k_ref[:, sl].astype(F32)
        sk[h] = x * c + pltpu.roll(x, C_DH // 2, axis=1) * s
        sv[h] = v_ref[:, sl].astype(F32)
        for di, d in enumerate(DILATIONS):
            for src, dst in zip((sq, sk, sv), outs[3 * di:3 * di + 3]):
                for r in range(d):
                    rows = src[h] if d == 1 else src[h, pl.ds(r, DIL_ROWS // d, stride=d), :]
                    dst[r, :, sl] = rows.astype(BF16)


def rope_split(p_r, cos_c, sin_c):
    S = p_r.shape[0]
    bm = DIL_ROWS
    tab = pl.BlockSpec((bm, C_DH), lambda i: (i, 0))
    col = lambda j: pl.BlockSpec((bm, C_W), lambda i: (i, j))
    out_specs, out_shape = [], []
    for d in DILATIONS:
        out_specs += [pl.BlockSpec((d, bm // d, C_W), lambda i: (0, i, 0))] * 3
        out_shape += [jax.ShapeDtypeStruct((d, S // d, C_W), BF16)] * 3
    outs = pl.pallas_call(
        _rope_split_body,
        grid=(S // bm,),
        in_specs=[col(0), col(1), col(2), tab, tab],
        out_specs=out_specs,
        out_shape=out_shape,
        scratch_shapes=[pltpu.VMEM((C_HEADS, bm, C_DH), F32)] * 3,
        compiler_params=_cparams("parallel"),
        name="rope_split",
    )(p_r, p_r, p_r, cos_c, sin_c)
    return [outs[3 * i:3 * i + 3] for i in range(len(DILATIONS))]


def _dilated_body(q_ref, kc_ref, vc_ref, o_ref, lse_ref, kp_ref, vp_ref):
    B = DIL_BLK
    n = pl.program_id(1)

    @pl.when(n == 0)
    def _():
        kp_ref[...] = jnp.zeros_like(kp_ref)
        vp_ref[...] = jnp.zeros_like(vp_ref)

    qi = lax.broadcasted_iota(jnp.int32, (B, 2 * B), 0)
    kj = lax.broadcasted_iota(jnp.int32, (B, 2 * B), 1)
    dist = qi + B - kj
    valid = (dist >= 0) & (dist <= B) & ((kj >= B) | (n > 0))
    lanes_of = [slice(h * C_DH, (h + 1) * C_DH) for h in range(C_HEADS)]
    scores = [jnp.where(valid, _dot_nt(q_ref[:, sl], jnp.concatenate([kp_ref[:, sl], kc_ref[:, sl]], axis=0)), NEG)
              for sl in lanes_of]
    probs = []
    lane = lax.broadcasted_iota(jnp.int32, (B, LANES), 1)
    lse = jnp.zeros((B, LANES), F32)
    for h, s in enumerate(scores):
        m = jnp.max(s, axis=1, keepdims=True)
        p = jnp.exp(s - m)
        l = jnp.sum(p, axis=1, keepdims=True)
        lse = jnp.where(lane == h, m + jnp.log(l), lse)
        probs.append((p.astype(BF16), l))
    lse_ref[...] = lse
    for sl, (p, l) in zip(lanes_of, probs):
        vv = jnp.concatenate([vp_ref[:, sl], vc_ref[:, sl]], axis=0)
        o_ref[:, sl] = (_dot(p, vv) / l).astype(o_ref.dtype)
    kp_ref[...] = kc_ref[...]
    vp_ref[...] = vc_ref[...]


def dilated_pattern(q, k, v):
    d, L, W = q.shape
    B = DIL_BLK
    cur = pl.BlockSpec((None, B, W), lambda r, n: (r, n, 0))
    return pl.pallas_call(
        _dilated_body,
        grid=(d, L // B),
        in_specs=[cur, cur, cur],
        out_specs=[cur, pl.BlockSpec((None, B, LANES), lambda r, n: (r, n, 0))],
        out_shape=[jax.ShapeDtypeStruct((d, L, W), BF16), jax.ShapeDtypeStruct((d, L, LANES), F32)],
        scratch_shapes=[pltpu.VMEM((B, W), BF16)] * 2,
        compiler_params=_cparams("arbitrary", "arbitrary"),
        name=f"dilated_d{d}",
    )(q, k, v)


def _dilated_merge_body(*refs):
    nd = len(DILATIONS)
    o_refs, lse_refs, out_ref = refs[:nd], refs[nd:2 * nd], refs[2 * nd]
    so, sl = refs[2 * nd + 1:]
    rows_of = lambda d, r: slice(None) if d == 1 else pl.ds(r, DIL_ROWS // d, stride=d)
    for di, d in enumerate(DILATIONS):
        for r in range(d):
            sl[di, rows_of(d, r), :] = lse_refs[di][r]
    top = sl[0]
    for di in range(1, nd):
        top = jnp.maximum(top, sl[di])
    weights = [jnp.exp(sl[di] - top) for di in range(nd)]
    lane = lax.broadcasted_iota(jnp.int32, top.shape, 1)
    for h in range(C_HEADS):
        cols = slice(h * C_DH, (h + 1) * C_DH)
        for di, d in enumerate(DILATIONS):
            for r in range(d):
                so[di, rows_of(d, r), :] = o_refs[di][r, :, cols].astype(F32)
        num = 0.0
        den = 0.0
        for di in range(nd):
            w = jnp.sum(jnp.where(lane == h, weights[di], 0.0), axis=1, keepdims=True)
            num = num + w * so[di]
            den = den + w
        out_ref[:, cols] = (num / den).astype(out_ref.dtype)


def dilated_merge(outs, lses):
    S = outs[0].shape[0] * outs[0].shape[1]
    bm = DIL_ROWS
    nd = len(DILATIONS)
    specs = [pl.BlockSpec((d, bm // d, C_W), lambda i: (0, i, 0)) for d in DILATIONS]
    lse_specs = [pl.BlockSpec((d, bm // d, LANES), lambda i: (0, i, 0)) for d in DILATIONS]
    return pl.pallas_call(
        _dilated_merge_body,
        grid=(S // bm,),
        in_specs=specs + lse_specs,
        out_specs=pl.BlockSpec((bm, C_W), lambda i: (i, 0)),
        out_shape=jax.ShapeDtypeStruct((S, C_W), BF16),
        scratch_shapes=[pltpu.VMEM((nd, bm, C_DH), F32)] * 2,
        compiler_params=_cparams("parallel"),
        name="dilated_merge",
    )(*outs, *lses)


def dilated_branch(p_r, cos_c, sin_c):
    assert all(w // d == DIL_BLK for w, d in C_PATTERNS)
    results = [dilated_pattern(q, k, v) for q, k, v in rope_split(p_r, cos_c, sin_c)]
    return dilated_merge([r[0] for r in results], [r[1] for r in results])


def _merge_body(oa_ref, ob_ref, oc_ref, ga_ref, gb_ref, gc_ref, wa_ref, wb_ref, wc_ref, o_ref, wa_s, wb_s, wc_s):
    @pl.when(pl.program_id(1) == 0)
    def _():
        wa_s[...] = wa_ref[...].astype(BF16)
        wb_s[...] = wb_ref[...].astype(BF16)
        wc_s[...] = wc_ref[...].astype(BF16)

    gates = [_sigmoid(g[...].astype(F32)) for g in (ga_ref, gb_ref, gc_ref)]
    ya = _dot(oa_ref[...], wa_s[...])
    yb = _dot(ob_ref[...], wb_s[...])
    yc = _dot(oc_ref[...], wc_s[...])
    o_ref[...] = (gates[0] * ya + gates[1] * yb + gates[2] * yc).astype(o_ref.dtype)


def merge_branches(o_a, o_b, o_c, p_r, w_a, w_b, w_c, layer, bm=512, bn=512):
    S, K = o_a.shape
    g0 = 3 * C_W // bn
    gstep = D_MODEL // bn
    act = pl.BlockSpec((bm, K), lambda j, i: (i, 0))
    wsp = pl.BlockSpec((None, K, bn), lambda j, i: (layer, 0, j))
    gate = lambda b: pl.BlockSpec((bm, bn), lambda j, i: (i, g0 + b * gstep + j))
    return pl.pallas_call(
        _merge_body,
        grid=(D_MODEL // bn, S // bm),
        in_specs=[act, act, act, gate(0), gate(1), gate(2), wsp, wsp, wsp],
        out_specs=pl.BlockSpec((bm, bn), lambda j, i: (i, j)),
        out_shape=jax.ShapeDtypeStruct((S, D_MODEL), BF16),
        scratch_shapes=[pltpu.VMEM((K, bn), BF16)] * 3,
        compiler_params=_cparams("parallel", "arbitrary"),
        name="merge_branches",
    )(o_a, o_b, o_c, p_r, p_r, p_r, w_a, w_b, w_c)


def _new_group(tg_ref):
    i = pl.program_id(1)
    return (i == 0) | (tg_ref[i] != tg_ref[jnp.maximum(i - 1, 0)])


def _swiglu_up_body(tg_ref, nt_ref, x_ref, w1_ref, w3_ref, o_ref, w1_s, w3_s):
    @pl.when(_new_group(tg_ref))
    def _():
        w1_s[...] = w1_ref[...].astype(BF16)
        w3_s[...] = w3_ref[...].astype(BF16)

    @pl.when(pl.program_id(1) < nt_ref[0])
    def _():
        x = x_ref[...]
        o_ref[...] = (_silu(_dot(x, w1_s[...])) * _dot(x, w3_s[...])).astype(o_ref.dtype)

    @pl.when(pl.program_id(1) >= nt_ref[0])
    def _():
        o_ref[...] = jnp.zeros_like(o_ref)


def swiglu_up(x, w1, w3, tile_group, n_tiles, bm, bf):
    R, D = x.shape
    F = w1.shape[2]
    wspec = pl.BlockSpec((None, D, bf), lambda f, i, tg, nt: (tg[i], 0, f))
    return pl.pallas_call(
        _swiglu_up_body,
        grid_spec=pltpu.PrefetchScalarGridSpec(
            num_scalar_prefetch=2,
            grid=(F // bf, R // bm),
            in_specs=[pl.BlockSpec((bm, D), lambda f, i, tg, nt: (jnp.minimum(i, nt[0] - 1), 0)), wspec, wspec],
            out_specs=pl.BlockSpec((bm, bf), lambda f, i, tg, nt: (i, f)),
            scratch_shapes=[pltpu.VMEM((D, bf), BF16)] * 2,
        ),
        out_shape=jax.ShapeDtypeStruct((R, F), BF16),
        compiler_params=_cparams("arbitrary", "arbitrary"),
        name="swiglu_up",
    )(tile_group, n_tiles, x, w1, w3)


def _grouped_mm_body(tg_ref, nt_ref, x_ref, w_ref, *rest, has_acc):
    if has_acc:
        acc_ref, o_ref, w_s = rest
    else:
        o_ref, w_s = rest

    @pl.when(_new_group(tg_ref))
    def _():
        w_s[...] = w_ref[...].astype(BF16)

    @pl.when(pl.program_id(1) < nt_ref[0])
    def _():
        y = _dot(x_ref[...], w_s[...])
        o_ref[...] = acc_ref[...] + y if has_acc else y

    @pl.when(pl.program_id(1) >= nt_ref[0])
    def _():
        o_ref[...] = jnp.zeros_like(o_ref)


def grouped_matmul(x, w, tile_group, n_tiles, bm, bn, k_blocks=1, k_index=0, acc=None):
    R, K = x.shape
    N = w.shape[2]
    kb = K // k_blocks
    in_specs = [pl.BlockSpec((bm, kb), lambda j, i, tg, nt: (jnp.minimum(i, nt[0] - 1), k_index)),
                pl.BlockSpec((None, kb, bn), lambda j, i, tg, nt: (tg[i], k_index, j))]
    out_spec = pl.BlockSpec((bm, bn), lambda j, i, tg, nt: (i, j))
    args = [tile_group, n_tiles, x, w]
    if acc is not None:
        in_specs.append(out_spec)
        args.append(acc)
    return pl.pallas_call(
        functools.partial(_grouped_mm_body, has_acc=acc is not None),
        grid_spec=pltpu.PrefetchScalarGridSpec(
            num_scalar_prefetch=2,
            grid=(N // bn, R // bm),
            in_specs=in_specs,
            out_specs=out_spec,
            scratch_shapes=[pltpu.VMEM((kb, bn), BF16)],
        ),
        out_shape=jax.ShapeDtypeStruct((R, N), F32),
        input_output_aliases={4: 0} if acc is not None else {},
        compiler_params=_cparams("arbitrary", "arbitrary"),
        name="grouped_matmul",
    )(*args)


DENSE_UP_BM = 1024
DENSE_DOWN_BM = 512
DENSE_KSPLIT = 2


def _one_group(n_rows, bm, group):
    return jnp.full((n_rows // bm,), group, jnp.int32), jnp.full((1,), n_rows // bm, jnp.int32)


def dense_ffn(h, w1, w3, w2, index, x_res):
    S = h.shape[0]
    g = swiglu_up(h, w1, w3, *_one_group(S, DENSE_UP_BM, index), bm=DENSE_UP_BM, bf=256)
    y = x_res
    for k in range(DENSE_KSPLIT):
        y = grouped_matmul(g, w2, *_one_group(S, DENSE_DOWN_BM, index), bm=DENSE_DOWN_BM, bn=512,
                           k_blocks=DENSE_KSPLIT, k_index=k, acc=y)
    return y


def _router_body(h_ref, rw_ref, sel_ref, gate_ref):
    logits = _dot(h_ref[...].astype(BF16), rw_ref[...].astype(BF16))
    lane = lax.broadcasted_iota(jnp.int32, logits.shape, 1)
    logits = jnp.where(lane < N_EXPERTS, logits, NEG)
    t1 = jnp.max(logits, axis=1, keepdims=True)
    i1 = jnp.min(jnp.where(logits == t1, lane, LANES), axis=1, keepdims=True)
    rest = jnp.where(lane == i1, NEG, logits)
    t2 = jnp.max(rest, axis=1, keepdims=True)
    i2 = jnp.min(jnp.where(rest == t2, lane, LANES), axis=1, keepdims=True)
    e2 = jnp.exp(t2 - t1)
    g1 = 1.0 / (1.0 + e2)
    sel_ref[...] = jnp.where(lane == 0, i1, jnp.where(lane == 1, i2, 0))
    gate_ref[...] = jnp.where(lane == 0, g1, jnp.where(lane == 1, e2 * g1, 0.0))


def router(h, router_w, bm=512):
    S, D = h.shape
    rw = jnp.pad(router_w, ((0, 0), (0, LANES - N_EXPERTS)))
    out = pl.BlockSpec((bm, LANES), lambda i: (i, 0))
    return pl.pallas_call(
        _router_body,
        grid=(S // bm,),
        in_specs=[pl.BlockSpec((bm, D), lambda i: (i, 0)), pl.BlockSpec((D, LANES), lambda i: (0, 0))],
        out_specs=[out, out],
        out_shape=[jax.ShapeDtypeStruct((S, LANES), jnp.int32), jax.ShapeDtypeStruct((S, LANES), F32)],
        compiler_params=_cparams("parallel"),
        name="router",
    )(h, rw)


MOE_BM = 512
MOE_BG = 256
DMA_QUEUES = 2


def _moe_plan(sel, bm):
    S = sel.shape[0]
    E = N_EXPERTS
    e = sel[:, :TOP_K].reshape(-1)
    onehot = (e[:, None] == jnp.arange(E, dtype=jnp.int32)[None, :]).astype(jnp.int32)
    csum = jnp.cumsum(onehot, axis=0)
    rank = jnp.sum((csum - 1) * onehot, axis=1)
    tiles_per = (csum[-1] + bm - 1) // bm
    tile_end = jnp.cumsum(tiles_per)
    pos = ((tile_end - tiles_per)[e] * bm + rank).astype(jnp.int32)
    n_tiles = tile_end[-1:].astype(jnp.int32)
    T = (TOP_K * S) // bm + E
    tid = jnp.arange(T, dtype=jnp.int32)
    te = jnp.minimum(jnp.sum((tid[:, None] >= tile_end[None, :]).astype(jnp.int32), axis=1), E - 1)
    te = jnp.where(tid < n_tiles[0], te, te[n_tiles[0] - 1]).astype(jnp.int32)
    src = jnp.zeros((T * bm,), jnp.int32).at[pos].set(jnp.arange(TOP_K * S, dtype=jnp.int32) // TOP_K)
    return pos, src, te, n_tiles


def _moe_gather_body(src_ref, slab_hbm, o_ref, buf_ref, sem):
    base = pl.program_id(0) * MOE_BG
    tiles = o_ref.shape[1] // LANES
    pitch = _slab_pitch(o_ref.shape[1])

    def row_copy(j):
        src_rows = pl.ds(src_ref[base + j] * pitch, tiles)
        return pltpu.make_async_copy(slab_hbm.at[src_rows, :], buf_ref.at[pl.ds(j * pitch, tiles), :], sem)

    def start(jj, c):
        for par in range(DMA_QUEUES):
            row_copy(jj * DMA_QUEUES + par).start(priority=par)
        return c

    def wait(j, c):
        row_copy(j).wait()
        return c

    lax.fori_loop(0, MOE_BG // DMA_QUEUES, start, 0)
    lax.fori_loop(0, MOE_BG, wait, 0)
    for c in range(tiles):
        o_ref[:, c * LANES:(c + 1) * LANES] = buf_ref[pl.ds(c, MOE_BG, stride=pitch), :].astype(o_ref.dtype)


def moe_gather(slab, src, d_model):
    R = src.shape[0]
    return pl.pallas_call(
        _moe_gather_body,
        grid_spec=pltpu.PrefetchScalarGridSpec(
            num_scalar_prefetch=1,
            grid=(R // MOE_BG,),
            in_specs=[pl.BlockSpec(memory_space=pl.ANY)],
            out_specs=pl.BlockSpec((MOE_BG, d_model), lambda i, src: (i, 0)),
            scratch_shapes=[pltpu.VMEM((MOE_BG * _slab_pitch(d_model), LANES), F32), pltpu.SemaphoreType.DMA(())],
        ),
        out_shape=jax.ShapeDtypeStruct((R, d_model), BF16),
        compiler_params=_cparams("arbitrary"),
        name="moe_gather",
    )(src, slab)


def _moe_combine_body(pos_ref, gate_ref, x_ref, g_ref, ys_hbm, *rest, final):
    if final:
        ho_ref, buf_ref, sem = rest
    else:
        xo_ref, ho_ref, buf_ref, sem = rest
    base = pl.program_id(0) * MOE_BG

    def row_copy(j, k):
        row = pos_ref[(base + j) * TOP_K + k]
        return pltpu.make_async_copy(ys_hbm.at[pl.ds(row, 1), :], buf_ref.at[k, pl.ds(j, 1), :], sem)

    def start(j, c):
        for k in range(TOP_K):
            row_copy(j, k).start(priority=k % DMA_QUEUES)
        return c

    def wait(j, c):
        for k in range(TOP_K):
            row_copy(j, k).wait()
        return c

    lax.fori_loop(0, MOE_BG, start, 0)
    lax.fori_loop(0, MOE_BG, wait, 0)
    gate = gate_ref[...]
    x = x_ref[...]
    for k in range(TOP_K):
        x = x + gate[:, k:k + 1] * buf_ref[k]
    if not final:
        xo_ref[...] = x
    y = x * lax.rsqrt(jnp.mean(x * x, axis=-1, keepdims=True) + EPS)
    ho_ref[...] = (y * g_ref[...]).astype(ho_ref.dtype)


def moe_combine(ys, pos, gate, x_res, g, final):
    S, D = x_res.shape
    row = pl.BlockSpec((MOE_BG, D), lambda i, pos: (i, 0))
    normed = jax.ShapeDtypeStruct((S, D), F32 if final else BF16)
    return pl.pallas_call(
        functools.partial(_moe_combine_body, final=final),
        grid_spec=pltpu.PrefetchScalarGridSpec(
            num_scalar_prefetch=1,
            grid=(S // MOE_BG,),
            in_specs=[pl.BlockSpec((MOE_BG, LANES), lambda i, pos: (i, 0)), row,
                      pl.BlockSpec((1, D), lambda i, pos: (0, 0)), pl.BlockSpec(memory_space=pl.ANY)],
            out_specs=row if final else [row, row],
            scratch_shapes=[pltpu.VMEM((TOP_K, MOE_BG, D), F32), pltpu.SemaphoreType.DMA(())],
        ),
        out_shape=normed if final else [jax.ShapeDtypeStruct((S, D), F32), normed],
        compiler_params=_cparams("arbitrary"),
        name="moe_combine",
    )(pos, gate, x_res, g.reshape(1, D), ys)


def moe_layer(h, h_slab, router_w, w1, w3, w2, index, x_res, next_norm, final):
    sel, gate = router(h, router_w)
    pos, src, tile_expert, n_tiles = _moe_plan(sel, MOE_BM)
    tile_expert = tile_expert + index * N_EXPERTS
    w1, w3, w2 = (w.reshape((-1,) + w.shape[2:]) for w in (w1, w3, w2))
    xs = moe_gather(h_slab, src, h.shape[1])
    g = swiglu_up(xs, w1, w3, tile_expert, n_tiles, bm=MOE_BM, bf=512)
    ys = grouped_matmul(g, w2, tile_expert, n_tiles, bm=MOE_BM, bn=1024)
    return moe_combine(ys, pos, gate, x_res, next_norm, final)


def _rot_cols(w):
    half = w.shape[-1] // 2
    return jnp.concatenate([-w[..., half:], w[..., :half]], axis=-1)


def _rope_tables(seq, dim):
    inv = ROPE_THETA ** (-jnp.arange(0, dim, 2, dtype=F32) / dim)
    ang = jnp.arange(seq, dtype=F32)[:, None] * inv[None, :]
    return jnp.cos(ang), jnp.sin(ang)


def _small_weight_t(wt_mid):
    row = lambda off, n: wt_mid[off - OFF_A_IN:off - OFF_A_IN + n]
    kr = row(OFF_KROPE, B_ROPE)
    parts = [row(OFF_CQ, B_Q_LORA), row(OFF_CKV, B_KV_LORA), kr, _rot_cols(kr.T).T,
             row(OFF_A_IN, A_HEADS), row(OFF_B_IN, A_HEADS)]
    w = jnp.concatenate(parts, axis=0)
    return jnp.pad(w, ((0, PS_W - w.shape[0]), (0, 0)))


def _mla_weights(w_uq, w_ukv):
    wq = w_uq.reshape(B_Q_LORA, B_HEADS, B_NOPE + B_ROPE)
    pe = wq[..., B_NOPE:]
    wq_ext = jnp.concatenate([wq[..., :B_NOPE], pe, _rot_cols(pe)], axis=-1)
    return wq_ext.reshape(B_Q_LORA, B_HEADS * B_QH).astype(BF16), w_ukv.astype(BF16)


def hybrid_mixer(h, x_res, layer, p, tables):
    cs_b, cos_c, sin_c = tables
    wt_in = jnp.swapaxes(p["w_in"], 1, 2)
    p_a = matmul_nt(h, wt_in, layer, OFF_A_IN, BF16, bm=1024, bn=512)
    p_s = matmul_nt(h, _small_weight_t(wt_in[layer, OFF_A_IN:OFF_REST])[None], 0, PS_W, F32, bm=1024, bn=256)
    p_r = matmul_nt_shifted(h, wt_in, layer, OFF_REST, REST_W, BF16, bm=1024, bn=512)
    o_a = gdn_branch(p_a, p_s, p["conv_w"][layer], p["gdn_a_log"][layer], p["gdn_dt_bias"][layer],
                     p["gdn_norm"][layer])
    wq_ext, wkv = _mla_weights(p["mla_w_uq"][layer], p["mla_w_ukv"][layer])
    q_cat, k_cat, v_b = mla_prep(p_s, p["mla_q_norm"][layer], p["mla_kv_norm"][layer], wq_ext, wkv, cs_b)
    o_b = mla_attention(q_cat, k_cat, v_b)
    o_c = dilated_branch(p_r, cos_c, sin_c)
    merged = merge_branches(o_a, o_b, o_c, p_r, p["w_branch_a"], p["w_branch_b"], p["w_branch_c"], layer)
    return matmul_acc(merged, p["w_out"], layer, x_res, bm=1024, bn=512)


def kernel(x, attn_norm, w_in, conv_w, gdn_a_log, gdn_dt_bias, gdn_norm, mla_q_norm, mla_w_uq, mla_kv_norm, mla_w_ukv, w_branch_a, w_branch_b, w_branch_c, w_out, ffn_norm, dense_w1, dense_w3, dense_w2, router_w, moe_w1, moe_w3, moe_w2, final_norm):
    p = dict(w_in=w_in, conv_w=conv_w, gdn_a_log=gdn_a_log, gdn_dt_bias=gdn_dt_bias, gdn_norm=gdn_norm,
             mla_q_norm=mla_q_norm, mla_w_uq=mla_w_uq, mla_kv_norm=mla_kv_norm, mla_w_ukv=mla_w_ukv,
             w_branch_a=w_branch_a, w_branch_b=w_branch_b, w_branch_c=w_branch_c, w_out=w_out)
    Bn, S, D = x.shape
    assert Bn == 1 and D == D_MODEL
    cos_b, sin_b = _rope_tables(S, B_ROPE)
    cos_c, sin_c = _rope_tables(S, C_DH)
    tables = (jnp.concatenate([cos_b, cos_b, sin_b, sin_b], axis=1),
              jnp.concatenate([cos_c, cos_c], axis=1), jnp.concatenate([-sin_c, sin_c], axis=1))
    xr = x.reshape(S, D)
    h = rmsnorm(xr, attn_norm[0], BF16)
    out = None
    for layer in range(DEPTH):
        last = layer + 1 == DEPTH
        xr = hybrid_mixer(h, xr, layer, p, tables)
        i = layer // 2
        if layer % 2 == 0:
            h = rmsnorm(xr, ffn_norm[layer], BF16)
            xr = dense_ffn(h, dense_w1, dense_w3, dense_w2, i, xr)
            if last:
                out = rmsnorm(xr, final_norm, F32)
            else:
                h = rmsnorm(xr, attn_norm[layer + 1], BF16)
        else:
            h, h_slab = rmsnorm_slab(xr, ffn_norm[layer])
            moe = functools.partial(moe_layer, h, h_slab, router_w[i], moe_w1, moe_w3, moe_w2, i, xr)
            if last:
                out = moe(final_norm, True)
            else:
                xr, h = moe(attn_norm[layer + 1], False)
    return out.reshape(Bn, S, D)
```

```python
import functools
import math

import jax
import jax.numpy as jnp
from jax import lax
from jax.experimental import pallas as pl
from jax.experimental.pallas import tpu as pltpu

F32 = jnp.float32
BF16 = jnp.bfloat16

EPS = 1e-6
ROPE_THETA = 10000.0
NEG = -1e30

LANES = 128
VMEM_LIMIT_BYTES = 56 * 1024 * 1024

D_MODEL = 4096
DEPTH = 2
A_HEADS, A_DK, A_DV, A_CONV, A_CHUNK = 16, 128, 128, 4, 64
B_HEADS, B_Q_LORA, B_KV_LORA, B_NOPE, B_ROPE, B_VDIM = 16, 1024, 512, 128, 64, 128
C_HEADS, C_DH = 16, 128
C_PATTERNS = ((128, 1), (512, 4), (2048, 16))
D_FF = 11008
N_EXPERTS, TOP_K, D_EXPERT = 8, 2, 4096

A_QK_W = A_HEADS * A_DK
A_V_W = A_HEADS * A_DV
A_QKV_W = 2 * A_QK_W + A_V_W
C_W = C_HEADS * C_DH
OFF_A_IN = A_QKV_W + A_V_W
OFF_B_IN = OFF_A_IN + A_HEADS
OFF_CQ = OFF_B_IN + A_HEADS
OFF_CKV = OFF_CQ + B_Q_LORA
OFF_KROPE = OFF_CKV + B_KV_LORA
OFF_REST = OFF_KROPE + B_ROPE
REST_W = 3 * C_W + 3 * D_MODEL
PS_CQ, PS_CKV, PS_KR, PS_AB, PS_W = 0, 1024, 1536, 1664, 1792

B_QH = 256
B_KH = 256


def _cparams(*sem):
    return pltpu.CompilerParams(dimension_semantics=sem, vmem_limit_bytes=VMEM_LIMIT_BYTES)


def _dot(a, b):
    return jnp.dot(a, b, preferred_element_type=F32)


def _dot_nt(a, b):
    return lax.dot_general(a, b, (((1,), (1,)), ((), ())), preferred_element_type=F32)


def _sigmoid(x):
    return 1.0 / (1.0 + jnp.exp(-x))


def _silu(x):
    return x * _sigmoid(x)


def _softplus(x):
    return jnp.maximum(x, 0.0) + jnp.log(1.0 + jnp.exp(-jnp.abs(x)))


def _rmsnorm_body(x_ref, g_ref, o_ref):
    x = x_ref[...]
    y = x * lax.rsqrt(jnp.mean(x * x, axis=-1, keepdims=True) + EPS)
    o_ref[...] = (y * g_ref[...]).astype(o_ref.dtype)


def rmsnorm(x, g, out_dtype, bm=256):
    M, D = x.shape
    return pl.pallas_call(
        _rmsnorm_body,
        grid=(M // bm,),
        in_specs=[pl.BlockSpec((bm, D), lambda i: (i, 0)), pl.BlockSpec((1, D), lambda i: (0, 0))],
        out_specs=pl.BlockSpec((bm, D), lambda i: (i, 0)),
        out_shape=jax.ShapeDtypeStruct((M, D), out_dtype),
        compiler_params=_cparams("parallel"),
        name="rmsnorm",
    )(x, g.reshape(1, D))


def _rmsnorm_slab_body(x_ref, g_ref, ho_ref, slab_ref):
    x = x_ref[...]
    y = x * lax.rsqrt(jnp.mean(x * x, axis=-1, keepdims=True) + EPS) * g_ref[...]
    ho_ref[...] = y.astype(ho_ref.dtype)
    bm, D = x.shape
    pitch = _slab_pitch(D)
    slab_ref[...] = jnp.zeros_like(slab_ref)
    for c in range(D // LANES):
        slab_ref[pl.ds(c, bm, stride=pitch), :] = y[:, c * LANES:(c + 1) * LANES]


def _slab_pitch(d_model):
    return d_model // LANES + 1


def rmsnorm_slab(x, g, bm=256):
    M, D = x.shape
    pitch = _slab_pitch(D)
    return pl.pallas_call(
        _rmsnorm_slab_body,
        grid=(M // bm,),
        in_specs=[pl.BlockSpec((bm, D), lambda i: (i, 0)), pl.BlockSpec((1, D), lambda i: (0, 0))],
        out_specs=[pl.BlockSpec((bm, D), lambda i: (i, 0)), pl.BlockSpec((bm * pitch, LANES), lambda i: (i, 0))],
        out_shape=[jax.ShapeDtypeStruct((M, D), BF16), jax.ShapeDtypeStruct((M * pitch, LANES), F32)],
        compiler_params=_cparams("parallel"),
        name="rmsnorm_slab",
    )(x, g.reshape(1, D))


def _mm_acc_body(x_ref, w_ref, acc_ref, o_ref, wbf_ref):
    @pl.when(pl.program_id(1) == 0)
    def _():
        wbf_ref[...] = w_ref[...].astype(BF16)

    o_ref[...] = acc_ref[...] + _dot(x_ref[...], wbf_ref[...])


def matmul_acc(x, w, layer, acc, bm, bn):
    M, K = x.shape
    N = acc.shape[1]
    tile = pl.BlockSpec((bm, bn), lambda j, i: (i, j))
    return pl.pallas_call(
        _mm_acc_body,
        grid=(N // bn, M // bm),
        in_specs=[pl.BlockSpec((bm, K), lambda j, i: (i, 0)),
                  pl.BlockSpec((None, K, bn), lambda j, i: (layer, 0, j)), tile],
        out_specs=tile,
        out_shape=jax.ShapeDtypeStruct((M, N), F32),
        scratch_shapes=[pltpu.VMEM((K, bn), BF16)],
        compiler_params=_cparams("parallel", "arbitrary"),
        name="matmul_acc",
    )(x, w, acc)


def _mm_nt_body(x_ref, wt_ref, o_ref, wbf_ref):
    @pl.when(pl.program_id(1) == 0)
    def _():
        wbf_ref[...] = wt_ref[...].astype(BF16)

    o_ref[...] = _dot_nt(x_ref[...], wbf_ref[...]).astype(o_ref.dtype)


def matmul_nt(x, wt, layer, n_cols, out_dtype, bm, bn):
    M, K = x.shape
    return pl.pallas_call(
        _mm_nt_body,
        grid=(n_cols // bn, M // bm),
        in_specs=[pl.BlockSpec((bm, K), lambda j, i: (i, 0)),
                  pl.BlockSpec((None, bn, K), lambda j, i: (layer, j, 0))],
        out_specs=pl.BlockSpec((bm, bn), lambda j, i: (i, j)),
        out_shape=jax.ShapeDtypeStruct((M, n_cols), out_dtype),
        scratch_shapes=[pltpu.VMEM((bn, K), BF16)],
        compiler_params=_cparams("parallel", "arbitrary"),
        name="matmul_nt",
    )(x, wt)


MM_TAIL = 128


def _mm_nt_shifted_body(x_ref, wa_ref, wb_ref, o_ref, wbf_ref, *, shift):
    @pl.when(pl.program_id(1) == 0)
    def _():
        keep = wa_ref.shape[0] - shift
        wbf_ref[:keep, :] = wa_ref[shift:, :].astype(BF16)
        wbf_ref[keep:, :] = wb_ref[:shift, :].astype(BF16)

    o_ref[...] = _dot_nt(x_ref[...], wbf_ref[...]).astype(o_ref.dtype)


def matmul_nt_shifted(x, wt, layer, row0, n_cols, out_dtype, bm, bn):
    M, K = x.shape
    base, shift = divmod(row0, MM_TAIL)
    assert (base * MM_TAIL) % bn == 0 and 0 < shift and shift % 16 == 0
    ja = base * MM_TAIL // bn
    per = bn // MM_TAIL
    return pl.pallas_call(
        functools.partial(_mm_nt_shifted_body, shift=shift),
        grid=(n_cols // bn, M // bm),
        in_specs=[pl.BlockSpec((bm, K), lambda j, i: (i, 0)),
                  pl.BlockSpec((None, bn, K), lambda j, i: (layer, ja + j, 0)),
                  pl.BlockSpec((None, MM_TAIL, K), lambda j, i: (layer, base + per * (j + 1), 0))],
        out_specs=pl.BlockSpec((bm, bn), lambda j, i: (i, j)),
        out_shape=jax.ShapeDtypeStruct((M, n_cols), out_dtype),
        scratch_shapes=[pltpu.VMEM((bn, K), BF16)],
        compiler_params=_cparams("parallel", "arbitrary"),
        name="matmul_nt_shifted",
    )(x, wt, wt)


GDN_ROWS = 256
GDN_HB = 8
GDN_PREV = 16


def _gdn_gates_body(ab_ref, alog_ref, dtb_ref, o_ref):
    ab = ab_ref[...]
    lane = lax.broadcasted_iota(jnp.int32, ab.shape, 1)
    g = -jnp.exp(alog_ref[...]) * _softplus(ab + dtb_ref[...])
    o_ref[...] = jnp.where(lane < A_HEADS, g, _sigmoid(ab))


def gdn_gates(p_s, a_log, dt_bias, bm=512):
    S = p_s.shape[0]
    pad = lambda a: jnp.pad(a.astype(F32), (0, LANES - A_HEADS)).reshape(1, LANES)
    vec = pl.BlockSpec((1, LANES), lambda i: (0, 0))
    return pl.pallas_call(
        _gdn_gates_body,
        grid=(S // bm,),
        in_specs=[pl.BlockSpec((bm, LANES), lambda i: (i, PS_AB // LANES)), vec, vec],
        out_specs=pl.BlockSpec((bm, LANES), lambda i: (i, 0)),
        out_shape=jax.ShapeDtypeStruct((S, LANES), F32),
        compiler_params=_cparams("parallel"),
        name="gdn_gates",
    )(p_s, pad(a_log), pad(dt_bias))


def _gdn_body(q_ref, k_ref, v_ref, qp_ref, kp_ref, vp_ref, z_ref, gb_ref, cwq_ref, cwk_ref, cwv_ref,
              nw_ref, o_ref, state_ref):
    hb = pl.program_id(0)
    i = pl.program_id(1)
    R = GDN_ROWS
    C = A_CHUNK
    NC = R // C

    @pl.when(i == 0)
    def _():
        state_ref[...] = jnp.zeros_like(state_ref)

    has_prev = (i > 0).astype(F32)
    lane = lax.broadcasted_iota(jnp.int32, (R, LANES), 1)
    row = lax.broadcasted_iota(jnp.int32, (R, LANES), 0)
    rin = row & (C - 1)
    ri = lax.broadcasted_iota(jnp.int32, (R, R), 0)
    ci = lax.broadcasted_iota(jnp.int32, (R, R), 1)
    same = (ri // C) == (ci // C)
    incl = same & (ri >= ci)
    strict = same & (ri > ci)

    g_all = beta_all = gb_ref[...]
    nw = nw_ref[...]

    def conv_silu(cur_ref, prev_ref, cw_ref, sl):
        cur = cur_ref[:, sl].astype(F32)
        prev = prev_ref[:, sl].astype(F32) * has_prev
        full = jnp.concatenate([prev, cur], axis=0)
        cw = cw_ref[:, sl]
        y = cw[A_CONV - 1:A_CONV, :] * cur
        for s in range(1, A_CONV):
            y = y + cw[A_CONV - 1 - s:A_CONV - s, :] * pltpu.roll(full, s, axis=0)[GDN_PREV:, :]
        return _silu(y)

    heads = range(GDN_HB)
    lanes_of = [slice(hh * LANES, (hh + 1) * LANES) for hh in heads]
    st = [dict() for _ in heads]
    for hh, t in enumerate(st):
        h = hb * GDN_HB + hh
        g = jnp.broadcast_to(jnp.sum(jnp.where(lane == h, g_all, 0.0), axis=1, keepdims=True), (R, LANES))
        t["beta"] = jnp.broadcast_to(
            jnp.sum(jnp.where(lane == h + A_HEADS, beta_all, 0.0), axis=1, keepdims=True), (R, LANES))
        gc = g
        s = 1
        while s < C:
            gc = gc + jnp.where(rin >= s, pltpu.roll(gc, s, axis=0), 0.0)
            s *= 2
        t["g_last"] = jnp.broadcast_to(jnp.sum(g.reshape(NC, C, LANES), axis=1, keepdims=True),
                                       (NC, C, LANES)).reshape(R, LANES)
        t["gc"] = gc
        t["eg"] = jnp.exp(gc)
        gc_t = gc.T
        diff = jnp.concatenate([gc, gc], axis=1) - jnp.concatenate([gc_t, gc_t], axis=0)
        t["decay"] = jnp.exp(jnp.where(incl, diff, NEG))
    for hh, t in enumerate(st):
        sl = lanes_of[hh]
        q = conv_silu(q_ref, qp_ref, cwq_ref, sl)
        k = conv_silu(k_ref, kp_ref, cwk_ref, sl)
        t["v"] = conv_silu(v_ref, vp_ref, cwv_ref, sl)
        t["qn"] = q * lax.rsqrt(jnp.sum(q * q, axis=-1, keepdims=True) + EPS) * (A_DK ** -0.5)
        t["kn"] = k * lax.rsqrt(jnp.sum(k * k, axis=-1, keepdims=True) + EPS)
    for t in st:
        kn, qn, beta, eg, decay = t["kn"], t["qn"], t["beta"], t["eg"], t["decay"]
        kb = kn * beta
        kn_b = kn.astype(BF16)
        lower = jnp.where(strict, _dot_nt(kb.astype(BF16), kn_b) * decay, 0.0)
        t["xp"] = t["qmat"] = -lower
        t["qk"] = (_dot_nt(qn.astype(BF16), kn_b) * decay).astype(BF16)
        t["rhs"] = jnp.concatenate([t["v"] * beta, kb * eg], axis=1)
        t["q_dec"] = (qn * eg).astype(BF16)
        t["k_dec"] = kn * jnp.exp(t["g_last"] - t["gc"])

    for _ in range(int(math.log2(C)) - 1):
        for t in st:
            xb = t["xp"].astype(BF16)
            t["xp"] = _dot(xb, xb)
            t["qmat"] = t["qmat"] + t["xp"] + _dot(t["qmat"].astype(BF16), t["xp"].astype(BF16))
    for t in st:
        sol = t["rhs"] + _dot(t["qmat"].astype(BF16), t["rhs"].astype(BF16))
        t["u"] = sol[:, :A_DV]
        t["w_b"] = sol[:, A_DV:].astype(BF16)
        t["outs"] = []

    states = [state_ref[hh] for hh in heads]
    for c in range(NC):
        rs = slice(c * C, (c + 1) * C)
        for hh, t in enumerate(st):
            sb = states[hh].astype(BF16)
            v_new_b = (t["u"][rs] - _dot(t["w_b"][rs], sb)).astype(BF16)
            t["outs"].append(_dot(t["q_dec"][rs], sb) + _dot(t["qk"][rs, :][:, rs], v_new_b))
            states[hh] = (states[hh] * jnp.exp(t["g_last"][c * C:c * C + 1, :])
                          + _dot(t["k_dec"][rs].T.astype(BF16), v_new_b))
    for hh, t in enumerate(st):
        sl = lanes_of[hh]
        state_ref[hh] = states[hh]
        out = jnp.concatenate(t["outs"], axis=0)
        out = out * lax.rsqrt(jnp.mean(out * out, axis=-1, keepdims=True) + EPS) * nw
        o_ref[:, sl] = (out * _silu(z_ref[:, sl].astype(F32))).astype(o_ref.dtype)


def gdn_branch(p_a, p_s, conv_w, a_log, dt_bias, norm_w):
    S = p_a.shape[0]
    R, W = GDN_ROWS, GDN_HB * LANES
    nq = A_QK_W // W
    gb = gdn_gates(p_s, a_log, dt_bias)

    def cur(off):
        return pl.BlockSpec((R, W), lambda hb, i: (i, off + hb))

    def prev(off):
        return pl.BlockSpec((GDN_PREV, W), lambda hb, i: (jnp.maximum(i * (R // GDN_PREV) - 1, 0), off + hb))

    def cw(off):
        return pl.BlockSpec((A_CONV, W), lambda hb, i: (0, off + hb))

    vec = pl.BlockSpec((1, LANES), lambda hb, i: (0, 0))
    return pl.pallas_call(
        _gdn_body,
        grid=(A_HEADS // GDN_HB, S // R),
        in_specs=[cur(0), cur(nq), cur(2 * nq), prev(0), prev(nq), prev(2 * nq), cur(3 * nq),
                  pl.BlockSpec((R, LANES), lambda hb, i: (i, 0)),
                  cw(0), cw(nq), cw(2 * nq), vec],
        out_specs=pl.BlockSpec((R, W), lambda hb, i: (i, hb)),
        out_shape=jax.ShapeDtypeStruct((S, A_V_W), BF16),
        scratch_shapes=[pltpu.VMEM((GDN_HB, A_DK, A_DV), F32)],
        compiler_params=_cparams("parallel", "arbitrary"),
        name="gdn",
    )(p_a, p_a, p_a, p_a, p_a, p_a, p_a, gb, conv_w, conv_w, conv_w, norm_w.astype(F32).reshape(1, LANES))


MLA_HG = 4


def _mla_prep_body(cq_ref, ckv_ref, kr_ref, qn_ref, kvn_ref, wq_ref, wkv_ref, cs_ref, qo_ref, ko_ref, vo_ref):
    scale = (B_NOPE + B_ROPE) ** -0.5 * math.log2(math.e)
    cq = cq_ref[...]
    cq = (cq * lax.rsqrt(jnp.mean(cq * cq, axis=-1, keepdims=True) + EPS) * qn_ref[...]).astype(BF16)
    ckv = ckv_ref[...]
    ckv = (ckv * lax.rsqrt(jnp.mean(ckv * ckv, axis=-1, keepdims=True) + EPS) * kvn_ref[...]).astype(BF16)
    rq = _dot(cq, wq_ref[...])
    rkv = _dot(ckv, wkv_ref[...])
    cs = cs_ref[...]
    lane = lax.broadcasted_iota(jnp.int32, cs.shape, 1)

    def rope(pair):
        prod = pair * cs
        return jnp.where(lane < B_ROPE, prod + pltpu.roll(prod, B_ROPE, axis=1), 0.0)

    k_pe = rope(kr_ref[...]).astype(BF16)
    for hh in range(MLA_HG):
        qb = hh * B_QH
        qo_ref[:, hh * B_KH:hh * B_KH + LANES] = (rq[:, qb:qb + LANES] * scale).astype(BF16)
        qo_ref[:, hh * B_KH + LANES:(hh + 1) * B_KH] = (rope(rq[:, qb + LANES:qb + 2 * LANES]) * scale).astype(BF16)
        ko_ref[:, hh * B_KH:hh * B_KH + LANES] = rkv[:, hh * 2 * LANES:hh * 2 * LANES + LANES].astype(BF16)
        ko_ref[:, hh * B_KH + LANES:(hh + 1) * B_KH] = k_pe
        vo_ref[:, hh * 2 * LANES:hh * 2 * LANES + LANES] = (
            rkv[:, hh * 2 * LANES + LANES:(hh + 1) * 2 * LANES].astype(BF16))
        vo_ref[:, hh * 2 * LANES + LANES:(hh + 1) * 2 * LANES] = jnp.ones((rkv.shape[0], LANES), BF16)


def mla_prep(p_s, q_norm, kv_norm, wq_ext, wkv, cs_b, bm=512):
    S = p_s.shape[0]
    HG = MLA_HG
    return pl.pallas_call(
        _mla_prep_body,
        grid=(B_HEADS // HG, S // bm),
        in_specs=[
            pl.BlockSpec((bm, B_Q_LORA), lambda g, i: (i, PS_CQ // B_Q_LORA)),
            pl.BlockSpec((bm, B_KV_LORA), lambda g, i: (i, PS_CKV // B_KV_LORA)),
            pl.BlockSpec((bm, LANES), lambda g, i: (i, PS_KR // LANES)),
            pl.BlockSpec((1, B_Q_LORA), lambda g, i: (0, 0)),
            pl.BlockSpec((1, B_KV_LORA), lambda g, i: (0, 0)),
            pl.BlockSpec((B_Q_LORA, HG * B_QH), lambda g, i: (0, g)),
            pl.BlockSpec((B_KV_LORA, HG * 2 * LANES), lambda g, i: (0, g)),
            pl.BlockSpec((bm, LANES), lambda g, i: (i, 0)),
        ],
        out_specs=[
            pl.BlockSpec((bm, HG * B_KH), lambda g, i: (i, g)),
            pl.BlockSpec((bm, HG * B_KH), lambda g, i: (i, g)),
            pl.BlockSpec((bm, HG * 2 * LANES), lambda g, i: (i, g)),
        ],
        out_shape=[
            jax.ShapeDtypeStruct((S, B_HEADS * B_KH), BF16),
            jax.ShapeDtypeStruct((S, B_HEADS * B_KH), BF16),
            jax.ShapeDtypeStruct((S, B_HEADS * 2 * B_VDIM), BF16),
        ],
        compiler_params=_cparams("parallel", "parallel"),
        name="mla_prep",
    )(p_s, p_s, p_s, q_norm.reshape(1, -1), kv_norm.reshape(1, -1), wq_ext, wkv, cs_b)


FLASH_BLK = 4096
FLASH_RC = 256


def _flash_body(qi_ref, kj_ref, q_ref, k_ref, v_ref, o_ref, m_ref, acc_ref):
    t = pl.program_id(1)
    qi = qi_ref[t]
    kj = kj_ref[t]

    @pl.when(kj == 0)
    def _():
        m_ref[...] = jnp.full_like(m_ref, NEG)
        acc_ref[...] = jnp.zeros_like(acc_ref)

    def step(diag):
        nc = FLASH_BLK // FLASH_RC
        rows_of = [slice(c * FLASH_RC, (c + 1) * FLASH_RC) for c in range(nc)]
        ncol_of = [(c + 1) * FLASH_RC if diag else FLASH_BLK for c in range(nc)]

        def scores(c):
            s = _dot_nt(q_ref[rows_of[c], :], k_ref[:ncol_of[c], :])
            if diag:
                r = lax.broadcasted_iota(jnp.int32, s.shape, 0) + c * FLASH_RC
                col = lax.broadcasted_iota(jnp.int32, s.shape, 1)
                s = jnp.where(col <= r, s, NEG)
            return s

        s_next = scores(0)
        for c in range(nc):
            s = s_next
            if c + 1 < nc:
                s_next = scores(c + 1)
            rows = rows_of[c]
            m_prev = m_ref[rows, :]
            m_new = jnp.maximum(m_prev, jnp.max(s, axis=1, keepdims=True))
            p = jnp.exp2(s - m_new).astype(BF16)
            acc_ref[rows, :] = jnp.exp2(m_prev - m_new) * acc_ref[rows, :] + _dot(p, v_ref[:ncol_of[c], :])
            m_ref[rows, :] = m_new

    @pl.when(kj < qi)
    def _():
        step(False)

    @pl.when(kj == qi)
    def _():
        step(True)
        acc = acc_ref[...]
        o_ref[...] = (acc[:, :B_VDIM] / acc[:, B_VDIM:]).astype(o_ref.dtype)


def mla_attention(q_cat, k_cat, v_ext):
    S = q_cat.shape[0]
    blk = FLASH_BLK
    n = S // blk
    pairs = [(qi, kj) for qi in range(n) for kj in range(qi + 1)]
    qi_tab = jnp.asarray([p[0] for p in pairs], jnp.int32)
    kj_tab = jnp.asarray([p[1] for p in pairs], jnp.int32)
    return pl.pallas_call(
        _flash_body,
        grid_spec=pltpu.PrefetchScalarGridSpec(
            num_scalar_prefetch=2,
            grid=(B_HEADS, len(pairs)),
            in_specs=[
                pl.BlockSpec((blk, B_KH), lambda h, t, qi, kj: (qi[t], h)),
                pl.BlockSpec((blk, B_KH), lambda h, t, qi, kj: (kj[t], h)),
                pl.BlockSpec((blk, 2 * B_VDIM), lambda h, t, qi, kj: (kj[t], h)),
            ],
            out_specs=pl.BlockSpec((blk, B_VDIM), lambda h, t, qi, kj: (qi[t], h)),
            scratch_shapes=[pltpu.VMEM((blk, 1), F32), pltpu.VMEM((blk, 2 * B_VDIM), F32)],
        ),
        out_shape=jax.ShapeDtypeStruct((S, B_HEADS * B_VDIM), BF16),
        compiler_params=_cparams("parallel", "arbitrary"),
        name="mla_flash",
    )(qi_tab, kj_tab, q_cat, k_cat, v_ext)


DIL_BLK = 128
DIL_ROWS = 256
DILATIONS = tuple(d for _, d in C_PATTERNS)


def _rope_split_body(q_ref, k_ref, v_ref, c_ref, s_ref, *refs):
    nd = len(DILATIONS)
    outs, (sq, sk, sv) = refs[:3 * nd], refs[3 * nd:]
    c = c_ref[...]
    s = s_ref[...]
    scale = C_DH ** -0.5
    for h in range(C_HEADS):
        sl = slice(h * C_DH, (h + 1) * C_DH)
        x = q_ref[:, sl].astype(F32)
        sq[h] = (x * c + pltpu.roll(x, C_DH // 2, axis=1) * s) * scale
        x = k_ref[:, sl].astype(F32)
        sk[h] = x * c + pltpu.roll(x, C_DH // 2, axis=1) * s
        sv[h] = v_ref[:, sl].astype(F32)
        for di, d in enumerate(DILATIONS):
            for src, dst in zip((sq, sk, sv), outs[3 * di:3 * di + 3]):
                for r in range(d):
                    rows = src[h] if d == 1 else src[h, pl.ds(r, DIL_ROWS // d, stride=d), :]
                    dst[r, :, sl] = rows.astype(BF16)


def rope_split(p_r, cos_c, sin_c):
    S = p_r.shape[0]
    bm = DIL_ROWS
    tab = pl.BlockSpec((bm, C_DH), lambda i: (i, 0))
    col = lambda j: pl.BlockSpec((bm, C_W), lambda i: (i, j))
    out_specs, out_shape = [], []
    for d in DILATIONS:
        out_specs += [pl.BlockSpec((d, bm // d, C_W), lambda i: (0, i, 0))] * 3
        out_shape += [jax.ShapeDtypeStruct((d, S // d, C_W), BF16)] * 3
    outs = pl.pallas_call(
        _rope_split_body,
        grid=(S // bm,),
        in_specs=[col(0), col(1), col(2), tab, tab],
        out_specs=out_specs,
        out_shape=out_shape,
        scratch_shapes=[pltpu.VMEM((C_HEADS, bm, C_DH), F32)] * 3,
        compiler_params=_cparams("parallel"),
        name="rope_split",
    )(p_r, p_r, p_r, cos_c, sin_c)
    return [outs[3 * i:3 * i + 3] for i in range(len(DILATIONS))]


def _dilated_body(q_ref, kc_ref, vc_ref, o_ref, lse_ref, kp_ref, vp_ref):
    B = DIL_BLK
    n = pl.program_id(1)

    @pl.when(n == 0)
    def _():
        kp_ref[...] = jnp.zeros_like(kp_ref)
        vp_ref[...] = jnp.zeros_like(vp_ref)

    qi = lax.broadcasted_iota(jnp.int32, (B, 2 * B), 0)
    kj = lax.broadcasted_iota(jnp.int32, (B, 2 * B), 1)
    dist = qi + B - kj
    valid = (dist >= 0) & (dist <= B) & ((kj >= B) | (n > 0))
    lanes_of = [slice(h * C_DH, (h + 1) * C_DH) for h in range(C_HEADS)]
    scores = [jnp.where(valid, _dot_nt(q_ref[:, sl], jnp.concatenate([kp_ref[:, sl], kc_ref[:, sl]], axis=0)), NEG)
              for sl in lanes_of]
    probs = []
    lane = lax.broadcasted_iota(jnp.int32, (B, LANES), 1)
    lse = jnp.zeros((B, LANES), F32)
    for h, s in enumerate(scores):
        m = jnp.max(s, axis=1, keepdims=True)
        p = jnp.exp(s - m)
        l = jnp.sum(p, axis=1, keepdims=True)
        lse = jnp.where(lane == h, m + jnp.log(l), lse)
        probs.append((p.astype(BF16), l))
    lse_ref[...] = lse
    for sl, (p, l) in zip(lanes_of, probs):
        vv = jnp.concatenate([vp_ref[:, sl], vc_ref[:, sl]], axis=0)
        o_ref[:, sl] = (_dot(p, vv) / l).astype(o_ref.dtype)
    kp_ref[...] = kc_ref[...]
    vp_ref[...] = vc_ref[...]


def dilated_pattern(q, k, v):
    d, L, W = q.shape
    B = DIL_BLK
    cur = pl.BlockSpec((None, B, W), lambda r, n: (r, n, 0))
    return pl.pallas_call(
        _dilated_body,
        grid=(d, L // B),
        in_specs=[cur, cur, cur],
        out_specs=[cur, pl.BlockSpec((None, B, LANES), lambda r, n: (r, n, 0))],
        out_shape=[jax.ShapeDtypeStruct((d, L, W), BF16), jax.ShapeDtypeStruct((d, L, LANES), F32)],
        scratch_shapes=[pltpu.VMEM((B, W), BF16)] * 2,
        compiler_params=_cparams("arbitrary", "arbitrary"),
        name=f"dilated_d{d}",
    )(q, k, v)


def _dilated_merge_body(*refs):
    nd = len(DILATIONS)
    o_refs, lse_refs, out_ref = refs[:nd], refs[nd:2 * nd], refs[2 * nd]
    so, sl = refs[2 * nd + 1:]
    rows_of = lambda d, r: slice(None) if d == 1 else pl.ds(r, DIL_ROWS // d, stride=d)
    for di, d in enumerate(DILATIONS):
        for r in range(d):
            sl[di, rows_of(d, r), :] = lse_refs[di][r]
    top = sl[0]
    for di in range(1, nd):
        top = jnp.maximum(top, sl[di])
    weights = [jnp.exp(sl[di] - top) for di in range(nd)]
    lane = lax.broadcasted_iota(jnp.int32, top.shape, 1)
    for h in range(C_HEADS):
        cols = slice(h * C_DH, (h + 1) * C_DH)
        for di, d in enumerate(DILATIONS):
            for r in range(d):
                so[di, rows_of(d, r), :] = o_refs[di][r, :, cols].astype(F32)
        num = 0.0
        den = 0.0
        for di in range(nd):
            w = jnp.sum(jnp.where(lane == h, weights[di], 0.0), axis=1, keepdims=True)
            num = num + w * so[di]
            den = den + w
        out_ref[:, cols] = (num / den).astype(out_ref.dtype)


def dilated_merge(outs, lses):
    S = outs[0].shape[0] * outs[0].shape[1]
    bm = DIL_ROWS
    nd = len(DILATIONS)
    specs = [pl.BlockSpec((d, bm // d, C_W), lambda i: (0, i, 0)) for d in DILATIONS]
    lse_specs = [pl.BlockSpec((d, bm // d, LANES), lambda i: (0, i, 0)) for d in DILATIONS]
    return pl.pallas_call(
        _dilated_merge_body,
        grid=(S // bm,),
        in_specs=specs + lse_specs,
        out_specs=pl.BlockSpec((bm, C_W), lambda i: (i, 0)),
        out_shape=jax.ShapeDtypeStruct((S, C_W), BF16),
        scratch_shapes=[pltpu.VMEM((nd, bm, C_DH), F32)] * 2,
        compiler_params=_cparams("parallel"),
        name="dilated_merge",
    )(*outs, *lses)


def dilated_branch(p_r, cos_c, sin_c):
    assert all(w // d == DIL_BLK for w, d in C_PATTERNS)
    results = [dilated_pattern(q, k, v) for q, k, v in rope_split(p_r, cos_c, sin_c)]
    return dilated_merge([r[0] for r in results], [r[1] for r in results])


def _merge_body(oa_ref, ob_ref, oc_ref, ga_ref, gb_ref, gc_ref, wa_ref, wb_ref, wc_ref, o_ref, wa_s, wb_s, wc_s):
    @pl.when(pl.program_id(1) == 0)
    def _():
        wa_s[...] = wa_ref[...].astype(BF16)
        wb_s[...] = wb_ref[...].astype(BF16)
        wc_s[...] = wc_ref[...].astype(BF16)

    gates = [_sigmoid(g[...].astype(F32)) for g in (ga_ref, gb_ref, gc_ref)]
    ya = _dot(oa_ref[...], wa_s[...])
    yb = _dot(ob_ref[...], wb_s[...])
    yc = _dot(oc_ref[...], wc_s[...])
    o_ref[...] = (gates[0] * ya + gates[1] * yb + gates[2] * yc).astype(o_ref.dtype)


def merge_branches(o_a, o_b, o_c, p_r, w_a, w_b, w_c, layer, bm=512, bn=512):
    S, K = o_a.shape
    g0 = 3 * C_W // bn
    gstep = D_MODEL // bn
    act = pl.BlockSpec((bm, K), lambda j, i: (i, 0))
    wsp = pl.BlockSpec((None, K, bn), lambda j, i: (layer, 0, j))
    gate = lambda b: pl.BlockSpec((bm, bn), lambda j, i: (i, g0 + b * gstep + j))
    return pl.pallas_call(
        _merge_body,
        grid=(D_MODEL // bn, S // bm),
        in_specs=[act, act, act, gate(0), gate(1), gate(2), wsp, wsp, wsp],
        out_specs=pl.BlockSpec((bm, bn), lambda j, i: (i, j)),
        out_shape=jax.ShapeDtypeStruct((S, D_MODEL), BF16),
        scratch_shapes=[pltpu.VMEM((K, bn), BF16)] * 3,
        compiler_params=_cparams("parallel", "arbitrary"),
        name="merge_branches",
    )(o_a, o_b, o_c, p_r, p_r, p_r, w_a, w_b, w_c)


def _new_group(tg_ref):
    i = pl.program_id(1)
    return (i == 0) | (tg_ref[i] != tg_ref[jnp.maximum(i - 1, 0)])


def _swiglu_up_body(tg_ref, nt_ref, x_ref, w1_ref, w3_ref, o_ref, w1_s, w3_s):
    @pl.when(_new_group(tg_ref))
    def _():
        w1_s[...] = w1_ref[...].astype(BF16)
        w3_s[...] = w3_ref[...].astype(BF16)

    @pl.when(pl.program_id(1) < nt_ref[0])
    def _():
        x = x_ref[...]
        o_ref[...] = (_silu(_dot(x, w1_s[...])) * _dot(x, w3_s[...])).astype(o_ref.dtype)

    @pl.when(pl.program_id(1) >= nt_ref[0])
    def _():
        o_ref[...] = jnp.zeros_like(o_ref)


def swiglu_up(x, w1, w3, tile_group, n_tiles, bm, bf):
    R, D = x.shape
    F = w1.shape[2]
    wspec = pl.BlockSpec((None, D, bf), lambda f, i, tg, nt: (tg[i], 0, f))
    return pl.pallas_call(
        _swiglu_up_body,
        grid_spec=pltpu.PrefetchScalarGridSpec(
            num_scalar_prefetch=2,
            grid=(F // bf, R // bm),
            in_specs=[pl.BlockSpec((bm, D), lambda f, i, tg, nt: (jnp.minimum(i, nt[0] - 1), 0)), wspec, wspec],
            out_specs=pl.BlockSpec((bm, bf), lambda f, i, tg, nt: (i, f)),
            scratch_shapes=[pltpu.VMEM((D, bf), BF16)] * 2,
        ),
        out_shape=jax.ShapeDtypeStruct((R, F), BF16),
        compiler_params=_cparams("arbitrary", "arbitrary"),
        name="swiglu_up",
    )(tile_group, n_tiles, x, w1, w3)


def _grouped_mm_body(tg_ref, nt_ref, x_ref, w_ref, *rest, has_acc):
    if has_acc:
        acc_ref, o_ref, w_s = rest
    else:
        o_ref, w_s = rest

    @pl.when(_new_group(tg_ref))
    def _():
        w_s[...] = w_ref[...].astype(BF16)

    @pl.when(pl.program_id(1) < nt_ref[0])
    def _():
        y = _dot(x_ref[...], w_s[...])
        o_ref[...] = acc_ref[...] + y if has_acc else y

    @pl.when(pl.program_id(1) >= nt_ref[0])
    def _():
        o_ref[...] = jnp.zeros_like(o_ref)


def grouped_matmul(x, w, tile_group, n_tiles, bm, bn, k_blocks=1, k_index=0, acc=None):
    R, K = x.shape
    N = w.shape[2]
    kb = K // k_blocks
    in_specs = [pl.BlockSpec((bm, kb), lambda j, i, tg, nt: (jnp.minimum(i, nt[0] - 1), k_index)),
                pl.BlockSpec((None, kb, bn), lambda j, i, tg, nt: (tg[i], k_index, j))]
    out_spec = pl.BlockSpec((bm, bn), lambda j, i, tg, nt: (i, j))
    args = [tile_group, n_tiles, x, w]
    if acc is not None:
        in_specs.append(out_spec)
        args.append(acc)
    return pl.pallas_call(
        functools.partial(_grouped_mm_body, has_acc=acc is not None),
        grid_spec=pltpu.PrefetchScalarGridSpec(
            num_scalar_prefetch=2,
            grid=(N // bn, R // bm),
            in_specs=in_specs,
            out_specs=out_spec,
            scratch_shapes=[pltpu.VMEM((kb, bn), BF16)],
        ),
        out_shape=jax.ShapeDtypeStruct((R, N), F32),
        input_output_aliases={4: 0} if acc is not None else {},
        compiler_params=_cparams("arbitrary", "arbitrary"),
        name="grouped_matmul",
    )(*args)


DENSE_UP_BM = 1024
DENSE_DOWN_BM = 512
DENSE_KSPLIT = 2


def _one_group(n_rows, bm, group):
    return jnp.full((n_rows // bm,), group, jnp.int32), jnp.full((1,), n_rows // bm, jnp.int32)


def dense_ffn(h, w1, w3, w2, index, x_res):
    S = h.shape[0]
    g = swiglu_up(h, w1, w3, *_one_group(S, DENSE_UP_BM, index), bm=DENSE_UP_BM, bf=256)
    y = x_res
    for k in range(DENSE_KSPLIT):
        y = grouped_matmul(g, w2, *_one_group(S, DENSE_DOWN_BM, index), bm=DENSE_DOWN_BM, bn=512,
                           k_blocks=DENSE_KSPLIT, k_index=k, acc=y)
    return y


def _router_body(h_ref, rw_ref, sel_ref, gate_ref):
    logits = _dot(h_ref[...].astype(BF16), rw_ref[...].astype(BF16))
    lane = lax.broadcasted_iota(jnp.int32, logits.shape, 1)
    logits = jnp.where(lane < N_EXPERTS, logits, NEG)
    t1 = jnp.max(logits, axis=1, keepdims=True)
    i1 = jnp.min(jnp.where(logits == t1, lane, LANES), axis=1, keepdims=True)
    rest = jnp.where(lane == i1, NEG, logits)
    t2 = jnp.max(rest, axis=1, keepdims=True)
    i2 = jnp.min(jnp.where(rest == t2, lane, LANES), axis=1, keepdims=True)
    e2 = jnp.exp(t2 - t1)
    g1 = 1.0 / (1.0 + e2)
    sel_ref[...] = jnp.where(lane == 0, i1, jnp.where(lane == 1, i2, 0))
    gate_ref[...] = jnp.where(lane == 0, g1, jnp.where(lane == 1, e2 * g1, 0.0))


def router(h, router_w, bm=512):
    S, D = h.shape
    rw = jnp.pad(router_w, ((0, 0), (0, LANES - N_EXPERTS)))
    out = pl.BlockSpec((bm, LANES), lambda i: (i, 0))
    return pl.pallas_call(
        _router_body,
        grid=(S // bm,),
        in_specs=[pl.BlockSpec((bm, D), lambda i: (i, 0)), pl.BlockSpec((D, LANES), lambda i: (0, 0))],
        out_specs=[out, out],
        out_shape=[jax.ShapeDtypeStruct((S, LANES), jnp.int32), jax.ShapeDtypeStruct((S, LANES), F32)],
        compiler_params=_cparams("parallel"),
        name="router",
    )(h, rw)


MOE_BM = 512
MOE_BG = 256
DMA_QUEUES = 2


def _moe_plan(sel, bm):
    S = sel.shape[0]
    E = N_EXPERTS
    e = sel[:, :TOP_K].reshape(-1)
    onehot = (e[:, None] == jnp.arange(E, dtype=jnp.int32)[None, :]).astype(jnp.int32)
    csum = jnp.cumsum(onehot, axis=0)
    rank = jnp.sum((csum - 1) * onehot, axis=1)
    tiles_per = (csum[-1] + bm - 1) // bm
    tile_end = jnp.cumsum(tiles_per)
    pos = ((tile_end - tiles_per)[e] * bm + rank).astype(jnp.int32)
    n_tiles = tile_end[-1:].astype(jnp.int32)
    T = (TOP_K * S) // bm + E
    tid = jnp.arange(T, dtype=jnp.int32)
    te = jnp.minimum(jnp.sum((tid[:, None] >= tile_end[None, :]).astype(jnp.int32), axis=1), E - 1)
    te = jnp.where(tid < n_tiles[0], te, te[n_tiles[0] - 1]).astype(jnp.int32)
    src = jnp.zeros((T * bm,), jnp.int32).at[pos].set(jnp.arange(TOP_K * S, dtype=jnp.int32) // TOP_K)
    return pos, src, te, n_tiles


def _moe_gather_body(src_ref, slab_hbm, o_ref, buf_ref, sem):
    base = pl.program_id(0) * MOE_BG
    tiles = o_ref.shape[1] // LANES
    pitch = _slab_pitch(o_ref.shape[1])

    def row_copy(j):
        src_rows = pl.ds(src_ref[base + j] * pitch, tiles)
        return pltpu.make_async_copy(slab_hbm.at[src_rows, :], buf_ref.at[pl.ds(j * pitch, tiles), :], sem)

    def start(jj, c):
        for par in range(DMA_QUEUES):
            row_copy(jj * DMA_QUEUES + par).start(priority=par)
        return c

    def wait(j, c):
        row_copy(j).wait()
        return c

    lax.fori_loop(0, MOE_BG // DMA_QUEUES, start, 0)
    lax.fori_loop(0, MOE_BG, wait, 0)
    for c in range(tiles):
        o_ref[:, c * LANES:(c + 1) * LANES] = buf_ref[pl.ds(c, MOE_BG, stride=pitch), :].astype(o_ref.dtype)


def moe_gather(slab, src, d_model):
    R = src.shape[0]
    return pl.pallas_call(
        _moe_gather_body,
        grid_spec=pltpu.PrefetchScalarGridSpec(
            num_scalar_prefetch=1,
            grid=(R // MOE_BG,),
            in_specs=[pl.BlockSpec(memory_space=pl.ANY)],
            out_specs=pl.BlockSpec((MOE_BG, d_model), lambda i, src: (i, 0)),
            scratch_shapes=[pltpu.VMEM((MOE_BG * _slab_pitch(d_model), LANES), F32), pltpu.SemaphoreType.DMA(())],
        ),
        out_shape=jax.ShapeDtypeStruct((R, d_model), BF16),
        compiler_params=_cparams("arbitrary"),
        name="moe_gather",
    )(src, slab)


def _moe_combine_body(pos_ref, gate_ref, x_ref, g_ref, ys_hbm, *rest, final):
    if final:
        ho_ref, buf_ref, sem = rest
    else:
        xo_ref, ho_ref, buf_ref, sem = rest
    base = pl.program_id(0) * MOE_BG

    def row_copy(j, k):
        row = pos_ref[(base + j) * TOP_K + k]
        return pltpu.make_async_copy(ys_hbm.at[pl.ds(row, 1), :], buf_ref.at[k, pl.ds(j, 1), :], sem)

    def start(j, c):
        for k in range(TOP_K):
            row_copy(j, k).start(priority=k % DMA_QUEUES)
        return c

    def wait(j, c):
        for k in range(TOP_K):
            row_copy(j, k).wait()
        return c

    lax.fori_loop(0, MOE_BG, start, 0)
    lax.fori_loop(0, MOE_BG, wait, 0)
    gate = gate_ref[...]
    x = x_ref[...]
    for k in range(TOP_K):
        x = x + gate[:, k:k + 1] * buf_ref[k]
    if not final:
        xo_ref[...] = x
    y = x * lax.rsqrt(jnp.mean(x * x, axis=-1, keepdims=True) + EPS)
    ho_ref[...] = (y * g_ref[...]).astype(ho_ref.dtype)


def moe_combine(ys, pos, gate, x_res, g, final):
    S, D = x_res.shape
    row = pl.BlockSpec((MOE_BG, D), lambda i, pos: (i, 0))
    normed = jax.ShapeDtypeStruct((S, D), F32 if final else BF16)
    return pl.pallas_call(
        functools.partial(_moe_combine_body, final=final),
        grid_spec=pltpu.PrefetchScalarGridSpec(
            num_scalar_prefetch=1,
            grid=(S // MOE_BG,),
            in_specs=[pl.BlockSpec((MOE_BG, LANES), lambda i, pos: (i, 0)), row,
                      pl.BlockSpec((1, D), lambda i, pos: (0, 0)), pl.BlockSpec(memory_space=pl.ANY)],
            out_specs=row if final else [row, row],
            scratch_shapes=[pltpu.VMEM((TOP_K, MOE_BG, D), F32), pltpu.SemaphoreType.DMA(())],
        ),
        out_shape=normed if final else [jax.ShapeDtypeStruct((S, D), F32), normed],
        compiler_params=_cparams("arbitrary"),
        name="moe_combine",
    )(pos, gate, x_res, g.reshape(1, D), ys)


def moe_layer(h, h_slab, router_w, w1, w3, w2, index, x_res, next_norm, final):
    sel, gate = router(h, router_w)
    pos, src, tile_expert, n_tiles = _moe_plan(sel, MOE_BM)
    tile_expert = tile_expert + index * N_EXPERTS
    w1, w3, w2 = (w.reshape((-1,) + w.shape[2:]) for w in (w1, w3, w2))
    xs = moe_gather(h_slab, src, h.shape[1])
    g = swiglu_up(xs, w1, w3, tile_expert, n_tiles, bm=MOE_BM, bf=512)
    ys = grouped_matmul(g, w2, tile_expert, n_tiles, bm=MOE_BM, bn=1024)
    return moe_combine(ys, pos, gate, x_res, next_norm, final)


def _rot_cols(w):
    half = w.shape[-1] // 2
    return jnp.concatenate([-w[..., half:], w[..., :half]], axis=-1)


def _rope_tables(seq, dim):
    inv = ROPE_THETA ** (-jnp.arange(0, dim, 2, dtype=F32) / dim)
    ang = jnp.arange(seq, dtype=F32)[:, None] * inv[None, :]
    return jnp.cos(ang), jnp.sin(ang)


def _small_weight_t(wt_mid):
    row = lambda off, n: wt_mid[off - OFF_A_IN:off - OFF_A_IN + n]
    kr = row(OFF_KROPE, B_ROPE)
    parts = [row(OFF_CQ, B_Q_LORA), row(OFF_CKV, B_KV_LORA), kr, _rot_cols(kr.T).T,
             row(OFF_A_IN, A_HEADS), row(OFF_B_IN, A_HEADS)]
    w = jnp.concatenate(parts, axis=0)
    return jnp.pad(w, ((0, PS_W - w.shape[0]), (0, 0)))


def _mla_weights(w_uq, w_ukv):
    wq = w_uq.reshape(B_Q_LORA, B_HEADS, B_NOPE + B_ROPE)
    pe = wq[..., B_NOPE:]
    wq_ext = jnp.concatenate([wq[..., :B_NOPE], pe, _rot_cols(pe)], axis=-1)
    return wq_ext.reshape(B_Q_LORA, B_HEADS * B_QH).astype(BF16), w_ukv.astype(BF16)


def hybrid_mixer(h, x_res, layer, p, tables):
    cs_b, cos_c, sin_c = tables
    wt_in = jnp.swapaxes(p["w_in"], 1, 2)
    p_a = matmul_nt(h, wt_in, layer, OFF_A_IN, BF16, bm=1024, bn=512)
    p_s = matmul_nt(h, _small_weight_t(wt_in[layer, OFF_A_IN:OFF_REST])[None], 0, PS_W, F32, bm=1024, bn=256)
    p_r = matmul_nt_shifted(h, wt_in, layer, OFF_REST, REST_W, BF16, bm=1024, bn=512)
    o_a = gdn_branch(p_a, p_s, p["conv_w"][layer], p["gdn_a_log"][layer], p["gdn_dt_bias"][layer],
                     p["gdn_norm"][layer])
    wq_ext, wkv = _mla_weights(p["mla_w_uq"][layer], p["mla_w_ukv"][layer])
    q_cat, k_cat, v_b = mla_prep(p_s, p["mla_q_norm"][layer], p["mla_kv_norm"][layer], wq_ext, wkv, cs_b)
    o_b = mla_attention(q_cat, k_cat, v_b)
    o_c = dilated_branch(p_r, cos_c, sin_c)
    merged = merge_branches(o_a, o_b, o_c, p_r, p["w_branch_a"], p["w_branch_b"], p["w_branch_c"], layer)
    return matmul_acc(merged, p["w_out"], layer, x_res, bm=1024, bn=512)


def kernel(x, attn_norm, w_in, conv_w, gdn_a_log, gdn_dt_bias, gdn_norm, mla_q_norm, mla_w_uq, mla_kv_norm, mla_w_ukv, w_branch_a, w_branch_b, w_branch_c, w_out, ffn_norm, dense_w1, dense_w3, dense_w2, router_w, moe_w1, moe_w3, moe_w2, final_norm):
    p = dict(w_in=w_in, conv_w=conv_w, gdn_a_log=gdn_a_log, gdn_dt_bias=gdn_dt_bias, gdn_norm=gdn_norm,
             mla_q_norm=mla_q_norm, mla_w_uq=mla_w_uq, mla_kv_norm=mla_kv_norm, mla_w_ukv=mla_w_ukv,
             w_branch_a=w_branch_a, w_branch_b=w_branch_b, w_branch_c=w_branch_c, w_out=w_out)
    Bn, S, D = x.shape
    assert Bn == 1 and D == D_MODEL
    cos_b, sin_b = _rope_tables(S, B_ROPE)
    cos_c, sin_c = _rope_tables(S, C_DH)
    tables = (jnp.concatenate([cos_b, cos_b, sin_b, sin_b], axis=1),
              jnp.concatenate([cos_c, cos_c], axis=1), jnp.concatenate([-sin_c, sin_c], axis=1))
    xr = x.reshape(S, D)
    h = rmsnorm(xr, attn_norm[0], BF16)
    out = None
    for layer in range(DEPTH):
        last = layer + 1 == DEPTH
        xr = hybrid_mixer(h, xr, layer, p, tables)
        i = layer // 2
        if layer % 2 == 0:
            h = rmsnorm(xr, ffn_norm[layer], BF16)
            xr = dense_ffn(h, dense_w1, dense_w3, dense_w2, i, xr)
            if last:
                out = rmsnorm(xr, final_norm, F32)
            else:
                h = rmsnorm(xr, attn_norm[layer + 1], BF16)
        else:
            h, h_slab = rmsnorm_slab(xr, ffn_norm[layer])
            moe = functools.partial(moe_layer, h, h_slab, router_w[i], moe_w1, moe_w3, moe_w2, i, xr)
            if last:
                out = moe(final_norm, True)
            else:
                xr, h = moe(attn_norm[layer + 1], False)
    return out.reshape(Bn, S, D)
```
